```python
import jax, jax.numpy as jnp
from jax import lax
import numpy as np

D_MODEL = 1024
BATCH = 8
SEQ = 2048
DEPTH = 1
DEC_BATCH = 128
DEC_SEQ = 1
PAST_LEN = 16384
PAGE_SIZE = 128

N_META = 16
MIX_WIDTH = D_MODEL
HG_WIDTH = MIX_WIDTH // 2
HG_HEADS = 4
HG_DK = HG_WIDTH // HG_HEADS
HG_DV = HG_WIDTH // HG_HEADS
CV_WIDTH = MIX_WIDTH - HG_WIDTH
CONV_K = 31
D_FF = 4 * D_MODEL
CHUNK = 64
EPS = 1e-6
IN_COLS = 4 * HG_WIDTH + 2 * CV_WIDTH
IN_SPLITS = (HG_WIDTH, 2 * HG_WIDTH, 3 * HG_WIDTH, 4 * HG_WIDTH, 4 * HG_WIDTH + CV_WIDTH)

kernel_name = 'hymba_hgrn2_conformer_decode_step'


def _rmsnorm(x, g):
    xf = x.astype(jnp.float32)
    y = xf * lax.rsqrt(jnp.mean(xf * xf, axis=-1, keepdims=True) + EPS)
    return (y * g.astype(jnp.float32)).astype(x.dtype)


def _layernorm(x, g, b):
    xf = x.astype(jnp.float32)
    xc = xf - jnp.mean(xf, axis=-1, keepdims=True)
    y = xc * lax.rsqrt(jnp.mean(xc * xc, axis=-1, keepdims=True) + EPS)
    return (y * g.astype(jnp.float32) + b.astype(jnp.float32)).astype(x.dtype)


def _layer_lower_bound(lb_params, layer):
    p = jax.nn.softmax(lb_params.astype(jnp.float32), axis=0)
    return jnp.cumsum(p, axis=0)[layer]


def _hgrn2_chunked(q, k, v, logf, s0):
    bsz, t, h, _ = q.shape
    dv = v.shape[-1]
    n = t // CHUNK

    def blocks(a):
        return a.reshape(bsz, n, CHUNK, h, a.shape[-1]).transpose(0, 1, 3, 2, 4)

    q, k, v, logf = blocks(q), blocks(k), blocks(v), blocks(logf)
    b = jnp.cumsum(logf, axis=3)
    b_end = b[:, :, :, -1:, :]
    q_dec = q * jnp.exp(b)
    k_inv = k * jnp.exp(-b)
    k_end = k * jnp.exp(b_end - b)
    causal = jnp.tril(jnp.ones((CHUNK, CHUNK), dtype=bool))
    scores = jnp.where(causal, jnp.einsum('bnhtd,bnhsd->bnhts', q_dec, k_inv), 0.0)
    o_intra = jnp.einsum('bnhts,bnhsv->bnhtv', scores, v)
    ds = jnp.einsum('bnhsd,bnhsv->nbhdv', k_end, v)
    decay = jnp.exp(b_end[:, :, :, 0, :]).transpose(1, 0, 2, 3)

    def step(s, inp):
        dec, d = inp
        return dec[..., None] * s + d, s

    s_fin, s_start = lax.scan(step, s0, (decay, ds))
    o_inter = jnp.einsum('bnhtd,nbhdv->bnhtv', q_dec, s_start)
    o = (o_intra + o_inter).transpose(0, 1, 3, 2, 4).reshape(bsz, t, h, dv)
    return o, s_fin


def _hgrn2_recurrent(q, k, v, logf, s0):
    def step(s, inp):
        qt, kt, vt, lft = inp
        s = jnp.exp(lft)[..., None] * s + kt[..., :, None] * vt[..., None, :]
        return s, jnp.einsum('bhd,bhdv->bhv', qt, s)

    xs = tuple(a.transpose(1, 0, 2, 3) for a in (q, k, v, logf))
    s_fin, o = lax.scan(step, s0, xs)
    return o.transpose(1, 0, 2, 3), s_fin


def _layer(h, s0, buf, chunked, lb, norm1_g, w_in, hg_onorm_g, conv_w, conv_b, conv_ln_g, conv_ln_b,
           w_out, norm2_g, w_up, w_down):
    bsz, t, _ = h.shape
    f32 = jnp.float32
    hn = _rmsnorm(h, norm1_g)
    z = jnp.einsum('btd,dc->btc', hn, w_in)
    q_r, f_r, i_r, g_r, a_r, b_r = jnp.split(z, IN_SPLITS, axis=-1)

    f = lb + (1.0 - lb) * jax.nn.sigmoid(f_r.astype(f32))
    heads = lambda a: a.reshape(bsz, t, HG_HEADS, a.shape[-1] // HG_HEADS)
    q = heads(jax.nn.silu(q_r.astype(f32)))
    k = heads(1.0 - f)
    logf = heads(jnp.log(f))
    v = heads(i_r.astype(f32))
    s0 = s0.astype(f32)
    if chunked:
        front = (-N_META) % CHUNK
        back = (-(front + t)) % CHUNK
        pad = lambda a: jnp.pad(a, ((0, 0), (front, back), (0, 0), (0, 0)))
        o, s_new = _hgrn2_chunked(pad(q), pad(k), pad(v), pad(logf), s0)
        o = o[:, front:front + t]
    else:
        o, s_new = _hgrn2_recurrent(q, k, v, logf, s0)
    gate = heads(jax.nn.silu(g_r.astype(f32)))
    y_hg = (_rmsnorm(o, hg_onorm_g) * gate).reshape(bsz, t, HG_WIDTH).astype(h.dtype)

    glu = a_r * jax.nn.sigmoid(b_r)
    xcat = jnp.concatenate([buf.astype(glu.dtype), glu], axis=1)
    new_buf = xcat[:, -(CONV_K - 1):]
    dw = lax.conv_general_dilated(xcat, conv_w[:, None, :].astype(xcat.dtype), window_strides=(1,),
                                  padding='VALID', dimension_numbers=('NWC', 'WIO', 'NWC'),
                                  feature_group_count=CV_WIDTH) + conv_b
    y_cv = jax.nn.silu(_layernorm(dw, conv_ln_g, conv_ln_b)).astype(h.dtype)

    h = h + jnp.concatenate([y_hg, y_cv], axis=-1) @ w_out
    hn2 = _rmsnorm(h, norm2_g)
    h = h + jnp.square(jax.nn.relu(hn2 @ w_up)) @ w_down
    return h, s_new, new_buf


def setup_inputs(seed: int = 0) -> dict:
    key = jax.random.key(seed)
    ks = jax.random.split(key, 20)
    nrm = lambda k, shape, s: jax.random.normal(k, shape, jnp.float32) * s
    return {
        'x_prompt': nrm(ks[0], (BATCH, SEQ, D_MODEL), 1.0),
        'x_sample': nrm(ks[1], (DEC_BATCH, DEC_SEQ, D_MODEL), 1.0),
        'state_hgrn': nrm(ks[2], (DEPTH, DEC_BATCH, HG_HEADS, HG_DK, HG_DV), 0.3),
        'state_conv': nrm(ks[3], (DEPTH, DEC_BATCH, CONV_K - 1, CV_WIDTH), 0.5),
        'meta_tokens': nrm(ks[4], (N_META, D_MODEL), 1.0),
        'hg_lb': nrm(ks[5], (DEPTH + 1, HG_WIDTH), 0.1),
        'norm1_g': 1.0 + nrm(ks[6], (DEPTH, D_MODEL), 0.02),
        'w_in': nrm(ks[7], (DEPTH, D_MODEL, IN_COLS), D_MODEL ** -0.5),
        'hg_onorm_g': 1.0 + nrm(ks[8], (DEPTH, HG_DV), 0.02),
        'conv_w': nrm(ks[9], (DEPTH, CONV_K, CV_WIDTH), CONV_K ** -0.5),
        'conv_b': nrm(ks[10], (DEPTH, CV_WIDTH), 0.02),
        'conv_ln_g': 1.0 + nrm(ks[11], (DEPTH, CV_WIDTH), 0.02),
        'conv_ln_b': nrm(ks[12], (DEPTH, CV_WIDTH), 0.02),
        'w_out': nrm(ks[13], (DEPTH, MIX_WIDTH, D_MODEL), MIX_WIDTH ** -0.5),
        'norm2_g': 1.0 + nrm(ks[14], (DEPTH, D_MODEL), 0.02),
        'w_up': nrm(ks[15], (DEPTH, D_MODEL, D_FF), D_MODEL ** -0.5),
        'w_down': nrm(ks[16], (DEPTH, D_FF, D_MODEL), D_FF ** -0.5),
        'final_g': 1.0 + nrm(ks[17], (D_MODEL,), 0.02),
    }


def reference(x_prompt, x_sample, state_hgrn, state_conv, meta_tokens, hg_lb, norm1_g, w_in, hg_onorm_g,
              conv_w, conv_b, conv_ln_g, conv_ln_b, w_out, norm2_g, w_up, w_down, final_g):
    bp = x_prompt.shape[0]
    meta = jnp.broadcast_to(meta_tokens[None].astype(x_prompt.dtype), (bp, N_META, D_MODEL))
    hp = jnp.concatenate([meta, x_prompt], axis=1)
    hs = x_sample
    sp_list, cp_list, ss_list, cs_list = [], [], [], []
    for l in range(DEPTH):
        lb = _layer_lower_bound(hg_lb, l)
        w = (norm1_g[l], w_in[l], hg_onorm_g[l], conv_w[l], conv_b[l], conv_ln_g[l], conv_ln_b[l],
             w_out[l], norm2_g[l], w_up[l], w_down[l])
        s0_p = jnp.zeros((bp, HG_HEADS, HG_DK, HG_DV), jnp.float32)
        buf_p = jnp.zeros((bp, CONV_K - 1, CV_WIDTH), x_prompt.dtype)
        hp, s_p, c_p = _layer(hp, s0_p, buf_p, True, lb, *w)
        hs, s_s, c_s = _layer(hs, state_hgrn[l], state_conv[l], False, lb, *w)
        sp_list.append(s_p)
        cp_list.append(c_p)
        ss_list.append(s_s)
        cs_list.append(c_s)
    y_prompt = _rmsnorm(hp[:, N_META:], final_g)
    y_sample = _rmsnorm(hs, final_g)
    new_state_hgrn_prompt = jnp.stack(sp_list).astype(state_hgrn.dtype)
    new_state_conv_prompt = jnp.stack(cp_list).astype(state_conv.dtype)
    new_state_hgrn_sample = jnp.stack(ss_list).astype(state_hgrn.dtype)
    new_state_conv_sample = jnp.stack(cs_list).astype(state_conv.dtype)
    return (y_prompt, y_sample, new_state_hgrn_prompt, new_state_conv_prompt, new_state_hgrn_sample, new_state_conv_sample)
```

```python
import functools

import jax
import jax.numpy as jnp
from jax import lax
from jax.experimental import pallas as pl
from jax.experimental.pallas import tpu as pltpu

F32 = jnp.float32
BF16 = jnp.bfloat16

N_META = 16
HG_HEADS = 4
HG_DK = 128
HG_DV = 128
HG_WIDTH = HG_HEADS * HG_DK
CV_WIDTH = 512
CONV_K = 31
CHUNK = 64
EPS = 1e-6
C_Q, C_F, C_I, C_G, C_A, C_B, C_END = (0, HG_WIDTH, 2 * HG_WIDTH, 3 * HG_WIDTH, 4 * HG_WIDTH,
                                       4 * HG_WIDTH + CV_WIDTH, 4 * HG_WIDTH + 2 * CV_WIDTH)
HIST = 32
TILE_T = 256
TILE_M = 512
SAMPLE_BLOCK = 8
VMEM_LIMIT = 56 * 1024 * 1024


def _rms(x, g):
    return x * lax.rsqrt(jnp.mean(x * x, axis=-1, keepdims=True) + EPS) * g


def _layernorm(x, g, b):
    xc = x - jnp.mean(x, axis=-1, keepdims=True)
    return xc * lax.rsqrt(jnp.mean(xc * xc, axis=-1, keepdims=True) + EPS) * g + b


def _silu(x):
    return x * jax.nn.sigmoid(x)


def _lower_bound(lbp):
    e = jnp.exp(lbp - jnp.max(lbp, axis=0, keepdims=True))
    return e[0:1] / jnp.sum(e, axis=0, keepdims=True)


def _project(x, n1g, win_ref):
    hn = _rms(x, n1g).astype(BF16)
    return jnp.dot(hn, win_ref[...], preferred_element_type=F32)


def _tril(n):
    r = lax.broadcasted_iota(jnp.int32, (n, n), 0)
    c = lax.broadcasted_iota(jnp.int32, (n, n), 1)
    return r >= c


def _cumsum_rows(x):
    n = x.shape[0]
    return jnp.dot(_tril(n).astype(F32), x, precision=lax.Precision.HIGHEST, preferred_element_type=F32)


def _head(h):
    return slice(h * HG_DK, (h + 1) * HG_DK)


def _dot_nt(a, b):
    return lax.dot_general(a, b, (((1,), (1,)), ((), ())), preferred_element_type=F32)


def _dot_tn(a, b):
    return lax.dot_general(a, b, (((0,), (0,)), ((), ())), preferred_element_type=F32)


def _mlp(h, n2g, wup_ref, wdn_ref, fg):
    hn = _rms(h, n2g).astype(BF16)
    u = jnp.dot(hn, wup_ref[...], preferred_element_type=F32)
    a = jnp.square(jnp.maximum(u, 0.0)).astype(BF16)
    h2 = h + jnp.dot(a, wdn_ref[...], preferred_element_type=F32)
    return _rms(h2, fg)


def _prompt_mixer_kernel(x_ref, meta_ref, lbp_ref, n1g_ref, win_ref, og_ref, cw_ref, cb_ref, lng_ref, lnb_ref,
                         wout_ref, h_ref, sfin_ref, cfin_ref, z_ref, y_ref, st_ref, e_ref, *, tile_t):
    t = pl.program_id(1)
    lb = _lower_bound(lbp_ref[...])
    n1g = n1g_ref[...]

    @pl.when(t == 0)
    def _meta_tokens():
        zm = _project(meta_ref[...], n1g, win_ref)
        f = lb + (1.0 - lb) * jax.nn.sigmoid(zm[:, C_F:C_I])
        b = _cumsum_rows(jnp.log(f))
        k_end = ((1.0 - f) * jnp.exp(b[N_META - 1:N_META] - b)).astype(BF16)
        v = zm[:, C_I:C_G].astype(BF16)
        for h in range(HG_HEADS):
            st_ref[h] = _dot_tn(v[:, _head(h)], k_end[:, _head(h)])
        e_ref[0:HIST - N_META, :] = jnp.zeros((HIST - N_META, CV_WIDTH), F32)
        e_ref[HIST - N_META:HIST, :] = zm[:, C_A:C_B] * jax.nn.sigmoid(zm[:, C_B:C_END])

    x = x_ref[0]
    z_ref[...] = _project(x, n1g, win_ref)
    causal = _tril(CHUNK)
    og = og_ref[...]
    cb = cb_ref[...]

    for c in range(tile_t // CHUNK):
        rows = slice(c * CHUNK, (c + 1) * CHUNK)
        f = lb + (1.0 - lb) * jax.nn.sigmoid(z_ref[rows, C_F:C_I])
        k = 1.0 - f
        b = _cumsum_rows(jnp.log(f))
        b_end = b[CHUNK - 1:CHUNK]
        q_dec = (_silu(z_ref[rows, C_Q:C_F]) * jnp.exp(b)).astype(BF16)
        k_inv = (k * jnp.exp(-b)).astype(BF16)
        k_end = (k * jnp.exp(b_end - b)).astype(BF16)
        decay = jnp.exp(b_end)
        v = z_ref[rows, C_I:C_G].astype(BF16)
        gate = _silu(z_ref[rows, C_G:C_A])
        for h in range(HG_HEADS):
            hs = _head(h)
            scores = jnp.where(causal, _dot_nt(q_dec[:, hs], k_inv[:, hs]), 0.0)
            s_t = st_ref[h]
            o = (jnp.dot(scores.astype(BF16), v[:, hs], preferred_element_type=F32)
                 + _dot_nt(q_dec[:, hs], s_t.astype(BF16)))
            st_ref[h] = s_t * decay[:, hs] + _dot_tn(v[:, hs], k_end[:, hs])
            y_ref[rows, hs] = (_rms(o, og) * gate[:, hs]).astype(BF16)
        e_ref[HIST + c * CHUNK:HIST + (c + 1) * CHUNK, :] = z_ref[rows, C_A:C_B] * jax.nn.sigmoid(z_ref[rows, C_B:C_END])
        for r0 in range(c * CHUNK, (c + 1) * CHUNK, 32):
            acc = jnp.broadcast_to(cb, (32, CV_WIDTH))
            for d in range(CONV_K):
                acc = acc + cw_ref[CONV_K - 1 - d:CONV_K - d, :] * e_ref[HIST + r0 - d:HIST + r0 - d + 32, :]
            y_ref[r0:r0 + 32, HG_WIDTH:] = _silu(_layernorm(acc, lng_ref[...], lnb_ref[...])).astype(BF16)

    h_ref[0] = x + jnp.dot(y_ref[...], wout_ref[...], preferred_element_type=F32)
    e_ref[0:HIST, :] = e_ref[tile_t:tile_t + HIST, :]

    @pl.when(t == pl.num_programs(1) - 1)
    def _final_state():
        for h in range(HG_HEADS):
            sfin_ref[0, h] = st_ref[h].T
        cfin_ref[0] = e_ref[tile_t + HIST - (CONV_K - 1):tile_t + HIST, :]


def _const_spec(shape, single=False):
    zeros = (0,) * len(shape)
    if single:
        return pl.BlockSpec(shape, lambda *_: zeros, pipeline_mode=pl.Buffered(1))
    return pl.BlockSpec(shape, lambda *_: zeros)


def _prompt_mixer(x, meta, lbp, n1g, win, og, cw, cb, lng, lnb, wout):
    bsz, seq, d = x.shape
    nt = seq // TILE_T
    return pl.pallas_call(
        functools.partial(_prompt_mixer_kernel, tile_t=TILE_T),
        grid=(bsz, nt),
        in_specs=[
            pl.BlockSpec((1, TILE_T, d), lambda b, t: (b, t, 0)),
            _const_spec(meta.shape), _const_spec(lbp.shape), _const_spec(n1g.shape),
            _const_spec(win.shape, single=True), _const_spec(og.shape), _const_spec(cw.shape),
            _const_spec(cb.shape), _const_spec(lng.shape), _const_spec(lnb.shape),
            _const_spec(wout.shape, single=True),
        ],
        out_specs=[
            pl.BlockSpec((1, TILE_T, d), lambda b, t: (b, t, 0)),
            pl.BlockSpec((1, HG_HEADS, HG_DK, HG_DV), lambda b, t: (b, 0, 0, 0)),
            pl.BlockSpec((1, CONV_K - 1, CV_WIDTH), lambda b, t: (b, 0, 0)),
        ],
        out_shape=[
            jax.ShapeDtypeStruct((bsz, seq, d), F32),
            jax.ShapeDtypeStruct((bsz, HG_HEADS, HG_DK, HG_DV), F32),
            jax.ShapeDtypeStruct((bsz, CONV_K - 1, CV_WIDTH), F32),
        ],
        scratch_shapes=[
            pltpu.VMEM((TILE_T, C_END), F32),
            pltpu.VMEM((TILE_T, HG_WIDTH + CV_WIDTH), BF16),
            pltpu.VMEM((HG_HEADS, HG_DV, HG_DK), F32),
            pltpu.VMEM((HIST + TILE_T, CV_WIDTH), F32),
        ],
        compiler_params=pltpu.CompilerParams(
            dimension_semantics=("arbitrary", "arbitrary"), vmem_limit_bytes=VMEM_LIMIT),
        name="prompt_mixer",
    )(x, meta, lbp, n1g, win, og, cw, cb, lng, lnb, wout)


def _mlp_kernel(h_ref, n2g_ref, wup_ref, wdn_ref, fg_ref, y_ref):
    y_ref[...] = _mlp(h_ref[...], n2g_ref[...], wup_ref, wdn_ref, fg_ref[...])


def _prompt_mlp(h, n2g, wup, wdn, fg):
    rows, d = h.shape
    return pl.pallas_call(
        _mlp_kernel,
        grid=(rows // TILE_M,),
        in_specs=[
            pl.BlockSpec((TILE_M, d), lambda i: (i, 0)),
            _const_spec(n2g.shape), _const_spec(wup.shape, single=True), _const_spec(wdn.shape, single=True),
            _const_spec(fg.shape),
        ],
        out_specs=pl.BlockSpec((TILE_M, d), lambda i: (i, 0)),
        out_shape=jax.ShapeDtypeStruct((rows, d), F32),
        compiler_params=pltpu.CompilerParams(dimension_semantics=("arbitrary",), vmem_limit_bytes=VMEM_LIMIT),
        name="prompt_mlp",
    )(h, n2g, wup, wdn, fg)


def _sample_state_kernel(x_ref, lbp_ref, n1g_ref, win_ref, cw_ref, cb_ref, st_ref, sc_ref,
                         nst_ref, nsc_ref, o_ref, dw_ref, gate_ref, ft_ref, vg_ref, *, block):
    i = pl.program_id(0)

    @pl.when(i == 0)
    def _gates():
        lb = _lower_bound(lbp_ref[...])
        z = _project(x_ref[...], n1g_ref[...], win_ref)
        f = lb + (1.0 - lb) * jax.nn.sigmoid(z[:, C_F:C_I])
        k = 1.0 - f
        q = _silu(z[:, C_Q:C_F])
        for h in range(HG_HEADS):
            ft_ref[h] = f[:, _head(h)].T
            ft_ref[HG_HEADS + h] = k[:, _head(h)].T
            ft_ref[2 * HG_HEADS + h] = q[:, _head(h)].T
        vg_ref[:, 0:HG_WIDTH] = z[:, C_I:C_G]
        vg_ref[:, HG_WIDTH:] = z[:, C_A:C_B] * jax.nn.sigmoid(z[:, C_B:C_END])
        gate_ref[...] = _silu(z[:, C_G:C_A])

    nseq = ft_ref.shape[2]
    shift = lax.rem(nseq - i * block, nseq)
    vg = vg_ref[pl.ds(pl.multiple_of(i * block, block), block), :]
    for h in range(HG_HEADS):
        hs = _head(h)
        f_t = pltpu.roll(ft_ref[h], shift, 1)
        k_t = pltpu.roll(ft_ref[HG_HEADS + h], shift, 1)
        q_t = pltpu.roll(ft_ref[2 * HG_HEADS + h], shift, 1)
        for j in range(block):
            v_row = vg[j:j + 1, hs]
            s_new = f_t[:, j:j + 1] * st_ref[j, h] + k_t[:, j:j + 1] * v_row
            nst_ref[j, h] = s_new
            o_ref[j:j + 1, hs] = jnp.sum(q_t[:, j:j + 1] * s_new, axis=0, keepdims=True)
    for j in range(block):
        glu_row = vg[j:j + 1, HG_WIDTH:]
        dw_ref[j:j + 1, :] = (jnp.sum(sc_ref[j] * cw_ref[0:CONV_K - 1, :], axis=0, keepdims=True)
                              + glu_row * cw_ref[CONV_K - 1:CONV_K, :] + cb_ref[...])
        nsc_ref[j, 0:CONV_K - 2, :] = sc_ref[j, 1:CONV_K - 1, :]
        nsc_ref[j, CONV_K - 2:CONV_K - 1, :] = glu_row


def _sample_state(x, lbp, n1g, win, cw, cb, st, sc):
    nseq, d = x.shape
    blk = SAMPLE_BLOCK
    return pl.pallas_call(
        functools.partial(_sample_state_kernel, block=blk),
        grid=(nseq // blk,),
        in_specs=[
            _const_spec(x.shape), _const_spec(lbp.shape), _const_spec(n1g.shape),
            _const_spec(win.shape, single=True), _const_spec(cw.shape), _const_spec(cb.shape),
            pl.BlockSpec((blk, HG_HEADS, HG_DK, HG_DV), lambda i: (i, 0, 0, 0)),
            pl.BlockSpec((blk, CONV_K - 1, CV_WIDTH), lambda i: (i, 0, 0)),
        ],
        out_specs=[
            pl.BlockSpec((blk, HG_HEADS, HG_DK, HG_DV), lambda i: (i, 0, 0, 0)),
            pl.BlockSpec((blk, CONV_K - 1, CV_WIDTH), lambda i: (i, 0, 0)),
            pl.BlockSpec((blk, HG_WIDTH), lambda i: (i, 0)),
            pl.BlockSpec((blk, CV_WIDTH), lambda i: (i, 0)),
            _const_spec((nseq, HG_WIDTH)),
        ],
        out_shape=[
            jax.ShapeDtypeStruct(st.shape, F32),
            jax.ShapeDtypeStruct(sc.shape, F32),
            jax.ShapeDtypeStruct((nseq, HG_WIDTH), F32),
            jax.ShapeDtypeStruct((nseq, CV_WIDTH), F32),
            jax.ShapeDtypeStruct((nseq, HG_WIDTH), F32),
        ],
        scratch_shapes=[
            pltpu.VMEM((3 * HG_HEADS, HG_DK, nseq), F32),
            pltpu.VMEM((nseq, HG_WIDTH + CV_WIDTH), F32),
        ],
        compiler_params=pltpu.CompilerParams(dimension_semantics=("arbitrary",), vmem_limit_bytes=VMEM_LIMIT),
        name="sample_state",
    )(x, lbp, n1g, win, cw, cb, st, sc)


def _sample_tail_kernel(x_ref, o_ref, dw_ref, gate_ref, og_ref, lng_ref, lnb_ref, wout_ref, n2g_ref, wup_ref,
                        wdn_ref, fg_ref, y_ref):
    og = og_ref[...]
    y_hg = [(_rms(o_ref[:, _head(h)], og) * gate_ref[:, _head(h)]).astype(BF16) for h in range(HG_HEADS)]
    y_cv = _silu(_layernorm(dw_ref[...], lng_ref[...], lnb_ref[...])).astype(BF16)
    ycat = jnp.concatenate(y_hg + [y_cv], axis=-1)
    h = x_ref[...] + jnp.dot(ycat, wout_ref[...], preferred_element_type=F32)
    y_ref[...] = _mlp(h, n2g_ref[...], wup_ref, wdn_ref, fg_ref[...])


def _sample_tail(x, o, dw, gate, og, lng, lnb, wout, n2g, wup, wdn, fg):
    return pl.pallas_call(
        _sample_tail_kernel,
        out_shape=jax.ShapeDtypeStruct(x.shape, F32),
        compiler_params=pltpu.CompilerParams(vmem_limit_bytes=VMEM_LIMIT),
        name="sample_tail",
    )(x, o, dw, gate, og, lng, lnb, wout, n2g, wup, wdn, fg)


def kernel(x_prompt, x_sample, state_hgrn, state_conv, meta_tokens, hg_lb, norm1_g, w_in, hg_onorm_g, conv_w, conv_b,
           conv_ln_g, conv_ln_b, w_out, norm2_g, w_up, w_down, final_g):
    assert state_hgrn.shape[0] == 1, "single-layer stack"
    bsz, seq, d = x_prompt.shape
    row = lambda a: a.reshape(1, -1)
    n1g, og, cb, lng, lnb, n2g, fg = (row(norm1_g[0]), row(hg_onorm_g[0]), row(conv_b[0]), row(conv_ln_g[0]),
                                      row(conv_ln_b[0]), row(norm2_g[0]), row(final_g))
    win, wout, wup, wdn = (w_in[0].astype(BF16), w_out[0].astype(BF16), w_up[0].astype(BF16),
                           w_down[0].astype(BF16))
    cw = conv_w[0]

    h_p, s_p, c_p = _prompt_mixer(x_prompt, meta_tokens, hg_lb, n1g, win, og, cw, cb, lng, lnb, wout)
    y_p = _prompt_mlp(h_p.reshape(bsz * seq, d), n2g, wup, wdn, fg).reshape(bsz, seq, d)

    x_s = x_sample.reshape(x_sample.shape[0], d)
    s_s, c_s, o_s, dw_s, gate_s = _sample_state(x_s, hg_lb, n1g, win, cw, cb, state_hgrn[0], state_conv[0])
    y_s = _sample_tail(x_s, o_s, dw_s, gate_s, og, lng, lnb, wout, n2g, wup, wdn, fg).reshape(x_sample.shape)

    return (y_p, y_s, s_p[None], c_p[None], s_s[None], c_s[None])
```

```python
import functools

import jax
import jax.numpy as jnp
from jax import lax
from jax.experimental import pallas as pl
from jax.experimental.pallas import tpu as pltpu

F32 = jnp.float32
BF16 = jnp.bfloat16

N_META = 16
HG_HEADS = 4
HG_DK = 128
HG_DV = 128
HG_WIDTH = HG_HEADS * HG_DK
CV_WIDTH = 512
CONV_K = 31
CHUNK = 64
EPS = 1e-6
C_Q, C_F, C_I, C_G, C_A, C_B, C_END = (0, HG_WIDTH, 2 * HG_WIDTH, 3 * HG_WIDTH, 4 * HG_WIDTH,
                                       4 * HG_WIDTH + CV_WIDTH, 4 * HG_WIDTH + 2 * CV_WIDTH)
HIST = 32
TILE_T = 256
TILE_M = 512
SAMPLE_BLOCK = 8
VMEM_LIMIT = 56 * 1024 * 1024


def _rms(x, g):
    return x * lax.rsqrt(jnp.mean(x * x, axis=-1, keepdims=True) + EPS) * g


def _layernorm(x, g, b):
    xc = x - jnp.mean(x, axis=-1, keepdims=True)
    return xc * lax.rsqrt(jnp.mean(xc * xc, axis=-1, keepdims=True) + EPS) * g + b


def _silu(x):
    return x * jax.nn.sigmoid(x)


def _lower_bound(lbp):
    e = jnp.exp(lbp - jnp.max(lbp, axis=0, keepdims=True))
    return e[0:1] / jnp.sum(e, axis=0, keepdims=True)


def _project(x, n1g, win_ref):
    hn = _rms(x, n1g).astype(BF16)
    return jnp.dot(hn, win_ref[...], preferred_element_type=F32)


def _tril(n):
    r = lax.broadcasted_iota(jnp.int32, (n, n), 0)
    c = lax.broadcasted_iota(jnp.int32, (n, n), 1)
    return r >= c


def _cumsum_rows(x):
    n = x.shape[0]
    return jnp.dot(_tril(n).astype(F32), x, precision=lax.Precision.HIGHEST, preferred_element_type=F32)


def _head(h):
    return slice(h * HG_DK, (h + 1) * HG_DK)


def _dot_nt(a, b):
    return lax.dot_general(a, b, (((1,), (1,)), ((), ())), preferred_element_type=F32)


def _dot_tn(a, b):
    return lax.dot_general(a, b, (((0,), (0,)), ((), ())), preferred_element_type=F32)


def _mlp(h, n2g, wup_ref, wdn_ref, fg):
    hn = _rms(h, n2g).astype(BF16)
    u = jnp.dot(hn, wup_ref[...], preferred_element_type=F32)
    a = jnp.square(jnp.maximum(u, 0.0)).astype(BF16)
    h2 = h + jnp.dot(a, wdn_ref[...], preferred_element_type=F32)
    return _rms(h2, fg)


def _dwconv_rows(e_ref, cw_ref, bias, base, rows):
    sub = 8
    outs = []
    for g in range(CV_WIDTH // 128):
        lanes = slice(g * 128, (g + 1) * 128)
        ext = e_ref[base - HIST:base + rows, lanes]
        acc = None
        for r in range(sub):
            u = None
            for a in range(-(-CONV_K // sub)):
                d = sub * a + r
                if d < CONV_K:
                    lo = HIST - sub - sub * a
                    term = cw_ref[CONV_K - 1 - d:CONV_K - d, lanes] * ext[lo:lo + rows + sub]
                    u = term if u is None else u + term
            shifted = u[sub:] if r == 0 else pltpu.roll(u, r, 0)[sub:]
            acc = shifted if acc is None else acc + shifted
        outs.append(acc + bias[:, lanes])
    return jnp.concatenate(outs, axis=-1)


def _prompt_mixer_kernel(x_ref, meta_ref, lbp_ref, n1g_ref, win_ref, og_ref, cw_ref, cb_ref, lng_ref, lnb_ref,
                         wout_ref, h_ref, sfin_ref, cfin_ref, z_ref, y_ref, st_ref, e_ref, *, tile_t):
    t = pl.program_id(1)
    lb = _lower_bound(lbp_ref[...])
    n1g = n1g_ref[...]

    @pl.when(t == 0)
    def _meta_tokens():
        zm = _project(meta_ref[...], n1g, win_ref)
        f = lb + (1.0 - lb) * jax.nn.sigmoid(zm[:, C_F:C_I])
        b = _cumsum_rows(jnp.log(f))
        k_end = ((1.0 - f) * jnp.exp(b[N_META - 1:N_META] - b)).astype(BF16)
        v = zm[:, C_I:C_G].astype(BF16)
        for h in range(HG_HEADS):
            st_ref[h] = _dot_tn(v[:, _head(h)], k_end[:, _head(h)])
        e_ref[0:HIST - N_META, :] = jnp.zeros((HIST - N_META, CV_WIDTH), F32)
        e_ref[HIST - N_META:HIST, :] = zm[:, C_A:C_B] * jax.nn.sigmoid(zm[:, C_B:C_END])

    x = x_ref[0]
    z_ref[...] = _project(x, n1g, win_ref)
    causal = _tril(CHUNK)
    og = og_ref[...]
    cb = cb_ref[...]

    for c in range(tile_t // CHUNK):
        rows = slice(c * CHUNK, (c + 1) * CHUNK)
        f = lb + (1.0 - lb) * jax.nn.sigmoid(z_ref[rows, C_F:C_I])
        k = 1.0 - f
        b = _cumsum_rows(jnp.log(f))
        b_end = b[CHUNK - 1:CHUNK]
        q_dec = (_silu(z_ref[rows, C_Q:C_F]) * jnp.exp(b)).astype(BF16)
        k_inv = (k * jnp.exp(-b)).astype(BF16)
        k_end = (k * jnp.exp(b_end - b)).astype(BF16)
        decay = jnp.exp(b_end)
        v = z_ref[rows, C_I:C_G].astype(BF16)
        gate = _silu(z_ref[rows, C_G:C_A])
        for h in range(HG_HEADS):
            hs = _head(h)
            scores = jnp.where(causal, _dot_nt(q_dec[:, hs], k_inv[:, hs]), 0.0)
            s_t = st_ref[h]
            o = (jnp.dot(scores.astype(BF16), v[:, hs], preferred_element_type=F32)
                 + _dot_nt(q_dec[:, hs], s_t.astype(BF16)))
            st_ref[h] = s_t * decay[:, hs] + _dot_tn(v[:, hs], k_end[:, hs])
            y_ref[rows, hs] = (_rms(o, og) * gate[:, hs]).astype(BF16)
        e_ref[HIST + c * CHUNK:HIST + (c + 1) * CHUNK, :] = z_ref[rows, C_A:C_B] * jax.nn.sigmoid(z_ref[rows, C_B:C_END])
        dw = _dwconv_rows(e_ref, cw_ref, cb, HIST + c * CHUNK, CHUNK)
        y_ref[rows, HG_WIDTH:] = _silu(_layernorm(dw, lng_ref[...], lnb_ref[...])).astype(BF16)

    h_ref[0] = x + jnp.dot(y_ref[...], wout_ref[...], preferred_element_type=F32)
    e_ref[0:HIST, :] = e_ref[tile_t:tile_t + HIST, :]

    @pl.when(t == pl.num_programs(1) - 1)
    def _final_state():
        for h in range(HG_HEADS):
            sfin_ref[0, h] = st_ref[h].T
        cfin_ref[0] = e_ref[tile_t + HIST - (CONV_K - 1):tile_t + HIST, :]


def _const_spec(shape, single=False):
    zeros = (0,) * len(shape)
    if single:
        return pl.BlockSpec(shape, lambda *_: zeros, pipeline_mode=pl.Buffered(1))
    return pl.BlockSpec(shape, lambda *_: zeros)


def _prompt_mixer(x, meta, lbp, n1g, win, og, cw, cb, lng, lnb, wout):
    bsz, seq, d = x.shape
    nt = seq // TILE_T
    return pl.pallas_call(
        functools.partial(_prompt_mixer_kernel, tile_t=TILE_T),
        grid=(bsz, nt),
        in_specs=[
            pl.BlockSpec((1, TILE_T, d), lambda b, t: (b, t, 0)),
            _const_spec(meta.shape), _const_spec(lbp.shape), _const_spec(n1g.shape),
            _const_spec(win.shape, single=True), _const_spec(og.shape), _const_spec(cw.shape),
            _const_spec(cb.shape), _const_spec(lng.shape), _const_spec(lnb.shape),
            _const_spec(wout.shape, single=True),
        ],
        out_specs=[
            pl.BlockSpec((1, TILE_T, d), lambda b, t: (b, t, 0)),
            pl.BlockSpec((1, HG_HEADS, HG_DK, HG_DV), lambda b, t: (b, 0, 0, 0)),
            pl.BlockSpec((1, CONV_K - 1, CV_WIDTH), lambda b, t: (b, 0, 0)),
        ],
        out_shape=[
            jax.ShapeDtypeStruct((bsz, seq, d), F32),
            jax.ShapeDtypeStruct((bsz, HG_HEADS, HG_DK, HG_DV), F32),
            jax.ShapeDtypeStruct((bsz, CONV_K - 1, CV_WIDTH), F32),
        ],
        scratch_shapes=[
            pltpu.VMEM((TILE_T, C_END), F32),
            pltpu.VMEM((TILE_T, HG_WIDTH + CV_WIDTH), BF16),
            pltpu.VMEM((HG_HEADS, HG_DV, HG_DK), F32),
            pltpu.VMEM((HIST + TILE_T, CV_WIDTH), F32),
        ],
        compiler_params=pltpu.CompilerParams(
            dimension_semantics=("arbitrary", "arbitrary"), vmem_limit_bytes=VMEM_LIMIT),
        name="prompt_mixer",
    )(x, meta, lbp, n1g, win, og, cw, cb, lng, lnb, wout)


def _mlp_kernel(h_ref, n2g_ref, wup_ref, wdn_ref, fg_ref, y_ref):
    y_ref[...] = _mlp(h_ref[...], n2g_ref[...], wup_ref, wdn_ref, fg_ref[...])


def _prompt_mlp(h, n2g, wup, wdn, fg):
    rows, d = h.shape
    return pl.pallas_call(
        _mlp_kernel,
        grid=(rows // TILE_M,),
        in_specs=[
            pl.BlockSpec((TILE_M, d), lambda i: (i, 0)),
            _const_spec(n2g.shape), _const_spec(wup.shape, single=True), _const_spec(wdn.shape, single=True),
            _const_spec(fg.shape),
        ],
        out_specs=pl.BlockSpec((TILE_M, d), lambda i: (i, 0)),
        out_shape=jax.ShapeDtypeStruct((rows, d), F32),
        compiler_params=pltpu.CompilerParams(dimension_semantics=("arbitrary",), vmem_limit_bytes=VMEM_LIMIT),
        name="prompt_mlp",
    )(h, n2g, wup, wdn, fg)


def _sample_state_kernel(x_ref, lbp_ref, n1g_ref, win_ref, cw_ref, cb_ref, st_ref, sc_ref,
                         nst_ref, nsc_ref, o_ref, dw_ref, gate_ref, ft_ref, vg_ref, *, block):
    i = pl.program_id(0)

    @pl.when(i == 0)
    def _gates():
        lb = _lower_bound(lbp_ref[...])
        z = _project(x_ref[...], n1g_ref[...], win_ref)
        f = lb + (1.0 - lb) * jax.nn.sigmoid(z[:, C_F:C_I])
        k = 1.0 - f
        q = _silu(z[:, C_Q:C_F])
        for h in range(HG_HEADS):
            ft_ref[h] = f[:, _head(h)].T
            ft_ref[HG_HEADS + h] = k[:, _head(h)].T
            ft_ref[2 * HG_HEADS + h] = q[:, _head(h)].T
        vg_ref[:, 0:HG_WIDTH] = z[:, C_I:C_G]
        vg_ref[:, HG_WIDTH:] = z[:, C_A:C_B] * jax.nn.sigmoid(z[:, C_B:C_END])
        gate_ref[...] = _silu(z[:, C_G:C_A])

    nseq = ft_ref.shape[2]
    shift = lax.rem(nseq - i * block, nseq)
    vg = vg_ref[pl.ds(pl.multiple_of(i * block, block), block), :]
    for h in range(HG_HEADS):
        hs = _head(h)
        f_t = pltpu.roll(ft_ref[h], shift, 1)
        k_t = pltpu.roll(ft_ref[HG_HEADS + h], shift, 1)
        q_t = pltpu.roll(ft_ref[2 * HG_HEADS + h], shift, 1)
        for j in range(block):
            v_row = vg[j:j + 1, hs]
            s_new = f_t[:, j:j + 1] * st_ref[j, h] + k_t[:, j:j + 1] * v_row
            nst_ref[j, h] = s_new
            o_ref[j:j + 1, hs] = jnp.sum(q_t[:, j:j + 1] * s_new, axis=0, keepdims=True)
    for j in range(block):
        glu_row = vg[j:j + 1, HG_WIDTH:]
        dw_ref[j:j + 1, :] = (jnp.sum(sc_ref[j] * cw_ref[0:CONV_K - 1, :], axis=0, keepdims=True)
                              + glu_row * cw_ref[CONV_K - 1:CONV_K, :] + cb_ref[...])
        nsc_ref[j, 0:CONV_K - 2, :] = sc_ref[j, 1:CONV_K - 1, :]
        nsc_ref[j, CONV_K - 2:CONV_K - 1, :] = glu_row


def _sample_state(x, lbp, n1g, win, cw, cb, st, sc):
    nseq, d = x.shape
    blk = SAMPLE_BLOCK
    return pl.pallas_call(
        functools.partial(_sample_state_kernel, block=blk),
        grid=(nseq // blk,),
        in_specs=[
            _const_spec(x.shape), _const_spec(lbp.shape), _const_spec(n1g.shape),
            _const_spec(win.shape, single=True), _const_spec(cw.shape), _const_spec(cb.shape),
            pl.BlockSpec((blk, HG_HEADS, HG_DK, HG_DV), lambda i: (i, 0, 0, 0)),
            pl.BlockSpec((blk, CONV_K - 1, CV_WIDTH), lambda i: (i, 0, 0)),
        ],
        out_specs=[
            pl.BlockSpec((blk, HG_HEADS, HG_DK, HG_DV), lambda i: (i, 0, 0, 0)),
            pl.BlockSpec((blk, CONV_K - 1, CV_WIDTH), lambda i: (i, 0, 0)),
            pl.BlockSpec((blk, HG_WIDTH), lambda i: (i, 0)),
            pl.BlockSpec((blk, CV_WIDTH), lambda i: (i, 0)),
            _const_spec((nseq, HG_WIDTH)),
        ],
        out_shape=[
            jax.ShapeDtypeStruct(st.shape, F32),
            jax.ShapeDtypeStruct(sc.shape, F32),
            jax.ShapeDtypeStruct((nseq, HG_WIDTH), F32),
            jax.ShapeDtypeStruct((nseq, CV_WIDTH), F32),
            jax.ShapeDtypeStruct((nseq, HG_WIDTH), F32),
        ],
        scratch_shapes=[
            pltpu.VMEM((3 * HG_HEADS, HG_DK, nseq), F32),
            pltpu.VMEM((nseq, HG_WIDTH + CV_WIDTH), F32),
        ],
        compiler_params=pltpu.CompilerParams(dimension_semantics=("arbitrary",), vmem_limit_bytes=VMEM_LIMIT),
        name="sample_state",
    )(x, lbp, n1g, win, cw, cb, st, sc)


def _sample_tail_kernel(x_ref, o_ref, dw_ref, gate_ref, og_ref, lng_ref, lnb_ref, wout_ref, n2g_ref, wup_ref,
                        wdn_ref, fg_ref, y_ref):
    og = og_ref[...]
    y_hg = [(_rms(o_ref[:, _head(h)], og) * gate_ref[:, _head(h)]).astype(BF16) for h in range(HG_HEADS)]
    y_cv = _silu(_layernorm(dw_ref[...], lng_ref[...], lnb_ref[...])).astype(BF16)
    ycat = jnp.concatenate(y_hg + [y_cv], axis=-1)
    h = x_ref[...] + jnp.dot(ycat, wout_ref[...], preferred_element_type=F32)
    y_ref[...] = _mlp(h, n2g_ref[...], wup_ref, wdn_ref, fg_ref[...])


def _sample_tail(x, o, dw, gate, og, lng, lnb, wout, n2g, wup, wdn, fg):
    return pl.pallas_call(
        _sample_tail_kernel,
        out_shape=jax.ShapeDtypeStruct(x.shape, F32),
        compiler_params=pltpu.CompilerParams(vmem_limit_bytes=VMEM_LIMIT),
        name="sample_tail",
    )(x, o, dw, gate, og, lng, lnb, wout, n2g, wup, wdn, fg)


def kernel(x_prompt, x_sample, state_hgrn, state_conv, meta_tokens, hg_lb, norm1_g, w_in, hg_onorm_g, conv_w, conv_b,
           conv_ln_g, conv_ln_b, w_out, norm2_g, w_up, w_down, final_g):
    assert state_hgrn.shape[0] == 1, "single-layer stack"
    bsz, seq, d = x_prompt.shape
    row = lambda a: a.reshape(1, -1)
    n1g, og, cb, lng, lnb, n2g, fg = (row(norm1_g[0]), row(hg_onorm_g[0]), row(conv_b[0]), row(conv_ln_g[0]),
                                      row(conv_ln_b[0]), row(norm2_g[0]), row(final_g))
    win, wout, wup, wdn = (w_in[0].astype(BF16), w_out[0].astype(BF16), w_up[0].astype(BF16),
                           w_down[0].astype(BF16))
    cw = conv_w[0]

    h_p, s_p, c_p = _prompt_mixer(x_prompt, meta_tokens, hg_lb, n1g, win, og, cw, cb, lng, lnb, wout)
    y_p = _prompt_mlp(h_p.reshape(bsz * seq, d), n2g, wup, wdn, fg).reshape(bsz, seq, d)

    x_s = x_sample.reshape(x_sample.shape[0], d)
    s_s, c_s, o_s, dw_s, gate_s = _sample_state(x_s, hg_lb, n1g, win, cw, cb, state_hgrn[0], state_conv[0])
    y_s = _sample_tail(x_s, o_s, dw_s, gate_s, og, lng, lnb, wout, n2g, wup, wdn, fg).reshape(x_sample.shape)

    return (y_p, y_s, s_p[None], c_p[None], s_s[None], c_s[None])
```

```python
import functools

import jax
import jax.numpy as jnp
from jax import lax
from jax.experimental import pallas as pl
from jax.experimental.pallas import tpu as pltpu

F32 = jnp.float32
BF16 = jnp.bfloat16

N_META = 16
HG_HEADS = 4
HG_DK = 128
HG_DV = 128
HG_WIDTH = HG_HEADS * HG_DK
CV_WIDTH = 512
CONV_K = 31
CHUNK = 64
EPS = 1e-6
C_Q, C_F, C_I, C_G, C_A, C_B, C_END = (0, HG_WIDTH, 2 * HG_WIDTH, 3 * HG_WIDTH, 4 * HG_WIDTH,
                                       4 * HG_WIDTH + CV_WIDTH, 4 * HG_WIDTH + 2 * CV_WIDTH)
HIST = 32
TILE_T = 256
SAMPLE_BLOCK = 8
VMEM_LIMIT = 56 * 1024 * 1024


def _rms(x, g):
    return x * lax.rsqrt(jnp.mean(x * x, axis=-1, keepdims=True) + EPS) * g


def _layernorm(x, g, b):
    xc = x - jnp.mean(x, axis=-1, keepdims=True)
    return xc * lax.rsqrt(jnp.mean(xc * xc, axis=-1, keepdims=True) + EPS) * g + b


def _silu(x):
    return x * jax.nn.sigmoid(x)


def _lower_bound(lbp):
    e = jnp.exp(lbp - jnp.max(lbp, axis=0, keepdims=True))
    return e[0:1] / jnp.sum(e, axis=0, keepdims=True)


def _project(x, n1g, win_ref):
    hn = _rms(x, n1g).astype(BF16)
    return jnp.dot(hn, win_ref[...], preferred_element_type=F32)


def _tril(n):
    r = lax.broadcasted_iota(jnp.int32, (n, n), 0)
    c = lax.broadcasted_iota(jnp.int32, (n, n), 1)
    return r >= c


def _cumsum_rows(x):
    n = x.shape[0]
    return jnp.dot(_tril(n).astype(F32), x, precision=lax.Precision.HIGHEST, preferred_element_type=F32)


def _head(h):
    return slice(h * HG_DK, (h + 1) * HG_DK)


def _dot_nt(a, b):
    return lax.dot_general(a, b, (((1,), (1,)), ((), ())), preferred_element_type=F32)


def _dot_tn(a, b):
    return lax.dot_general(a, b, (((0,), (0,)), ((), ())), preferred_element_type=F32)


def _mlp(h, n2g, wup_ref, wdn_ref, fg):
    hn = _rms(h, n2g).astype(BF16)
    u = jnp.dot(hn, wup_ref[...], preferred_element_type=F32)
    a = jnp.square(jnp.maximum(u, 0.0)).astype(BF16)
    h2 = h + jnp.dot(a, wdn_ref[...], preferred_element_type=F32)
    return _rms(h2, fg)


def _dwconv_rows(e_ref, cw_ref, bias, base, rows):
    return jnp.concatenate([_dwconv_lane_group(e_ref, cw_ref, bias, base, rows, g) for g in range(CV_WIDTH // 128)],
                           axis=-1)


def _dwconv_lane_group(e_ref, cw_ref, bias, base, rows, g):
    sub = 8
    lanes = slice(g * 128, (g + 1) * 128)
    ext = e_ref[base - HIST:base + rows, lanes]
    acc = None
    for r in range(sub):
        u = None
        for a in range(-(-CONV_K // sub)):
            d = sub * a + r
            if d < CONV_K:
                lo = HIST - sub - sub * a
                term = cw_ref[CONV_K - 1 - d:CONV_K - d, lanes] * ext[lo:lo + rows + sub]
                u = term if u is None else u + term
        shifted = u[sub:] if r == 0 else pltpu.roll(u, r, 0)[sub:]
        acc = shifted if acc is None else acc + shifted
    return acc + bias[:, lanes]


def _prompt_layer_kernel(x_ref, meta_ref, lbp_ref, n1g_ref, win_ref, og_ref, cw_ref, cb_ref, lng_ref, lnb_ref,
                         wout_ref, n2g_ref, wup_ref, wdn_ref, fg_ref, y_out_ref, sfin_ref, cfin_ref,
                         z_ref, y_ref, st_ref, e_ref, h_ref, hn_ref, a_ref, acc_ref, *, tile_t, nt):
    s = pl.program_id(0)
    ns = pl.num_programs(0) - 1
    mixing = s < ns
    t = lax.rem(jnp.minimum(s, ns - 1), nt)
    cur = lax.rem(s, 2)
    prev = 1 - cur
    lb = _lower_bound(lbp_ref[...])
    n1g = n1g_ref[...]

    @pl.when(s == 0)
    def _no_previous_tile():
        h_ref[1] = jnp.zeros(h_ref.shape[1:], F32)

    @pl.when(mixing & (t == 0))
    def _meta_tokens():
        zm = _project(meta_ref[...], n1g, win_ref)
        f = lb + (1.0 - lb) * jax.nn.sigmoid(zm[:, C_F:C_I])
        b = _cumsum_rows(jnp.log(f))
        k_end = ((1.0 - f) * jnp.exp(b[N_META - 1:N_META] - b)).astype(BF16)
        v = zm[:, C_I:C_G].astype(BF16)
        for h in range(HG_HEADS):
            st_ref[h] = _dot_tn(v[:, _head(h)], k_end[:, _head(h)])
        e_ref[0:HIST - N_META, :] = jnp.zeros((HIST - N_META, CV_WIDTH), F32)
        e_ref[HIST - N_META:HIST, :] = zm[:, C_A:C_B] * jax.nn.sigmoid(zm[:, C_B:C_END])

    x = x_ref[0]
    z_ref[...] = _project(x, n1g, win_ref)
    h_prev = h_ref[prev]
    hn_ref[...] = _rms(h_prev, n2g_ref[...]).astype(BF16)
    acc_ref[...] = h_prev
    causal = _tril(CHUNK)
    og = og_ref[...]
    cb = cb_ref[...]
    n_chunks = tile_t // CHUNK
    ff = wup_ref.shape[1] // n_chunks
    fft = ff // HG_HEADS
    dt = acc_ref.shape[1] // (CV_WIDTH // 128)

    for c in range(n_chunks):
        rows = slice(c * CHUNK, (c + 1) * CHUNK)
        f = lb + (1.0 - lb) * jax.nn.sigmoid(z_ref[rows, C_F:C_I])
        k = 1.0 - f
        b = _cumsum_rows(jnp.log(f))
        b_end = b[CHUNK - 1:CHUNK]
        q_dec = (_silu(z_ref[rows, C_Q:C_F]) * jnp.exp(b)).astype(BF16)
        k_inv = (k * jnp.exp(-b)).astype(BF16)
        k_end = (k * jnp.exp(b_end - b)).astype(BF16)
        decay = jnp.exp(b_end)
        v = z_ref[rows, C_I:C_G].astype(BF16)
        gate = _silu(z_ref[rows, C_G:C_A])
        e_ref[HIST + c * CHUNK:HIST + (c + 1) * CHUNK, :] = z_ref[rows, C_A:C_B] * jax.nn.sigmoid(z_ref[rows, C_B:C_END])
        for h in range(HG_HEADS):
            ucols = slice(c * ff + h * fft, c * ff + (h + 1) * fft)
            u = jnp.dot(hn_ref[...], wup_ref[:, ucols], preferred_element_type=F32)
            a_ref[:, h * fft:(h + 1) * fft] = jnp.square(jnp.maximum(u, 0.0)).astype(BF16)
            hs = _head(h)
            scores = jnp.where(causal, _dot_nt(q_dec[:, hs], k_inv[:, hs]), 0.0)
            s_t = st_ref[h]
            o = (jnp.dot(scores.astype(BF16), v[:, hs], preferred_element_type=F32)
                 + _dot_nt(q_dec[:, hs], s_t.astype(BF16)))
            st_ref[h] = s_t * decay[:, hs] + _dot_tn(v[:, hs], k_end[:, hs])
            y_ref[rows, hs] = (_rms(o, og) * gate[:, hs]).astype(BF16)
        dw = []
        for g in range(CV_WIDTH // 128):
            dw.append(_dwconv_lane_group(e_ref, cw_ref, cb, HIST + c * CHUNK, CHUNK, g))
            dcols = slice(g * dt, (g + 1) * dt)
            acc_ref[:, dcols] += jnp.dot(a_ref[...], wdn_ref[c * ff:(c + 1) * ff, dcols], preferred_element_type=F32)
        dw = jnp.concatenate(dw, axis=-1)
        y_ref[rows, HG_WIDTH:] = _silu(_layernorm(dw, lng_ref[...], lnb_ref[...])).astype(BF16)

    y_out_ref[0] = _rms(acc_ref[...], fg_ref[...])
    h_ref[cur] = x + jnp.dot(y_ref[...], wout_ref[...], preferred_element_type=F32)
    e_ref[0:HIST, :] = e_ref[tile_t:tile_t + HIST, :]

    @pl.when(mixing & (t == nt - 1))
    def _final_state():
        for h in range(HG_HEADS):
            sfin_ref[0, h] = st_ref[h].T
        cfin_ref[0] = e_ref[tile_t + HIST - (CONV_K - 1):tile_t + HIST, :]


def _const_spec(shape, single=False):
    zeros = (0,) * len(shape)
    if single:
        return pl.BlockSpec(shape, lambda *_: zeros, pipeline_mode=pl.Buffered(1))
    return pl.BlockSpec(shape, lambda *_: zeros)


def _prompt_layer(x, meta, lbp, n1g, win, og, cw, cb, lng, lnb, wout, n2g, wup, wdn, fg):
    bsz, seq, d = x.shape
    nt = seq // TILE_T
    ns = bsz * nt

    def mix_tile(s):
        s1 = jnp.minimum(s, ns - 1)
        return s1 // nt, s1 % nt

    def mlp_tile(s):
        s2 = jnp.maximum(s - 1, 0)
        return s2 // nt, s2 % nt

    return pl.pallas_call(
        functools.partial(_prompt_layer_kernel, tile_t=TILE_T, nt=nt),
        grid=(ns + 1,),
        in_specs=[
            pl.BlockSpec((1, TILE_T, d), lambda s: (*mix_tile(s), 0)),
            _const_spec(meta.shape), _const_spec(lbp.shape), _const_spec(n1g.shape),
            _const_spec(win.shape, single=True), _const_spec(og.shape), _const_spec(cw.shape),
            _const_spec(cb.shape), _const_spec(lng.shape), _const_spec(lnb.shape),
            _const_spec(wout.shape, single=True), _const_spec(n2g.shape),
            _const_spec(wup.shape, single=True), _const_spec(wdn.shape, single=True), _const_spec(fg.shape),
        ],
        out_specs=[
            pl.BlockSpec((1, TILE_T, d), lambda s: (*mlp_tile(s), 0)),
            pl.BlockSpec((1, HG_HEADS, HG_DK, HG_DV), lambda s: (mix_tile(s)[0], 0, 0, 0)),
            pl.BlockSpec((1, CONV_K - 1, CV_WIDTH), lambda s: (mix_tile(s)[0], 0, 0)),
        ],
        out_shape=[
            jax.ShapeDtypeStruct((bsz, seq, d), F32),
            jax.ShapeDtypeStruct((bsz, HG_HEADS, HG_DK, HG_DV), F32),
            jax.ShapeDtypeStruct((bsz, CONV_K - 1, CV_WIDTH), F32),
        ],
        scratch_shapes=[
            pltpu.VMEM((TILE_T, C_END), F32),
            pltpu.VMEM((TILE_T, HG_WIDTH + CV_WIDTH), BF16),
            pltpu.VMEM((HG_HEADS, HG_DV, HG_DK), F32),
            pltpu.VMEM((HIST + TILE_T, CV_WIDTH), F32),
            pltpu.VMEM((2, TILE_T, d), F32),
            pltpu.VMEM((TILE_T, d), BF16),
            pltpu.VMEM((TILE_T, wup.shape[1] // (TILE_T // CHUNK)), BF16),
            pltpu.VMEM((TILE_T, d), F32),
        ],
        compiler_params=pltpu.CompilerParams(dimension_semantics=("arbitrary",), vmem_limit_bytes=VMEM_LIMIT),
        name="prompt_layer",
    )(x, meta, lbp, n1g, win, og, cw, cb, lng, lnb, wout, n2g, wup, wdn, fg)


def _sample_state_kernel(x_ref, lbp_ref, n1g_ref, win_ref, cw_ref, cb_ref, st_ref, sc_ref,
                         nst_ref, nsc_ref, o_ref, dw_ref, gate_ref, ft_ref, vg_ref, *, block):
    i = pl.program_id(0)

    @pl.when(i == 0)
    def _gates():
        lb = _lower_bound(lbp_ref[...])
        z = _project(x_ref[...], n1g_ref[...], win_ref)
        f = lb + (1.0 - lb) * jax.nn.sigmoid(z[:, C_F:C_I])
        k = 1.0 - f
        q = _silu(z[:, C_Q:C_F])
        for h in range(HG_HEADS):
            ft_ref[h] = f[:, _head(h)].T
            ft_ref[HG_HEADS + h] = k[:, _head(h)].T
            ft_ref[2 * HG_HEADS + h] = q[:, _head(h)].T
        vg_ref[:, 0:HG_WIDTH] = z[:, C_I:C_G]
        vg_ref[:, HG_WIDTH:] = z[:, C_A:C_B] * jax.nn.sigmoid(z[:, C_B:C_END])
        gate_ref[...] = _silu(z[:, C_G:C_A])

    nseq = ft_ref.shape[2]
    shift = lax.rem(nseq - i * block, nseq)
    vg = vg_ref[pl.ds(pl.multiple_of(i * block, block), block), :]
    for h in range(HG_HEADS):
        hs = _head(h)
        f_t = pltpu.roll(ft_ref[h], shift, 1)
        k_t = pltpu.roll(ft_ref[HG_HEADS + h], shift, 1)
        q_t = pltpu.roll(ft_ref[2 * HG_HEADS + h], shift, 1)
        for j in range(block):
            v_row = vg[j:j + 1, hs]
            s_new = f_t[:, j:j + 1] * st_ref[j, h] + k_t[:, j:j + 1] * v_row
            nst_ref[j, h] = s_new
            o_ref[j:j + 1, hs] = jnp.sum(q_t[:, j:j + 1] * s_new, axis=0, keepdims=True)
    for j in range(block):
        glu_row = vg[j:j + 1, HG_WIDTH:]
        dw_ref[j:j + 1, :] = (jnp.sum(sc_ref[j] * cw_ref[0:CONV_K - 1, :], axis=0, keepdims=True)
                              + glu_row * cw_ref[CONV_K - 1:CONV_K, :] + cb_ref[...])
        nsc_ref[j, 0:CONV_K - 2, :] = sc_ref[j, 1:CONV_K - 1, :]
        nsc_ref[j, CONV_K - 2:CONV_K - 1, :] = glu_row


def _sample_state(x, lbp, n1g, win, cw, cb, st, sc):
    nseq, d = x.shape
    blk = SAMPLE_BLOCK
    return pl.pallas_call(
        functools.partial(_sample_state_kernel, block=blk),
        grid=(nseq // blk,),
        in_specs=[
            _const_spec(x.shape), _const_spec(lbp.shape), _const_spec(n1g.shape),
            _const_spec(win.shape, single=True), _const_spec(cw.shape), _const_spec(cb.shape),
            pl.BlockSpec((blk, HG_HEADS, HG_DK, HG_DV), lambda i: (i, 0, 0, 0)),
            pl.BlockSpec((blk, CONV_K - 1, CV_WIDTH), lambda i: (i, 0, 0)),
        ],
        out_specs=[
            pl.BlockSpec((blk, HG_HEADS, HG_DK, HG_DV), lambda i: (i, 0, 0, 0)),
            pl.BlockSpec((blk, CONV_K - 1, CV_WIDTH), lambda i: (i, 0, 0)),
            pl.BlockSpec((blk, HG_WIDTH), lambda i: (i, 0)),
            pl.BlockSpec((blk, CV_WIDTH), lambda i: (i, 0)),
            _const_spec((nseq, HG_WIDTH)),
        ],
        out_shape=[
            jax.ShapeDtypeStruct(st.shape, F32),
            jax.ShapeDtypeStruct(sc.shape, F32),
            jax.ShapeDtypeStruct((nseq, HG_WIDTH), F32),
            jax.ShapeDtypeStruct((nseq, CV_WIDTH), F32),
            jax.ShapeDtypeStruct((nseq, HG_WIDTH), F32),
        ],
        scratch_shapes=[
            pltpu.VMEM((3 * HG_HEADS, HG_DK, nseq), F32),
            pltpu.VMEM((nseq, HG_WIDTH + CV_WIDTH), F32),
        ],
        compiler_params=pltpu.CompilerParams(dimension_semantics=("arbitrary",), vmem_limit_bytes=VMEM_LIMIT),
        name="sample_state",
    )(x, lbp, n1g, win, cw, cb, st, sc)


def _sample_tail_kernel(x_ref, o_ref, dw_ref, gate_ref, og_ref, lng_ref, lnb_ref, wout_ref, n2g_ref, wup_ref,
                        wdn_ref, fg_ref, y_ref):
    og = og_ref[...]
    y_hg = [(_rms(o_ref[:, _head(h)], og) * gate_ref[:, _head(h)]).astype(BF16) for h in range(HG_HEADS)]
    y_cv = _silu(_layernorm(dw_ref[...], lng_ref[...], lnb_ref[...])).astype(BF16)
    ycat = jnp.concatenate(y_hg + [y_cv], axis=-1)
    h = x_ref[...] + jnp.dot(ycat, wout_ref[...], preferred_element_type=F32)
    y_ref[...] = _mlp(h, n2g_ref[...], wup_ref, wdn_ref, fg_ref[...])


def _sample_tail(x, o, dw, gate, og, lng, lnb, wout, n2g, wup, wdn, fg):
    return pl.pallas_call(
        _sample_tail_kernel,
        out_shape=jax.ShapeDtypeStruct(x.shape, F32),
        compiler_params=pltpu.CompilerParams(vmem_limit_bytes=VMEM_LIMIT),
        name="sample_tail",
    )(x, o, dw, gate, og, lng, lnb, wout, n2g, wup, wdn, fg)


def kernel(x_prompt, x_sample, state_hgrn, state_conv, meta_tokens, hg_lb, norm1_g, w_in, hg_onorm_g, conv_w, conv_b,
           conv_ln_g, conv_ln_b, w_out, norm2_g, w_up, w_down, final_g):
    assert state_hgrn.shape[0] == 1, "single-layer stack"
    bsz, seq, d = x_prompt.shape
    row = lambda a: a.reshape(1, -1)
    n1g, og, cb, lng, lnb, n2g, fg = (row(norm1_g[0]), row(hg_onorm_g[0]), row(conv_b[0]), row(conv_ln_g[0]),
                                      row(conv_ln_b[0]), row(norm2_g[0]), row(final_g))
    win, wout, wup, wdn = (w_in[0].astype(BF16), w_out[0].astype(BF16), w_up[0].astype(BF16),
                           w_down[0].astype(BF16))
    cw = conv_w[0]

    y_p, s_p, c_p = _prompt_layer(x_prompt, meta_tokens, hg_lb, n1g, win, og, cw, cb, lng, lnb, wout, n2g, wup, wdn,
                                  fg)

    x_s = x_sample.reshape(x_sample.shape[0], d)
    s_s, c_s, o_s, dw_s, gate_s = _sample_state(x_s, hg_lb, n1g, win, cw, cb, state_hgrn[0], state_conv[0])
    y_s = _sample_tail(x_s, o_s, dw_s, gate_s, og, lng, lnb, wout, n2g, wup, wdn, fg).reshape(x_sample.shape)

    return (y_p, y_s, s_p[None], c_p[None], s_s[None], c_s[None])
```

```python
import functools

import jax
import jax.numpy as jnp
from jax import lax
from jax.experimental import pallas as pl
from jax.experimental.pallas import tpu as pltpu

F32 = jnp.float32
BF16 = jnp.bfloat16

N_META = 16
HG_HEADS = 4
HG_DK = 128
HG_DV = 128
HG_WIDTH = HG_HEADS * HG_DK
CV_WIDTH = 512
CONV_K = 31
CHUNK = 64
EPS = 1e-6
C_Q, C_F, C_I, C_G, C_A, C_B, C_END = (0, HG_WIDTH, 2 * HG_WIDTH, 3 * HG_WIDTH, 4 * HG_WIDTH,
                                       4 * HG_WIDTH + CV_WIDTH, 4 * HG_WIDTH + 2 * CV_WIDTH)
HIST = 32
MXU_COLS = 256
TILE_T = 512
SAMPLE_BLOCK = 8
VMEM_LIMIT = 56 * 1024 * 1024


def _rms(x, g):
    return x * lax.rsqrt(jnp.mean(x * x, axis=-1, keepdims=True) + EPS) * g


def _layernorm(x, g, b):
    xc = x - jnp.mean(x, axis=-1, keepdims=True)
    return xc * lax.rsqrt(jnp.mean(xc * xc, axis=-1, keepdims=True) + EPS) * g + b


def _silu(x):
    return x * jax.nn.sigmoid(x)


def _lower_bound(lbp):
    e = jnp.exp(lbp - jnp.max(lbp, axis=0, keepdims=True))
    return e[0:1] / jnp.sum(e, axis=0, keepdims=True)


def _project(x, n1g, win_ref):
    hn = _rms(x, n1g).astype(BF16)
    return jnp.dot(hn, win_ref[...], preferred_element_type=F32)


def _tril(n):
    r = lax.broadcasted_iota(jnp.int32, (n, n), 0)
    c = lax.broadcasted_iota(jnp.int32, (n, n), 1)
    return r >= c


def _cumsum_rows(x):
    n = x.shape[0]
    return jnp.dot(_tril(n).astype(F32), x, precision=lax.Precision.HIGHEST, preferred_element_type=F32)


def _head(h):
    return slice(h * HG_DK, (h + 1) * HG_DK)


def _dot_nt(a, b):
    return lax.dot_general(a, b, (((1,), (1,)), ((), ())), preferred_element_type=F32)


def _dot_tn(a, b):
    return lax.dot_general(a, b, (((0,), (0,)), ((), ())), preferred_element_type=F32)


def _mlp(h, n2g, wup_ref, wdn_ref, fg):
    hn = _rms(h, n2g).astype(BF16)
    u = jnp.dot(hn, wup_ref[...], preferred_element_type=F32)
    a = jnp.square(jnp.maximum(u, 0.0)).astype(BF16)
    h2 = h + jnp.dot(a, wdn_ref[...], preferred_element_type=F32)
    return _rms(h2, fg)


def _dwconv_rows(e_ref, cw_ref, bias, base, rows):
    return jnp.concatenate([_dwconv_lane_group(e_ref, cw_ref, bias, base, rows, g) for g in range(CV_WIDTH // 128)],
                           axis=-1)


def _dwconv_lane_group(e_ref, cw_ref, bias, base, rows, g):
    sub = 8
    lanes = slice(g * 128, (g + 1) * 128)
    ext = e_ref[base - HIST:base + rows, lanes]
    acc = None
    for r in range(sub):
        u = None
        for a in range(-(-CONV_K // sub)):
            d = sub * a + r
            if d < CONV_K:
                lo = HIST - sub - sub * a
                term = cw_ref[CONV_K - 1 - d:CONV_K - d, lanes] * ext[lo:lo + rows + sub]
                u = term if u is None else u + term
        shifted = u[sub:] if r == 0 else pltpu.roll(u, r, 0)[sub:]
        acc = shifted if acc is None else acc + shifted
    return acc + bias[:, lanes]


def _prompt_layer_kernel(x_ref, meta_ref, lbp_ref, n1g_ref, win_ref, og_ref, cw_ref, cb_ref, lng_ref, lnb_ref,
                         wout_ref, n2g_ref, wup_ref, wdn_ref, fg_ref, y_out_ref, sfin_ref, cfin_ref,
                         z_ref, y_ref, st_ref, e_ref, h_ref, hn_ref, a_ref, acc_ref, *, tile_t, nt):
    s = pl.program_id(0)
    ns = pl.num_programs(0) - 1
    mixing = s < ns
    t = lax.rem(jnp.minimum(s, ns - 1), nt)
    cur = lax.rem(s, 2)
    prev = 1 - cur
    lb = _lower_bound(lbp_ref[...])
    n1g = n1g_ref[...]

    @pl.when(s == 0)
    def _no_previous_tile():
        h_ref[1] = jnp.zeros(h_ref.shape[1:], F32)

    @pl.when(mixing & (t == 0))
    def _meta_tokens():
        zm = _project(meta_ref[...], n1g, win_ref)
        f = lb + (1.0 - lb) * jax.nn.sigmoid(zm[:, C_F:C_I])
        b = _cumsum_rows(jnp.log(f))
        k_end = ((1.0 - f) * jnp.exp(b[N_META - 1:N_META] - b)).astype(BF16)
        v = zm[:, C_I:C_G].astype(BF16)
        for h in range(HG_HEADS):
            st_ref[h] = _dot_tn(v[:, _head(h)], k_end[:, _head(h)])
        e_ref[0:HIST - N_META, :] = jnp.zeros((HIST - N_META, CV_WIDTH), F32)
        e_ref[HIST - N_META:HIST, :] = zm[:, C_A:C_B] * jax.nn.sigmoid(zm[:, C_B:C_END])

    x = x_ref[0]
    z_ref[...] = _project(x, n1g, win_ref)
    h_prev = h_ref[prev]
    hn_ref[...] = _rms(h_prev, n2g_ref[...]).astype(BF16)
    acc_ref[...] = h_prev
    causal = _tril(CHUNK)
    og = og_ref[...]
    cb = cb_ref[...]
    n_chunks = tile_t // CHUNK
    ff = wup_ref.shape[1] // n_chunks
    fft = max(ff // HG_HEADS, MXU_COLS)
    dt = acc_ref.shape[1] // (CV_WIDTH // 128)

    for c in range(n_chunks):
        rows = slice(c * CHUNK, (c + 1) * CHUNK)
        f = lb + (1.0 - lb) * jax.nn.sigmoid(z_ref[rows, C_F:C_I])
        k = 1.0 - f
        b = _cumsum_rows(jnp.log(f))
        b_end = b[CHUNK - 1:CHUNK]
        q_dec = (_silu(z_ref[rows, C_Q:C_F]) * jnp.exp(b)).astype(BF16)
        k_inv = (k * jnp.exp(-b)).astype(BF16)
        k_end = (k * jnp.exp(b_end - b)).astype(BF16)
        decay = jnp.exp(b_end)
        v = z_ref[rows, C_I:C_G].astype(BF16)
        gate = _silu(z_ref[rows, C_G:C_A])
        e_ref[HIST + c * CHUNK:HIST + (c + 1) * CHUNK, :] = z_ref[rows, C_A:C_B] * jax.nn.sigmoid(z_ref[rows, C_B:C_END])
        for h in range(HG_HEADS):
            if (h + 1) * fft <= ff:
                ucols = slice(c * ff + h * fft, c * ff + (h + 1) * fft)
                u = jnp.dot(hn_ref[...], wup_ref[:, ucols], preferred_element_type=F32)
                a_ref[:, h * fft:(h + 1) * fft] = jnp.square(jnp.maximum(u, 0.0)).astype(BF16)
            hs = _head(h)
            scores = jnp.where(causal, _dot_nt(q_dec[:, hs], k_inv[:, hs]), 0.0)
            s_t = st_ref[h]
            o = (jnp.dot(scores.astype(BF16), v[:, hs], preferred_element_type=F32)
                 + _dot_nt(q_dec[:, hs], s_t.astype(BF16)))
            st_ref[h] = s_t * decay[:, hs] + _dot_tn(v[:, hs], k_end[:, hs])
            y_ref[rows, hs] = (_rms(o, og) * gate[:, hs]).astype(BF16)
        dw = []
        for g in range(CV_WIDTH // 128):
            dw.append(_dwconv_lane_group(e_ref, cw_ref, cb, HIST + c * CHUNK, CHUNK, g))
            dcols = slice(g * dt, (g + 1) * dt)
            acc_ref[:, dcols] += jnp.dot(a_ref[...], wdn_ref[c * ff:(c + 1) * ff, dcols], preferred_element_type=F32)
        dw = jnp.concatenate(dw, axis=-1)
        y_ref[rows, HG_WIDTH:] = _silu(_layernorm(dw, lng_ref[...], lnb_ref[...])).astype(BF16)

    y_out_ref[0] = _rms(acc_ref[...], fg_ref[...])
    h_ref[cur] = x + jnp.dot(y_ref[...], wout_ref[...], preferred_element_type=F32)
    e_ref[0:HIST, :] = e_ref[tile_t:tile_t + HIST, :]

    @pl.when(mixing & (t == nt - 1))
    def _final_state():
        for h in range(HG_HEADS):
            sfin_ref[0, h] = st_ref[h].T
        cfin_ref[0] = e_ref[tile_t + HIST - (CONV_K - 1):tile_t + HIST, :]


def _const_spec(shape, single=False):
    zeros = (0,) * len(shape)
    if single:
        return pl.BlockSpec(shape, lambda *_: zeros, pipeline_mode=pl.Buffered(1))
    return pl.BlockSpec(shape, lambda *_: zeros)


def _prompt_layer(x, meta, lbp, n1g, win, og, cw, cb, lng, lnb, wout, n2g, wup, wdn, fg):
    bsz, seq, d = x.shape
    nt = seq // TILE_T
    ns = bsz * nt

    def mix_tile(s):
        s1 = jnp.minimum(s, ns - 1)
        return s1 // nt, s1 % nt

    def mlp_tile(s):
        s2 = jnp.maximum(s - 1, 0)
        return s2 // nt, s2 % nt

    return pl.pallas_call(
        functools.partial(_prompt_layer_kernel, tile_t=TILE_T, nt=nt),
        grid=(ns + 1,),
        in_specs=[
            pl.BlockSpec((1, TILE_T, d), lambda s: (*mix_tile(s), 0)),
            _const_spec(meta.shape), _const_spec(lbp.shape), _const_spec(n1g.shape),
            _const_spec(win.shape, single=True), _const_spec(og.shape), _const_spec(cw.shape),
            _const_spec(cb.shape), _const_spec(lng.shape), _const_spec(lnb.shape),
            _const_spec(wout.shape, single=True), _const_spec(n2g.shape),
            _const_spec(wup.shape, single=True), _const_spec(wdn.shape, single=True), _const_spec(fg.shape),
        ],
        out_specs=[
            pl.BlockSpec((1, TILE_T, d), lambda s: (*mlp_tile(s), 0)),
            pl.BlockSpec((1, HG_HEADS, HG_DK, HG_DV), lambda s: (mix_tile(s)[0], 0, 0, 0)),
            pl.BlockSpec((1, CONV_K - 1, CV_WIDTH), lambda s: (mix_tile(s)[0], 0, 0)),
        ],
        out_shape=[
            jax.ShapeDtypeStruct((bsz, seq, d), F32),
            jax.ShapeDtypeStruct((bsz, HG_HEADS, HG_DK, HG_DV), F32),
            jax.ShapeDtypeStruct((bsz, CONV_K - 1, CV_WIDTH), F32),
        ],
        scratch_shapes=[
            pltpu.VMEM((TILE_T, C_END), F32),
            pltpu.VMEM((TILE_T, HG_WIDTH + CV_WIDTH), BF16),
            pltpu.VMEM((HG_HEADS, HG_DV, HG_DK), F32),
            pltpu.VMEM((HIST + TILE_T, CV_WIDTH), F32),
            pltpu.VMEM((2, TILE_T, d), F32),
            pltpu.VMEM((TILE_T, d), BF16),
            pltpu.VMEM((TILE_T, wup.shape[1] // (TILE_T // CHUNK)), BF16),
            pltpu.VMEM((TILE_T, d), F32),
        ],
        compiler_params=pltpu.CompilerParams(dimension_semantics=("arbitrary",), vmem_limit_bytes=VMEM_LIMIT),
        name="prompt_layer",
    )(x, meta, lbp, n1g, win, og, cw, cb, lng, lnb, wout, n2g, wup, wdn, fg)


def _sample_state_kernel(x_ref, lbp_ref, n1g_ref, win_ref, cw_ref, cb_ref, st_ref, sc_ref,
                         nst_ref, nsc_ref, o_ref, dw_ref, gate_ref, ft_ref, vg_ref, *, block):
    i = pl.program_id(0)

    @pl.when(i == 0)
    def _gates():
        lb = _lower_bound(lbp_ref[...])
        z = _project(x_ref[...], n1g_ref[...], win_ref)
        f = lb + (1.0 - lb) * jax.nn.sigmoid(z[:, C_F:C_I])
        k = 1.0 - f
        q = _silu(z[:, C_Q:C_F])
        for h in range(HG_HEADS):
            ft_ref[h] = f[:, _head(h)].T
            ft_ref[HG_HEADS + h] = k[:, _head(h)].T
            ft_ref[2 * HG_HEADS + h] = q[:, _head(h)].T
        vg_ref[:, 0:HG_WIDTH] = z[:, C_I:C_G]
        vg_ref[:, HG_WIDTH:] = z[:, C_A:C_B] * jax.nn.sigmoid(z[:, C_B:C_END])
        gate_ref[...] = _silu(z[:, C_G:C_A])

    nseq = ft_ref.shape[2]
    shift = lax.rem(nseq - i * block, nseq)
    vg = vg_ref[pl.ds(pl.multiple_of(i * block, block), block), :]
    for h in range(HG_HEADS):
        hs = _head(h)
        f_t = pltpu.roll(ft_ref[h], shift, 1)
        k_t = pltpu.roll(ft_ref[HG_HEADS + h], shift, 1)
        q_t = pltpu.roll(ft_ref[2 * HG_HEADS + h], shift, 1)
        for j in range(block):
            v_row = vg[j:j + 1, hs]
            s_new = f_t[:, j:j + 1] * st_ref[j, h] + k_t[:, j:j + 1] * v_row
            nst_ref[j, h] = s_new
            o_ref[j:j + 1, hs] = jnp.sum(q_t[:, j:j + 1] * s_new, axis=0, keepdims=True)
    for j in range(block):
        glu_row = vg[j:j + 1, HG_WIDTH:]
        dw_ref[j:j + 1, :] = (jnp.sum(sc_ref[j] * cw_ref[0:CONV_K - 1, :], axis=0, keepdims=True)
                              + glu_row * cw_ref[CONV_K - 1:CONV_K, :] + cb_ref[...])
        nsc_ref[j, 0:CONV_K - 2, :] = sc_ref[j, 1:CONV_K - 1, :]
        nsc_ref[j, CONV_K - 2:CONV_K - 1, :] = glu_row


def _sample_state(x, lbp, n1g, win, cw, cb, st, sc):
    nseq, d = x.shape
    blk = SAMPLE_BLOCK
    return pl.pallas_call(
        functools.partial(_sample_state_kernel, block=blk),
        grid=(nseq // blk,),
        in_specs=[
            _const_spec(x.shape), _const_spec(lbp.shape), _const_spec(n1g.shape),
            _const_spec(win.shape, single=True), _const_spec(cw.shape), _const_spec(cb.shape),
            pl.BlockSpec((blk, HG_HEADS, HG_DK, HG_DV), lambda i: (i, 0, 0, 0)),
            pl.BlockSpec((blk, CONV_K - 1, CV_WIDTH), lambda i: (i, 0, 0)),
        ],
        out_specs=[
            pl.BlockSpec((blk, HG_HEADS, HG_DK, HG_DV), lambda i: (i, 0, 0, 0)),
            pl.BlockSpec((blk, CONV_K - 1, CV_WIDTH), lambda i: (i, 0, 0)),
            pl.BlockSpec((blk, HG_WIDTH), lambda i: (i, 0)),
            pl.BlockSpec((blk, CV_WIDTH), lambda i: (i, 0)),
            _const_spec((nseq, HG_WIDTH)),
        ],
        out_shape=[
            jax.ShapeDtypeStruct(st.shape, F32),
            jax.ShapeDtypeStruct(sc.shape, F32),
            jax.ShapeDtypeStruct((nseq, HG_WIDTH), F32),
            jax.ShapeDtypeStruct((nseq, CV_WIDTH), F32),
            jax.ShapeDtypeStruct((nseq, HG_WIDTH), F32),
        ],
        scratch_shapes=[
            pltpu.VMEM((3 * HG_HEADS, HG_DK, nseq), F32),
            pltpu.VMEM((nseq, HG_WIDTH + CV_WIDTH), F32),
        ],
        compiler_params=pltpu.CompilerParams(dimension_semantics=("arbitrary",), vmem_limit_bytes=VMEM_LIMIT),
        name="sample_state",
    )(x, lbp, n1g, win, cw, cb, st, sc)


def _sample_tail_kernel(x_ref, o_ref, dw_ref, gate_ref, og_ref, lng_ref, lnb_ref, wout_ref, n2g_ref, wup_ref,
                        wdn_ref, fg_ref, y_ref):
    og = og_ref[...]
    y_hg = [(_rms(o_ref[:, _head(h)], og) * gate_ref[:, _head(h)]).astype(BF16) for h in range(HG_HEADS)]
    y_cv = _silu(_layernorm(dw_ref[...], lng_ref[...], lnb_ref[...])).astype(BF16)
    ycat = jnp.concatenate(y_hg + [y_cv], axis=-1)
    h = x_ref[...] + jnp.dot(ycat, wout_ref[...], preferred_element_type=F32)
    y_ref[...] = _mlp(h, n2g_ref[...], wup_ref, wdn_ref, fg_ref[...])


def _sample_tail(x, o, dw, gate, og, lng, lnb, wout, n2g, wup, wdn, fg):
    return pl.pallas_call(
        _sample_tail_kernel,
        out_shape=jax.ShapeDtypeStruct(x.shape, F32),
        compiler_params=pltpu.CompilerParams(vmem_limit_bytes=VMEM_LIMIT),
        name="sample_tail",
    )(x, o, dw, gate, og, lng, lnb, wout, n2g, wup, wdn, fg)


def kernel(x_prompt, x_sample, state_hgrn, state_conv, meta_tokens, hg_lb, norm1_g, w_in, hg_onorm_g, conv_w, conv_b,
           conv_ln_g, conv_ln_b, w_out, norm2_g, w_up, w_down, final_g):
    assert state_hgrn.shape[0] == 1, "single-layer stack"
    bsz, seq, d = x_prompt.shape
    row = lambda a: a.reshape(1, -1)
    n1g, og, cb, lng, lnb, n2g, fg = (row(norm1_g[0]), row(hg_onorm_g[0]), row(conv_b[0]), row(conv_ln_g[0]),
                                      row(conv_ln_b[0]), row(norm2_g[0]), row(final_g))
    win, wout, wup, wdn = (w_in[0].astype(BF16), w_out[0].astype(BF16), w_up[0].astype(BF16),
                           w_down[0].astype(BF16))
    cw = conv_w[0]

    y_p, s_p, c_p = _prompt_layer(x_prompt, meta_tokens, hg_lb, n1g, win, og, cw, cb, lng, lnb, wout, n2g, wup, wdn,
                                  fg)

    x_s = x_sample.reshape(x_sample.shape[0], d)
    s_s, c_s, o_s, dw_s, gate_s = _sample_state(x_s, hg_lb, n1g, win, cw, cb, state_hgrn[0], state_conv[0])
    y_s = _sample_tail(x_s, o_s, dw_s, gate_s, og, lng, lnb, wout, n2g, wup, wdn, fg).reshape(x_sample.shape)

    return (y_p, y_s, s_p[None], c_p[None], s_s[None], c_s[None])
```

```python
import functools

import jax
import jax.numpy as jnp
from jax import lax
from jax.experimental import pallas as pl
from jax.experimental.pallas import tpu as pltpu

F32 = jnp.float32
BF16 = jnp.bfloat16

N_META = 16
HG_HEADS = 4
HG_DK = 128
HG_DV = 128
HG_WIDTH = HG_HEADS * HG_DK
CV_WIDTH = 512
CONV_K = 31
CHUNK = 64
EPS = 1e-6
C_Q, C_F, C_I, C_G, C_A, C_B, C_END = (0, HG_WIDTH, 2 * HG_WIDTH, 3 * HG_WIDTH, 4 * HG_WIDTH,
                                       4 * HG_WIDTH + CV_WIDTH, 4 * HG_WIDTH + 2 * CV_WIDTH)
HIST = 32
MXU_COLS = 256
TILE_T = 512
SAMPLE_BLOCK = 8
VMEM_LIMIT = 56 * 1024 * 1024


def _rms(x, g):
    return x * lax.rsqrt(jnp.mean(x * x, axis=-1, keepdims=True) + EPS) * g


def _layernorm(x, g, b):
    xc = x - jnp.mean(x, axis=-1, keepdims=True)
    return xc * lax.rsqrt(jnp.mean(xc * xc, axis=-1, keepdims=True) + EPS) * g + b


def _silu(x):
    return x * jax.nn.sigmoid(x)


def _lower_bound(lbp):
    e = jnp.exp(lbp - jnp.max(lbp, axis=0, keepdims=True))
    return e[0:1] / jnp.sum(e, axis=0, keepdims=True)


def _project(x, n1g, win_ref):
    hn = _rms(x, n1g).astype(BF16)
    return jnp.dot(hn, win_ref[...], preferred_element_type=F32)


def _tril(n):
    r = lax.broadcasted_iota(jnp.int32, (n, n), 0)
    c = lax.broadcasted_iota(jnp.int32, (n, n), 1)
    return r >= c


def _cumsum_rows(x):
    n = x.shape[0]
    return jnp.dot(_tril(n).astype(F32), x, precision=lax.Precision.HIGHEST, preferred_element_type=F32)


def _head(h):
    return slice(h * HG_DK, (h + 1) * HG_DK)


def _dot_nt(a, b):
    return lax.dot_general(a, b, (((1,), (1,)), ((), ())), preferred_element_type=F32)


def _dot_tn(a, b):
    return lax.dot_general(a, b, (((0,), (0,)), ((), ())), preferred_element_type=F32)


def _mlp(h, n2g, wup_ref, wdn_ref, fg):
    hn = _rms(h, n2g).astype(BF16)
    u = jnp.dot(hn, wup_ref[...], preferred_element_type=F32)
    a = jnp.square(jnp.maximum(u, 0.0)).astype(BF16)
    h2 = h + jnp.dot(a, wdn_ref[...], preferred_element_type=F32)
    return _rms(h2, fg)


def _dwconv_lane_group(e_ref, cw_ref, bias, base, rows, g):
    sub = 8
    lanes = slice(g * 128, (g + 1) * 128)
    ext = e_ref[base - HIST:base + rows, lanes]
    acc = None
    for r in range(sub):
        u = None
        for a in range(-(-CONV_K // sub)):
            d = sub * a + r
            if d < CONV_K:
                lo = HIST - sub - sub * a
                term = cw_ref[CONV_K - 1 - d:CONV_K - d, lanes] * ext[lo:lo + rows + sub]
                u = term if u is None else u + term
        shifted = u[sub:] if r == 0 else pltpu.roll(u, r, 0)[sub:]
        acc = shifted if acc is None else acc + shifted
    return acc + bias[:, lanes]


def _prompt_layer_kernel(x_ref, meta_ref, lbp_ref, n1g_ref, win_ref, og_ref, cw_ref, cb_ref, lng_ref, lnb_ref,
                         wout_ref, n2g_ref, wup_ref, wdn_ref, fg_ref, y_out_ref, sfin_ref, cfin_ref,
                         z_ref, y_ref, st_ref, e_ref, h_ref, hn_ref, a_ref, acc_ref, *, tile_t, nt):
    s = pl.program_id(0)
    ns = pl.num_programs(0) - 1
    mixing = s < ns
    t = lax.rem(jnp.minimum(s, ns - 1), nt)
    cur = lax.rem(s, 2)
    prev = 1 - cur
    lb = _lower_bound(lbp_ref[...])
    n1g = n1g_ref[...]

    @pl.when(s == 0)
    def _no_previous_tile():
        h_ref[1] = jnp.zeros(h_ref.shape[1:], F32)

    @pl.when(mixing & (t == 0))
    def _meta_tokens():
        zm = _project(meta_ref[...], n1g, win_ref)
        f = lb + (1.0 - lb) * jax.nn.sigmoid(zm[:, C_F:C_I])
        b = _cumsum_rows(jnp.log(f))
        k_end = ((1.0 - f) * jnp.exp(b[N_META - 1:N_META] - b)).astype(BF16)
        v = zm[:, C_I:C_G].astype(BF16)
        for h in range(HG_HEADS):
            st_ref[h] = _dot_tn(v[:, _head(h)], k_end[:, _head(h)])
        e_ref[0:HIST - N_META, :] = jnp.zeros((HIST - N_META, CV_WIDTH), F32)
        e_ref[HIST - N_META:HIST, :] = zm[:, C_A:C_B] * jax.nn.sigmoid(zm[:, C_B:C_END])

    x = x_ref[0]
    z_ref[...] = _project(x, n1g, win_ref)
    h_prev = h_ref[prev]
    hn_ref[...] = _rms(h_prev, n2g_ref[...]).astype(BF16)
    acc_ref[...] = h_prev
    causal = _tril(CHUNK)
    og = og_ref[...]
    cb = cb_ref[...]
    n_chunks = tile_t // CHUNK
    ff = wup_ref.shape[1] // n_chunks
    fft = max(ff // HG_HEADS, MXU_COLS)
    dt = acc_ref.shape[1] // (CV_WIDTH // 128)

    for c in range(n_chunks):
        rows = slice(c * CHUNK, (c + 1) * CHUNK)
        f = lb + (1.0 - lb) * jax.nn.sigmoid(z_ref[rows, C_F:C_I])
        k = 1.0 - f
        b = _cumsum_rows(jnp.log(f))
        b_end = b[CHUNK - 1:CHUNK]
        q_dec = (_silu(z_ref[rows, C_Q:C_F]) * jnp.exp(b)).astype(BF16)
        k_inv = (k * jnp.exp(-b)).astype(BF16)
        k_end = (k * jnp.exp(b_end - b)).astype(BF16)
        decay = jnp.exp(b_end)
        v = z_ref[rows, C_I:C_G].astype(BF16)
        gate = _silu(z_ref[rows, C_G:C_A])
        e_ref[HIST + c * CHUNK:HIST + (c + 1) * CHUNK, :] = z_ref[rows, C_A:C_B] * jax.nn.sigmoid(z_ref[rows, C_B:C_END])
        for h in range(HG_HEADS):
            if (h + 1) * fft <= ff:
                ucols = slice(c * ff + h * fft, c * ff + (h + 1) * fft)
                u = jnp.dot(hn_ref[...], wup_ref[:, ucols], preferred_element_type=F32)
                a_ref[:, h * fft:(h + 1) * fft] = jnp.square(jnp.maximum(u, 0.0)).astype(BF16)
            hs = _head(h)
            scores = jnp.where(causal, _dot_nt(q_dec[:, hs], k_inv[:, hs]), 0.0)
            s_t = st_ref[h]
            o = (jnp.dot(scores.astype(BF16), v[:, hs], preferred_element_type=F32)
                 + _dot_nt(q_dec[:, hs], s_t.astype(BF16)))
            st_ref[h] = s_t * decay[:, hs] + _dot_tn(v[:, hs], k_end[:, hs])
            y_ref[rows, hs] = (_rms(o, og) * gate[:, hs]).astype(BF16)
        dw = []
        for g in range(CV_WIDTH // 128):
            dw.append(_dwconv_lane_group(e_ref, cw_ref, cb, HIST + c * CHUNK, CHUNK, g))
            dcols = slice(g * dt, (g + 1) * dt)
            acc_ref[:, dcols] += jnp.dot(a_ref[...], wdn_ref[c * ff:(c + 1) * ff, dcols], preferred_element_type=F32)
        dw = jnp.concatenate(dw, axis=-1)
        y_ref[rows, HG_WIDTH:] = _silu(_layernorm(dw, lng_ref[...], lnb_ref[...])).astype(BF16)

    y_out_ref[0] = _rms(acc_ref[...], fg_ref[...])
    h_ref[cur] = x + jnp.dot(y_ref[...], wout_ref[...], preferred_element_type=F32)
    e_ref[0:HIST, :] = e_ref[tile_t:tile_t + HIST, :]

    @pl.when(mixing & (t == nt - 1))
    def _final_state():
        for h in range(HG_HEADS):
            sfin_ref[0, h] = st_ref[h].T
        cfin_ref[0] = e_ref[tile_t + HIST - (CONV_K - 1):tile_t + HIST, :]


def _const_spec(shape, single=False):
    zeros = (0,) * len(shape)
    if single:
        return pl.BlockSpec(shape, lambda *_: zeros, pipeline_mode=pl.Buffered(1))
    return pl.BlockSpec(shape, lambda *_: zeros)


def _prompt_layer(x, meta, lbp, n1g, win, og, cw, cb, lng, lnb, wout, n2g, wup, wdn, fg):
    bsz, seq, d = x.shape
    nt = seq // TILE_T
    ns = bsz * nt

    def mix_tile(s):
        s1 = jnp.minimum(s, ns - 1)
        return s1 // nt, s1 % nt

    def mlp_tile(s):
        s2 = jnp.maximum(s - 1, 0)
        return s2 // nt, s2 % nt

    return pl.pallas_call(
        functools.partial(_prompt_layer_kernel, tile_t=TILE_T, nt=nt),
        grid=(ns + 1,),
        in_specs=[
            pl.BlockSpec((1, TILE_T, d), lambda s: (*mix_tile(s), 0)),
            _const_spec(meta.shape), _const_spec(lbp.shape), _const_spec(n1g.shape),
            _const_spec(win.shape, single=True), _const_spec(og.shape), _const_spec(cw.shape),
            _const_spec(cb.shape), _const_spec(lng.shape), _const_spec(lnb.shape),
            _const_spec(wout.shape, single=True), _const_spec(n2g.shape),
            _const_spec(wup.shape, single=True), _const_spec(wdn.shape, single=True), _const_spec(fg.shape),
        ],
        out_specs=[
            pl.BlockSpec((1, TILE_T, d), lambda s: (*mlp_tile(s), 0)),
            pl.BlockSpec((1, HG_HEADS, HG_DK, HG_DV), lambda s: (mix_tile(s)[0], 0, 0, 0)),
            pl.BlockSpec((1, CONV_K - 1, CV_WIDTH), lambda s: (mix_tile(s)[0], 0, 0)),
        ],
        out_shape=[
            jax.ShapeDtypeStruct((bsz, seq, d), F32),
            jax.ShapeDtypeStruct((bsz, HG_HEADS, HG_DK, HG_DV), F32),
            jax.ShapeDtypeStruct((bsz, CONV_K - 1, CV_WIDTH), F32),
        ],
        scratch_shapes=[
            pltpu.VMEM((TILE_T, C_END), F32),
            pltpu.VMEM((TILE_T, HG_WIDTH + CV_WIDTH), BF16),
            pltpu.VMEM((HG_HEADS, HG_DV, HG_DK), F32),
            pltpu.VMEM((HIST + TILE_T, CV_WIDTH), F32),
            pltpu.VMEM((2, TILE_T, d), F32),
            pltpu.VMEM((TILE_T, d), BF16),
            pltpu.VMEM((TILE_T, wup.shape[1] // (TILE_T // CHUNK)), BF16),
            pltpu.VMEM((TILE_T, d), F32),
        ],
        compiler_params=pltpu.CompilerParams(dimension_semantics=("arbitrary",), vmem_limit_bytes=VMEM_LIMIT),
        name="prompt_layer",
    )(x, meta, lbp, n1g, win, og, cw, cb, lng, lnb, wout, n2g, wup, wdn, fg)


def _sample_state_kernel(x_ref, lbp_ref, n1g_ref, win_ref, cw_ref, cb_ref, st_ref, sc_ref,
                         nst_ref, nsc_ref, o_ref, dw_ref, gate_ref, ft_ref, qvg_ref, *, block):
    i = pl.program_id(0)

    @pl.when(i == 0)
    def _gates():
        lb = _lower_bound(lbp_ref[...])
        z = _project(x_ref[...], n1g_ref[...], win_ref)
        f = lb + (1.0 - lb) * jax.nn.sigmoid(z[:, C_F:C_I])
        for h in range(HG_HEADS):
            ft_ref[h] = f[:, _head(h)].T
        qvg_ref[:, 0:HG_WIDTH] = _silu(z[:, C_Q:C_F])
        qvg_ref[:, HG_WIDTH:2 * HG_WIDTH] = z[:, C_I:C_G]
        qvg_ref[:, 2 * HG_WIDTH:] = z[:, C_A:C_B] * jax.nn.sigmoid(z[:, C_B:C_END])
        gate_ref[...] = _silu(z[:, C_G:C_A])

    nseq = ft_ref.shape[2]
    shift = lax.rem(nseq - i * block, nseq)
    qvg = qvg_ref[pl.ds(pl.multiple_of(i * block, block), block), :]
    q = qvg[:, 0:HG_WIDTH].astype(BF16)
    v = qvg[:, HG_WIDTH:2 * HG_WIDTH]
    glu = qvg[:, 2 * HG_WIDTH:]
    seq_row = lax.broadcasted_iota(jnp.int32, (block, HG_DV), 0)
    for h in range(HG_HEADS):
        hs = _head(h)
        f_t = pltpu.roll(ft_ref[h], shift, 1)
        o = jnp.zeros((block, HG_DV), F32)
        for j in range(block):
            f_col = f_t[:, j:j + 1]
            s_new = f_col * st_ref[j, h] + (1.0 - f_col) * v[j:j + 1, hs]
            nst_ref[j, h] = s_new
            o_all = jnp.dot(q[:, hs], s_new.astype(BF16), preferred_element_type=F32)
            o = jnp.where(seq_row == j, o_all, o)
        o_ref[:, hs] = o
    taps = CONV_K - 1
    dw = cb_ref[...] + cw_ref[taps:taps + 1, :] * glu
    for j in range(taps):
        dw = dw + cw_ref[j:j + 1, :] * sc_ref[j]
    dw_ref[...] = dw
    nsc_ref[0:taps - 1] = sc_ref[1:taps]
    nsc_ref[taps - 1] = glu


def _sample_state(x, lbp, n1g, win, cw, cb, st, sc):
    nseq, d = x.shape
    blk = SAMPLE_BLOCK
    taps = sc.shape[0]
    return pl.pallas_call(
        functools.partial(_sample_state_kernel, block=blk),
        grid=(nseq // blk,),
        in_specs=[
            _const_spec(x.shape), _const_spec(lbp.shape), _const_spec(n1g.shape),
            _const_spec(win.shape, single=True), _const_spec(cw.shape), _const_spec(cb.shape),
            pl.BlockSpec((blk, HG_HEADS, HG_DK, HG_DV), lambda i: (i, 0, 0, 0)),
            pl.BlockSpec((taps, blk, CV_WIDTH), lambda i: (0, i, 0)),
        ],
        out_specs=[
            pl.BlockSpec((blk, HG_HEADS, HG_DK, HG_DV), lambda i: (i, 0, 0, 0)),
            pl.BlockSpec((taps, blk, CV_WIDTH), lambda i: (0, i, 0)),
            pl.BlockSpec((blk, HG_WIDTH), lambda i: (i, 0)),
            pl.BlockSpec((blk, CV_WIDTH), lambda i: (i, 0)),
            _const_spec((nseq, HG_WIDTH)),
        ],
        out_shape=[
            jax.ShapeDtypeStruct(st.shape, F32),
            jax.ShapeDtypeStruct(sc.shape, F32),
            jax.ShapeDtypeStruct((nseq, HG_WIDTH), F32),
            jax.ShapeDtypeStruct((nseq, CV_WIDTH), F32),
            jax.ShapeDtypeStruct((nseq, HG_WIDTH), F32),
        ],
        scratch_shapes=[
            pltpu.VMEM((HG_HEADS, HG_DK, nseq), F32),
            pltpu.VMEM((nseq, 2 * HG_WIDTH + CV_WIDTH), F32),
        ],
        compiler_params=pltpu.CompilerParams(dimension_semantics=("arbitrary",), vmem_limit_bytes=VMEM_LIMIT),
        name="sample_state",
    )(x, lbp, n1g, win, cw, cb, st, sc)


def _sample_tail_kernel(x_ref, o_ref, dw_ref, gate_ref, og_ref, lng_ref, lnb_ref, wout_ref, n2g_ref, wup_ref,
                        wdn_ref, fg_ref, y_ref):
    og = og_ref[...]
    y_hg = [(_rms(o_ref[:, _head(h)], og) * gate_ref[:, _head(h)]).astype(BF16) for h in range(HG_HEADS)]
    y_cv = _silu(_layernorm(dw_ref[...], lng_ref[...], lnb_ref[...])).astype(BF16)
    ycat = jnp.concatenate(y_hg + [y_cv], axis=-1)
    h = x_ref[...] + jnp.dot(ycat, wout_ref[...], preferred_element_type=F32)
    y_ref[...] = _mlp(h, n2g_ref[...], wup_ref, wdn_ref, fg_ref[...])


def _sample_tail(x, o, dw, gate, og, lng, lnb, wout, n2g, wup, wdn, fg):
    return pl.pallas_call(
        _sample_tail_kernel,
        out_shape=jax.ShapeDtypeStruct(x.shape, F32),
        compiler_params=pltpu.CompilerParams(vmem_limit_bytes=VMEM_LIMIT),
        name="sample_tail",
    )(x, o, dw, gate, og, lng, lnb, wout, n2g, wup, wdn, fg)


def kernel(x_prompt, x_sample, state_hgrn, state_conv, meta_tokens, hg_lb, norm1_g, w_in, hg_onorm_g, conv_w, conv_b,
           conv_ln_g, conv_ln_b, w_out, norm2_g, w_up, w_down, final_g):
    assert state_hgrn.shape[0] == 1, "single-layer stack"
    bsz, seq, d = x_prompt.shape
    row = lambda a: a.reshape(1, -1)
    n1g, og, cb, lng, lnb, n2g, fg = (row(norm1_g[0]), row(hg_onorm_g[0]), row(conv_b[0]), row(conv_ln_g[0]),
                                      row(conv_ln_b[0]), row(norm2_g[0]), row(final_g))
    win, wout, wup, wdn = (w_in[0].astype(BF16), w_out[0].astype(BF16), w_up[0].astype(BF16),
                           w_down[0].astype(BF16))
    cw = conv_w[0]

    y_p, s_p, c_p = _prompt_layer(x_prompt, meta_tokens, hg_lb, n1g, win, og, cw, cb, lng, lnb, wout, n2g, wup, wdn,
                                  fg)

    x_s = x_sample.reshape(x_sample.shape[0], d)
    s_s, c_s, o_s, dw_s, gate_s = _sample_state(x_s, hg_lb, n1g, win, cw, cb, state_hgrn[0],
                                                jnp.swapaxes(state_conv[0], 0, 1))
    y_s = _sample_tail(x_s, o_s, dw_s, gate_s, og, lng, lnb, wout, n2g, wup, wdn, fg).reshape(x_sample.shape)

    return (y_p, y_s, s_p[None], c_p[None], s_s[None], jnp.swapaxes(c_s, 0, 1)[None])
```

```python
import functools

import jax
import jax.numpy as jnp
from jax import lax
from jax.experimental import pallas as pl
from jax.experimental.pallas import tpu as pltpu

F32 = jnp.float32
BF16 = jnp.bfloat16

N_META = 16
HG_HEADS = 4
HG_DK = 128
HG_DV = 128
HG_WIDTH = HG_HEADS * HG_DK
CV_WIDTH = 512
CONV_K = 31
CHUNK = 64
EPS = 1e-6
C_Q, C_F, C_I, C_G, C_A, C_B, C_END = (0, HG_WIDTH, 2 * HG_WIDTH, 3 * HG_WIDTH, 4 * HG_WIDTH,
                                       4 * HG_WIDTH + CV_WIDTH, 4 * HG_WIDTH + 2 * CV_WIDTH)
HIST = 32
MXU_COLS = 256
TILE_T = 512
SAMPLE_BLOCK = 16
VMEM_LIMIT = 56 * 1024 * 1024


def _rms(x, g):
    return x * lax.rsqrt(jnp.mean(x * x, axis=-1, keepdims=True) + EPS) * g


def _layernorm(x, g, b):
    xc = x - jnp.mean(x, axis=-1, keepdims=True)
    return xc * lax.rsqrt(jnp.mean(xc * xc, axis=-1, keepdims=True) + EPS) * g + b


def _silu(x):
    return x * jax.nn.sigmoid(x)


def _lower_bound(lbp):
    e = jnp.exp(lbp - jnp.max(lbp, axis=0, keepdims=True))
    return e[0:1] / jnp.sum(e, axis=0, keepdims=True)


def _project(x, n1g, win_ref):
    hn = _rms(x, n1g).astype(BF16)
    return jnp.dot(hn, win_ref[...], preferred_element_type=F32)


def _tril(n):
    r = lax.broadcasted_iota(jnp.int32, (n, n), 0)
    c = lax.broadcasted_iota(jnp.int32, (n, n), 1)
    return r >= c


def _cumsum_rows(x):
    n = x.shape[0]
    return jnp.dot(_tril(n).astype(F32), x, precision=lax.Precision.HIGHEST, preferred_element_type=F32)


def _head(h):
    return slice(h * HG_DK, (h + 1) * HG_DK)


def _dot_nt(a, b):
    return lax.dot_general(a, b, (((1,), (1,)), ((), ())), preferred_element_type=F32)


def _dot_tn(a, b):
    return lax.dot_general(a, b, (((0,), (0,)), ((), ())), preferred_element_type=F32)


def _mlp(h, n2g, wup_ref, wdn_ref, fg):
    hn = _rms(h, n2g).astype(BF16)
    u = jnp.dot(hn, wup_ref[...], preferred_element_type=F32)
    a = jnp.square(jnp.maximum(u, 0.0)).astype(BF16)
    h2 = h + jnp.dot(a, wdn_ref[...], preferred_element_type=F32)
    return _rms(h2, fg)


def _dwconv_lane_group(e_ref, cw_ref, bias, base, rows, g):
    sub = 8
    lanes = slice(g * 128, (g + 1) * 128)
    ext = e_ref[base - HIST:base + rows, lanes]
    acc = None
    for r in range(sub):
        u = None
        for a in range(-(-CONV_K // sub)):
            d = sub * a + r
            if d < CONV_K:
                lo = HIST - sub - sub * a
                term = cw_ref[CONV_K - 1 - d:CONV_K - d, lanes] * ext[lo:lo + rows + sub]
                u = term if u is None else u + term
        shifted = u[sub:] if r == 0 else pltpu.roll(u, r, 0)[sub:]
        acc = shifted if acc is None else acc + shifted
    return acc + bias[:, lanes]


def _prompt_layer_kernel(x_ref, meta_ref, lbp_ref, n1g_ref, win_ref, og_ref, cw_ref, cb_ref, lng_ref, lnb_ref,
                         wout_ref, n2g_ref, wup_ref, wdn_ref, fg_ref, y_out_ref, sfin_ref, cfin_ref,
                         z_ref, y_ref, st_ref, e_ref, h_ref, hn_ref, a_ref, acc_ref, *, tile_t, nt):
    s = pl.program_id(0)
    ns = pl.num_programs(0) - 1
    mixing = s < ns
    t = lax.rem(jnp.minimum(s, ns - 1), nt)
    cur = lax.rem(s, 2)
    prev = 1 - cur
    lb = _lower_bound(lbp_ref[...])
    n1g = n1g_ref[...]

    @pl.when(s == 0)
    def _no_previous_tile():
        h_ref[1] = jnp.zeros(h_ref.shape[1:], F32)

    @pl.when(mixing & (t == 0))
    def _meta_tokens():
        zm = _project(meta_ref[...], n1g, win_ref)
        f = lb + (1.0 - lb) * jax.nn.sigmoid(zm[:, C_F:C_I])
        b = _cumsum_rows(jnp.log(f))
        k_end = ((1.0 - f) * jnp.exp(b[N_META - 1:N_META] - b)).astype(BF16)
        v = zm[:, C_I:C_G].astype(BF16)
        for h in range(HG_HEADS):
            st_ref[h] = _dot_tn(v[:, _head(h)], k_end[:, _head(h)])
        e_ref[0:HIST - N_META, :] = jnp.zeros((HIST - N_META, CV_WIDTH), F32)
        e_ref[HIST - N_META:HIST, :] = zm[:, C_A:C_B] * jax.nn.sigmoid(zm[:, C_B:C_END])

    x = x_ref[0]
    z_ref[...] = _project(x, n1g, win_ref)
    h_prev = h_ref[prev]
    hn_ref[...] = _rms(h_prev, n2g_ref[...]).astype(BF16)
    acc_ref[...] = h_prev
    causal = _tril(CHUNK)
    og = og_ref[...]
    cb = cb_ref[...]
    n_chunks = tile_t // CHUNK
    ff = wup_ref.shape[1] // n_chunks
    fft = max(ff // HG_HEADS, MXU_COLS)
    dt = acc_ref.shape[1] // (CV_WIDTH // 128)

    for c in range(n_chunks):
        rows = slice(c * CHUNK, (c + 1) * CHUNK)
        f = lb + (1.0 - lb) * jax.nn.sigmoid(z_ref[rows, C_F:C_I])
        k = 1.0 - f
        b = _cumsum_rows(jnp.log(f))
        b_end = b[CHUNK - 1:CHUNK]
        q_dec = (_silu(z_ref[rows, C_Q:C_F]) * jnp.exp(b)).astype(BF16)
        k_inv = (k * jnp.exp(-b)).astype(BF16)
        k_end = (k * jnp.exp(b_end - b)).astype(BF16)
        decay = jnp.exp(b_end)
        v = z_ref[rows, C_I:C_G].astype(BF16)
        gate = _silu(z_ref[rows, C_G:C_A])
        e_ref[HIST + c * CHUNK:HIST + (c + 1) * CHUNK, :] = z_ref[rows, C_A:C_B] * jax.nn.sigmoid(z_ref[rows, C_B:C_END])
        for h in range(HG_HEADS):
            if (h + 1) * fft <= ff:
                ucols = slice(c * ff + h * fft, c * ff + (h + 1) * fft)
                u = jnp.dot(hn_ref[...], wup_ref[:, ucols], preferred_element_type=F32)
                a_ref[:, h * fft:(h + 1) * fft] = jnp.square(jnp.maximum(u, 0.0)).astype(BF16)
            hs = _head(h)
            scores = jnp.where(causal, _dot_nt(q_dec[:, hs], k_inv[:, hs]), 0.0)
            s_t = st_ref[h]
            o = (jnp.dot(scores.astype(BF16), v[:, hs], preferred_element_type=F32)
                 + _dot_nt(q_dec[:, hs], s_t.astype(BF16)))
            st_ref[h] = s_t * decay[:, hs] + _dot_tn(v[:, hs], k_end[:, hs])
            y_ref[rows, hs] = (_rms(o, og) * gate[:, hs]).astype(BF16)
        dw = []
        for g in range(CV_WIDTH // 128):
            dw.append(_dwconv_lane_group(e_ref, cw_ref, cb, HIST + c * CHUNK, CHUNK, g))
            dcols = slice(g * dt, (g + 1) * dt)
            acc_ref[:, dcols] += jnp.dot(a_ref[...], wdn_ref[c * ff:(c + 1) * ff, dcols], preferred_element_type=F32)
        dw = jnp.concatenate(dw, axis=-1)
        y_ref[rows, HG_WIDTH:] = _silu(_layernorm(dw, lng_ref[...], lnb_ref[...])).astype(BF16)

    y_out_ref[0] = _rms(acc_ref[...], fg_ref[...])
    h_ref[cur] = x + jnp.dot(y_ref[...], wout_ref[...], preferred_element_type=F32)
    e_ref[0:HIST, :] = e_ref[tile_t:tile_t + HIST, :]

    @pl.when(mixing & (t == nt - 1))
    def _final_state():
        for h in range(HG_HEADS):
            sfin_ref[0, h] = st_ref[h].T
        cfin_ref[0] = e_ref[tile_t + HIST - (CONV_K - 1):tile_t + HIST, :]


def _const_spec(shape, single=False):
    zeros = (0,) * len(shape)
    if single:
        return pl.BlockSpec(shape, lambda *_: zeros, pipeline_mode=pl.Buffered(1))
    return pl.BlockSpec(shape, lambda *_: zeros)


def _prompt_layer(x, meta, lbp, n1g, win, og, cw, cb, lng, lnb, wout, n2g, wup, wdn, fg):
    bsz, seq, d = x.shape
    nt = seq // TILE_T
    ns = bsz * nt

    def mix_tile(s):
        s1 = jnp.minimum(s, ns - 1)
        return s1 // nt, s1 % nt

    def mlp_tile(s):
        s2 = jnp.maximum(s - 1, 0)
        return s2 // nt, s2 % nt

    return pl.pallas_call(
        functools.partial(_prompt_layer_kernel, tile_t=TILE_T, nt=nt),
        grid=(ns + 1,),
        in_specs=[
            pl.BlockSpec((1, TILE_T, d), lambda s: (*mix_tile(s), 0)),
            _const_spec(meta.shape), _const_spec(lbp.shape), _const_spec(n1g.shape),
            _const_spec(win.shape, single=True), _const_spec(og.shape), _const_spec(cw.shape),
            _const_spec(cb.shape), _const_spec(lng.shape), _const_spec(lnb.shape),
            _const_spec(wout.shape, single=True), _const_spec(n2g.shape),
            _const_spec(wup.shape, single=True), _const_spec(wdn.shape, single=True), _const_spec(fg.shape),
        ],
        out_specs=[
            pl.BlockSpec((1, TILE_T, d), lambda s: (*mlp_tile(s), 0)),
            pl.BlockSpec((1, HG_HEADS, HG_DK, HG_DV), lambda s: (mix_tile(s)[0], 0, 0, 0)),
            pl.BlockSpec((1, CONV_K - 1, CV_WIDTH), lambda s: (mix_tile(s)[0], 0, 0)),
        ],
        out_shape=[
            jax.ShapeDtypeStruct((bsz, seq, d), F32),
            jax.ShapeDtypeStruct((bsz, HG_HEADS, HG_DK, HG_DV), F32),
            jax.ShapeDtypeStruct((bsz, CONV_K - 1, CV_WIDTH), F32),
        ],
        scratch_shapes=[
            pltpu.VMEM((TILE_T, C_END), F32),
            pltpu.VMEM((TILE_T, HG_WIDTH + CV_WIDTH), BF16),
            pltpu.VMEM((HG_HEADS, HG_DV, HG_DK), F32),
            pltpu.VMEM((HIST + TILE_T, CV_WIDTH), F32),
            pltpu.VMEM((2, TILE_T, d), F32),
            pltpu.VMEM((TILE_T, d), BF16),
            pltpu.VMEM((TILE_T, wup.shape[1] // (TILE_T // CHUNK)), BF16),
            pltpu.VMEM((TILE_T, d), F32),
        ],
        compiler_params=pltpu.CompilerParams(dimension_semantics=("arbitrary",), vmem_limit_bytes=VMEM_LIMIT),
        name="prompt_layer",
    )(x, meta, lbp, n1g, win, og, cw, cb, lng, lnb, wout, n2g, wup, wdn, fg)


def _sample_state_kernel(x_ref, lbp_ref, n1g_ref, win_ref, cw_ref, cb_ref, st_ref, sc_ref,
                         nst_ref, nsc_ref, o_ref, dw_ref, gate_ref, ft_ref, qvg_ref, *, block):
    i = pl.program_id(0)

    @pl.when(i == 0)
    def _gates():
        lb = _lower_bound(lbp_ref[...])
        z = _project(x_ref[...], n1g_ref[...], win_ref)
        f = lb + (1.0 - lb) * jax.nn.sigmoid(z[:, C_F:C_I])
        for h in range(HG_HEADS):
            ft_ref[h] = f[:, _head(h)].T
        qvg_ref[:, 0:HG_WIDTH] = _silu(z[:, C_Q:C_F])
        qvg_ref[:, HG_WIDTH:2 * HG_WIDTH] = z[:, C_I:C_G]
        qvg_ref[:, 2 * HG_WIDTH:] = z[:, C_A:C_B] * jax.nn.sigmoid(z[:, C_B:C_END])
        gate_ref[...] = _silu(z[:, C_G:C_A])

    nseq = ft_ref.shape[2]
    shift = lax.rem(nseq - i * block, nseq)
    qvg = qvg_ref[pl.ds(pl.multiple_of(i * block, block), block), :]
    q = qvg[:, 0:HG_WIDTH].astype(BF16)
    v = qvg[:, HG_WIDTH:2 * HG_WIDTH]
    glu = qvg[:, 2 * HG_WIDTH:]
    seq_row = lax.broadcasted_iota(jnp.int32, (block, HG_DV), 0)
    for h in range(HG_HEADS):
        hs = _head(h)
        f_t = pltpu.roll(ft_ref[h], shift, 1)
        o = jnp.zeros((block, HG_DV), F32)
        for j in range(block):
            f_b = jnp.broadcast_to(f_t[:, j:j + 1], (HG_DK, HG_DV))
            s_new = f_b * st_ref[j, h] + (1.0 - f_b) * v[j:j + 1, hs]
            nst_ref[j, h] = s_new
            o_all = jnp.dot(q[:, hs], s_new.astype(BF16), preferred_element_type=F32)
            o = jnp.where(seq_row == j, o_all, o)
        o_ref[:, hs] = o
    taps = CONV_K - 1
    dw = cb_ref[...] + cw_ref[taps:taps + 1, :] * glu
    for j in range(taps):
        dw = dw + cw_ref[j:j + 1, :] * sc_ref[j]
    dw_ref[...] = dw
    nsc_ref[0:taps - 1] = sc_ref[1:taps]
    nsc_ref[taps - 1] = glu


def _sample_state(x, lbp, n1g, win, cw, cb, st, sc):
    nseq, d = x.shape
    blk = SAMPLE_BLOCK
    taps = sc.shape[0]
    return pl.pallas_call(
        functools.partial(_sample_state_kernel, block=blk),
        grid=(nseq // blk,),
        in_specs=[
            _const_spec(x.shape), _const_spec(lbp.shape), _const_spec(n1g.shape),
            _const_spec(win.shape, single=True), _const_spec(cw.shape), _const_spec(cb.shape),
            pl.BlockSpec((blk, HG_HEADS, HG_DK, HG_DV), lambda i: (i, 0, 0, 0)),
            pl.BlockSpec((taps, blk, CV_WIDTH), lambda i: (0, i, 0)),
        ],
        out_specs=[
            pl.BlockSpec((blk, HG_HEADS, HG_DK, HG_DV), lambda i: (i, 0, 0, 0)),
            pl.BlockSpec((taps, blk, CV_WIDTH), lambda i: (0, i, 0)),
            pl.BlockSpec((blk, HG_WIDTH), lambda i: (i, 0)),
            pl.BlockSpec((blk, CV_WIDTH), lambda i: (i, 0)),
            _const_spec((nseq, HG_WIDTH)),
        ],
        out_shape=[
            jax.ShapeDtypeStruct(st.shape, F32),
            jax.ShapeDtypeStruct(sc.shape, F32),
            jax.ShapeDtypeStruct((nseq, HG_WIDTH), F32),
            jax.ShapeDtypeStruct((nseq, CV_WIDTH), F32),
            jax.ShapeDtypeStruct((nseq, HG_WIDTH), F32),
        ],
        scratch_shapes=[
            pltpu.VMEM((HG_HEADS, HG_DK, nseq), F32),
            pltpu.VMEM((nseq, 2 * HG_WIDTH + CV_WIDTH), F32),
        ],
        compiler_params=pltpu.CompilerParams(dimension_semantics=("arbitrary",), vmem_limit_bytes=VMEM_LIMIT),
        name="sample_state",
    )(x, lbp, n1g, win, cw, cb, st, sc)


def _sample_tail_kernel(x_ref, o_ref, dw_ref, gate_ref, og_ref, lng_ref, lnb_ref, wout_ref, n2g_ref, wup_ref,
                        wdn_ref, fg_ref, y_ref):
    og = og_ref[...]
    y_hg = [(_rms(o_ref[:, _head(h)], og) * gate_ref[:, _head(h)]).astype(BF16) for h in range(HG_HEADS)]
    y_cv = _silu(_layernorm(dw_ref[...], lng_ref[...], lnb_ref[...])).astype(BF16)
    ycat = jnp.concatenate(y_hg + [y_cv], axis=-1)
    h = x_ref[...] + jnp.dot(ycat, wout_ref[...], preferred_element_type=F32)
    y_ref[...] = _mlp(h, n2g_ref[...], wup_ref, wdn_ref, fg_ref[...])


def _sample_tail(x, o, dw, gate, og, lng, lnb, wout, n2g, wup, wdn, fg):
    return pl.pallas_call(
        _sample_tail_kernel,
        out_shape=jax.ShapeDtypeStruct(x.shape, F32),
        compiler_params=pltpu.CompilerParams(vmem_limit_bytes=VMEM_LIMIT),
        name="sample_tail",
    )(x, o, dw, gate, og, lng, lnb, wout, n2g, wup, wdn, fg)


def kernel(x_prompt, x_sample, state_hgrn, state_conv, meta_tokens, hg_lb, norm1_g, w_in, hg_onorm_g, conv_w, conv_b,
           conv_ln_g, conv_ln_b, w_out, norm2_g, w_up, w_down, final_g):
    assert state_hgrn.shape[0] == 1, "single-layer stack"
    bsz, seq, d = x_prompt.shape
    row = lambda a: a.reshape(1, -1)
    n1g, og, cb, lng, lnb, n2g, fg = (row(norm1_g[0]), row(hg_onorm_g[0]), row(conv_b[0]), row(conv_ln_g[0]),
                                      row(conv_ln_b[0]), row(norm2_g[0]), row(final_g))
    win, wout, wup, wdn = (w_in[0].astype(BF16), w_out[0].astype(BF16), w_up[0].astype(BF16),
                           w_down[0].astype(BF16))
    cw = conv_w[0]

    y_p, s_p, c_p = _prompt_layer(x_prompt, meta_tokens, hg_lb, n1g, win, og, cw, cb, lng, lnb, wout, n2g, wup, wdn,
                                  fg)

    x_s = x_sample.reshape(x_sample.shape[0], d)
    s_s, c_s, o_s, dw_s, gate_s = _sample_state(x_s, hg_lb, n1g, win, cw, cb, state_hgrn[0],
                                                jnp.swapaxes(state_conv[0], 0, 1))
    y_s = _sample_tail(x_s, o_s, dw_s, gate_s, og, lng, lnb, wout, n2g, wup, wdn, fg).reshape(x_sample.shape)

    return (y_p, y_s, s_p[None], c_p[None], s_s[None], jnp.swapaxes(c_s, 0, 1)[None])
```

```python
import functools

import jax
import jax.numpy as jnp
from jax import lax
from jax.experimental import pallas as pl
from jax.experimental.pallas import tpu as pltpu

F32 = jnp.float32
BF16 = jnp.bfloat16

N_META = 16
HG_HEADS = 4
HG_DK = 128
HG_DV = 128
HG_WIDTH = HG_HEADS * HG_DK
CV_WIDTH = 512
CONV_K = 31
CHUNK = 64
EPS = 1e-6
C_Q, C_F, C_I, C_G, C_A, C_B, C_END = (0, HG_WIDTH, 2 * HG_WIDTH, 3 * HG_WIDTH, 4 * HG_WIDTH,
                                       4 * HG_WIDTH + CV_WIDTH, 4 * HG_WIDTH + 2 * CV_WIDTH)
HIST = 32
MXU_COLS = 256
TILE_T = 512
SAMPLE_SEQS = 8
SAMPLE_HEADS = 2
VMEM_LIMIT = 57 * 1024 * 1024


def _rms(x, g):
    return x * lax.rsqrt(jnp.mean(x * x, axis=-1, keepdims=True) + EPS) * g


def _layernorm(x, g, b):
    xc = x - jnp.mean(x, axis=-1, keepdims=True)
    return xc * lax.rsqrt(jnp.mean(xc * xc, axis=-1, keepdims=True) + EPS) * g + b


def _silu(x):
    return x * jax.nn.sigmoid(x)


def _lower_bound(lbp):
    e = jnp.exp(lbp - jnp.max(lbp, axis=0, keepdims=True))
    return e[0:1] / jnp.sum(e, axis=0, keepdims=True)


def _project(x, n1g, win_ref):
    hn = _rms(x, n1g).astype(BF16)
    return jnp.dot(hn, win_ref[...], preferred_element_type=F32)


def _tril(n):
    r = lax.broadcasted_iota(jnp.int32, (n, n), 0)
    c = lax.broadcasted_iota(jnp.int32, (n, n), 1)
    return r >= c


def _cumsum_rows(x):
    n = x.shape[0]
    return jnp.dot(_tril(n).astype(F32), x, precision=lax.Precision.HIGHEST, preferred_element_type=F32)


def _head(h):
    return slice(h * HG_DK, (h + 1) * HG_DK)


def _dot_nt(a, b):
    return lax.dot_general(a, b, (((1,), (1,)), ((), ())), preferred_element_type=F32)


def _dot_tn(a, b):
    return lax.dot_general(a, b, (((0,), (0,)), ((), ())), preferred_element_type=F32)


def _mlp(h, n2g, wup_ref, wdn_ref, fg):
    hn = _rms(h, n2g).astype(BF16)
    u = jnp.dot(hn, wup_ref[...], preferred_element_type=F32)
    a = jnp.square(jnp.maximum(u, 0.0)).astype(BF16)
    h2 = h + jnp.dot(a, wdn_ref[...], preferred_element_type=F32)
    return _rms(h2, fg)


def _dwconv_lane_group(e_ref, cw_ref, bias, base, rows, g):
    sub = 8
    lanes = slice(g * 128, (g + 1) * 128)
    ext = e_ref[base - HIST:base + rows, lanes]
    acc = None
    for r in range(sub):
        u = None
        for a in range(-(-CONV_K // sub)):
            d = sub * a + r
            if d < CONV_K:
                lo = HIST - sub - sub * a
                term = cw_ref[CONV_K - 1 - d:CONV_K - d, lanes] * ext[lo:lo + rows + sub]
                u = term if u is None else u + term
        shifted = u[sub:] if r == 0 else pltpu.roll(u, r, 0)[sub:]
        acc = shifted if acc is None else acc + shifted
    return acc + bias[:, lanes]


def _layer_kernel(x_ref, meta_ref, lbp_ref, n1g_ref, win_ref, og_ref, cw_ref, cb_ref, lng_ref, lnb_ref,
                  wout_ref, n2g_ref, wup_ref, wdn_ref, fg_ref, sx_ref, sst_ref, ssc_ref,
                  y_out_ref, sfin_ref, cfin_ref, snst_ref, snsc_ref, ys_ref,
                  z_ref, y_ref, st_ref, e_ref, h_ref, hn_ref, a_ref,
                  sft_ref, sqt_ref, sv_ref, sglu_ref, sgate_ref, so_ref, sdw_ref, *, tile_t, nt):
    s = pl.program_id(0)
    ns = pl.num_programs(0) - 1
    mixing = s < ns
    t = lax.rem(jnp.minimum(s, ns - 1), nt)
    lb = _lower_bound(lbp_ref[...])
    n1g = n1g_ref[...]

    @pl.when(s == 0)
    def _first_step():
        h_ref[...] = jnp.zeros(h_ref.shape, F32)
        zs = _project(sx_ref[...], n1g, win_ref)
        fs = lb + (1.0 - lb) * jax.nn.sigmoid(zs[:, C_F:C_I])
        qs = _silu(zs[:, C_Q:C_F])
        for h in range(HG_HEADS):
            sft_ref[h] = fs[:, _head(h)].T
            sqt_ref[h] = qs[:, _head(h)].T
        pair = SAMPLE_HEADS * HG_DV
        for p in range(HG_HEADS // SAMPLE_HEADS):
            sv_ref[p] = zs[:, C_I + p * pair:C_I + (p + 1) * pair]
        sglu_ref[...] = zs[:, C_A:C_B] * jax.nn.sigmoid(zs[:, C_B:C_END])
        sgate_ref[...] = _silu(zs[:, C_G:C_A])

    @pl.when(mixing & (t == 0))
    def _meta_tokens():
        zm = _project(meta_ref[...], n1g, win_ref)
        f = lb + (1.0 - lb) * jax.nn.sigmoid(zm[:, C_F:C_I])
        b = _cumsum_rows(jnp.log(f))
        k_end = ((1.0 - f) * jnp.exp(b[N_META - 1:N_META] - b)).astype(BF16)
        v = zm[:, C_I:C_G].astype(BF16)
        for h in range(HG_HEADS):
            st_ref[h] = _dot_tn(v[:, _head(h)], k_end[:, _head(h)])
        e_ref[0:HIST - N_META, :] = jnp.zeros((HIST - N_META, CV_WIDTH), F32)
        e_ref[HIST - N_META:HIST, :] = zm[:, C_A:C_B] * jax.nn.sigmoid(zm[:, C_B:C_END])

    cb = cb_ref[...]
    _sample_state_update(s, sst_ref, ssc_ref, snst_ref, snsc_ref, cw_ref, cb, sft_ref, sqt_ref, sv_ref, sglu_ref,
                         so_ref, sdw_ref)
    x = x_ref[0]
    z_ref[...] = _project(x, n1g, win_ref)
    h_prev = h_ref[...]
    hn_ref[...] = _rms(h_prev, n2g_ref[...]).astype(BF16)
    y_out_ref[0] = h_prev
    causal = _tril(CHUNK)
    og = og_ref[...]
    n_chunks = tile_t // CHUNK
    ff = wup_ref.shape[1] // n_chunks
    fft = max(ff // HG_HEADS, MXU_COLS)
    dt = y_out_ref.shape[2] // (CV_WIDTH // 128)

    for c in range(n_chunks):
        rows = slice(c * CHUNK, (c + 1) * CHUNK)
        f = lb + (1.0 - lb) * jax.nn.sigmoid(z_ref[rows, C_F:C_I])
        k = 1.0 - f
        b = _cumsum_rows(jnp.log(f))
        b_end = b[CHUNK - 1:CHUNK]
        q_dec = (_silu(z_ref[rows, C_Q:C_F]) * jnp.exp(b)).astype(BF16)
        k_inv = (k * jnp.exp(-b)).astype(BF16)
        k_end = (k * jnp.exp(b_end - b)).astype(BF16)
        decay = jnp.exp(b_end)
        v = z_ref[rows, C_I:C_G].astype(BF16)
        gate = _silu(z_ref[rows, C_G:C_A])
        e_ref[HIST + c * CHUNK:HIST + (c + 1) * CHUNK, :] = z_ref[rows, C_A:C_B] * jax.nn.sigmoid(z_ref[rows, C_B:C_END])
        for h in range(HG_HEADS):
            if (h + 1) * fft <= ff:
                ucols = slice(c * ff + h * fft, c * ff + (h + 1) * fft)
                u = jnp.dot(hn_ref[...], wup_ref[:, ucols], preferred_element_type=F32)
                a_ref[:, h * fft:(h + 1) * fft] = jnp.square(jnp.maximum(u, 0.0)).astype(BF16)
            hs = _head(h)
            scores = jnp.where(causal, _dot_nt(q_dec[:, hs], k_inv[:, hs]), 0.0)
            s_t = st_ref[h]
            o = (jnp.dot(scores.astype(BF16), v[:, hs], preferred_element_type=F32)
                 + _dot_nt(q_dec[:, hs], s_t.astype(BF16)))
            st_ref[h] = s_t * decay[:, hs] + _dot_tn(v[:, hs], k_end[:, hs])
            y_ref[rows, hs] = (_rms(o, og) * gate[:, hs]).astype(BF16)
        dw = []
        for g in range(CV_WIDTH // 128):
            dw.append(_dwconv_lane_group(e_ref, cw_ref, cb, HIST + c * CHUNK, CHUNK, g))
            dcols = slice(g * dt, (g + 1) * dt)
            y_out_ref[0, :, dcols] += jnp.dot(a_ref[...], wdn_ref[c * ff:(c + 1) * ff, dcols],
                                              preferred_element_type=F32)
        dw = jnp.concatenate(dw, axis=-1)
        y_ref[rows, HG_WIDTH:] = _silu(_layernorm(dw, lng_ref[...], lnb_ref[...])).astype(BF16)

    y_out_ref[0] = _rms(y_out_ref[0], fg_ref[...])
    h_ref[...] = x + jnp.dot(y_ref[...], wout_ref[...], preferred_element_type=F32)
    e_ref[0:HIST, :] = e_ref[tile_t:tile_t + HIST, :]

    @pl.when(mixing & (t == nt - 1))
    def _final_state():
        for h in range(HG_HEADS):
            sfin_ref[0, h] = st_ref[h].T
        cfin_ref[0] = e_ref[tile_t + HIST - (CONV_K - 1):tile_t + HIST, :]

    @pl.when(s == ns)
    def _sample_outputs():
        y_hg = []
        for h in range(HG_HEADS):
            p, hh = divmod(h, SAMPLE_HEADS)
            o_h = so_ref[p, :, hh * HG_DV:(hh + 1) * HG_DV]
            y_hg.append((_rms(o_h, og) * sgate_ref[:, _head(h)]).astype(BF16))
        y_cv = _silu(_layernorm(sdw_ref[...], lng_ref[...], lnb_ref[...])).astype(BF16)
        ycat = jnp.concatenate(y_hg + [y_cv], axis=-1)
        hs = sx_ref[...] + jnp.dot(ycat, wout_ref[...], preferred_element_type=F32)
        ys_ref[...] = _mlp(hs, n2g_ref[...], wup_ref, wdn_ref, fg_ref[...])


def _sample_units(s, nseq):
    n_pairs = HG_HEADS // SAMPLE_HEADS
    unit = jnp.minimum(s, nseq // SAMPLE_SEQS * n_pairs - 1)
    return unit // n_pairs, unit % n_pairs, jnp.minimum(s, nseq // SAMPLE_SEQS - 1)


def _sample_state_update(s, sst_ref, ssc_ref, snst_ref, snsc_ref, cw_ref, cb, sft_ref, sqt_ref, sv_ref, sglu_ref,
                         so_ref, sdw_ref):
    nseq = sft_ref.shape[2]
    sb, hp, cblk = _sample_units(s, nseq)
    r0 = pl.multiple_of(sb * SAMPLE_SEQS, SAMPLE_SEQS)
    shift = lax.rem(nseq - r0, nseq)
    v_rows = sv_ref[hp, pl.ds(r0, SAMPLE_SEQS), :]
    seq_row = lax.broadcasted_iota(jnp.int32, (SAMPLE_SEQS, HG_DV), 0)
    for hh in range(SAMPLE_HEADS):
        h = hp * SAMPLE_HEADS + hh
        f_t = pltpu.roll(sft_ref[h], shift, 1)
        q_t = pltpu.roll(sqt_ref[h], shift, 1)
        o = jnp.zeros((SAMPLE_SEQS, HG_DV), F32)
        for j in range(SAMPLE_SEQS):
            f_b = jnp.broadcast_to(f_t[:, j:j + 1], (HG_DK, HG_DV))
            s_new = f_b * sst_ref[j, hh] + (1.0 - f_b) * v_rows[j:j + 1, hh * HG_DV:(hh + 1) * HG_DV]
            snst_ref[j, hh] = s_new
            q_b = jnp.broadcast_to(q_t[:, j:j + 1], (HG_DK, HG_DV))
            o = jnp.where(seq_row == j, jnp.sum(q_b * s_new, axis=0, keepdims=True), o)
        so_ref[hp, pl.ds(r0, SAMPLE_SEQS), hh * HG_DV:(hh + 1) * HG_DV] = o
    c0 = pl.multiple_of(cblk * SAMPLE_SEQS, SAMPLE_SEQS)
    glu = sglu_ref[pl.ds(c0, SAMPLE_SEQS), :]
    taps = CONV_K - 1
    dw = cb + cw_ref[taps:taps + 1, :] * glu
    for j in range(taps):
        dw = dw + cw_ref[j:j + 1, :] * ssc_ref[j]
    sdw_ref[pl.ds(c0, SAMPLE_SEQS), :] = dw
    snsc_ref[0:taps - 1] = ssc_ref[1:taps]
    snsc_ref[taps - 1] = glu


def _const_spec(shape, single=False):
    zeros = (0,) * len(shape)
    if single:
        return pl.BlockSpec(shape, lambda *_: zeros, pipeline_mode=pl.Buffered(1))
    return pl.BlockSpec(shape, lambda *_: zeros)


def _layer(x, meta, lbp, n1g, win, og, cw, cb, lng, lnb, wout, n2g, wup, wdn, fg, xs, sst, ssc):
    bsz, seq, d = x.shape
    nt = seq // TILE_T
    ns = bsz * nt
    nseq = xs.shape[0]
    n_pairs = HG_HEADS // SAMPLE_HEADS
    assert nseq // SAMPLE_SEQS * n_pairs <= ns, "one (sequence block, head pair) per grid step"

    def mix_tile(s):
        s1 = jnp.minimum(s, ns - 1)
        return s1 // nt, s1 % nt

    def mlp_tile(s):
        s2 = jnp.maximum(s - 1, 0)
        return s2 // nt, s2 % nt

    state_spec = pl.BlockSpec((SAMPLE_SEQS, SAMPLE_HEADS, HG_DK, HG_DV),
                              lambda s: (*_sample_units(s, nseq)[:2], 0, 0))
    conv_spec = pl.BlockSpec((CONV_K - 1, SAMPLE_SEQS, CV_WIDTH), lambda s: (0, _sample_units(s, nseq)[2], 0))
    pair = SAMPLE_HEADS * HG_DV

    return pl.pallas_call(
        functools.partial(_layer_kernel, tile_t=TILE_T, nt=nt),
        grid=(ns + 1,),
        in_specs=[
            pl.BlockSpec((1, TILE_T, d), lambda s: (*mix_tile(s), 0)),
            _const_spec(meta.shape), _const_spec(lbp.shape), _const_spec(n1g.shape),
            _const_spec(win.shape, single=True), _const_spec(og.shape), _const_spec(cw.shape),
            _const_spec(cb.shape), _const_spec(lng.shape), _const_spec(lnb.shape),
            _const_spec(wout.shape, single=True), _const_spec(n2g.shape),
            _const_spec(wup.shape, single=True), _const_spec(wdn.shape, single=True), _const_spec(fg.shape),
            _const_spec(xs.shape, single=True), state_spec, conv_spec,
        ],
        out_specs=[
            pl.BlockSpec((1, TILE_T, d), lambda s: (*mlp_tile(s), 0)),
            pl.BlockSpec((1, HG_HEADS, HG_DK, HG_DV), lambda s: (mix_tile(s)[0], 0, 0, 0)),
            pl.BlockSpec((1, CONV_K - 1, CV_WIDTH), lambda s: (mix_tile(s)[0], 0, 0)),
            state_spec, conv_spec, _const_spec(xs.shape),
        ],
        out_shape=[
            jax.ShapeDtypeStruct((bsz, seq, d), F32),
            jax.ShapeDtypeStruct((bsz, HG_HEADS, HG_DK, HG_DV), F32),
            jax.ShapeDtypeStruct((bsz, CONV_K - 1, CV_WIDTH), F32),
            jax.ShapeDtypeStruct(sst.shape, F32),
            jax.ShapeDtypeStruct(ssc.shape, F32),
            jax.ShapeDtypeStruct(xs.shape, F32),
        ],
        scratch_shapes=[
            pltpu.VMEM((TILE_T, C_END), F32),
            pltpu.VMEM((TILE_T, HG_WIDTH + CV_WIDTH), BF16),
            pltpu.VMEM((HG_HEADS, HG_DV, HG_DK), F32),
            pltpu.VMEM((HIST + TILE_T, CV_WIDTH), F32),
            pltpu.VMEM((TILE_T, d), F32),
            pltpu.VMEM((TILE_T, d), BF16),
            pltpu.VMEM((TILE_T, wup.shape[1] // (TILE_T // CHUNK)), BF16),
            pltpu.VMEM((HG_HEADS, HG_DK, nseq), F32),
            pltpu.VMEM((HG_HEADS, HG_DK, nseq), F32),
            pltpu.VMEM((n_pairs, nseq, pair), F32),
            pltpu.VMEM((nseq, CV_WIDTH), F32),
            pltpu.VMEM((nseq, HG_WIDTH), F32),
            pltpu.VMEM((n_pairs, nseq, pair), F32),
            pltpu.VMEM((nseq, CV_WIDTH), F32),
        ],
        compiler_params=pltpu.CompilerParams(dimension_semantics=("arbitrary",), vmem_limit_bytes=VMEM_LIMIT),
        name="layer",
    )(x, meta, lbp, n1g, win, og, cw, cb, lng, lnb, wout, n2g, wup, wdn, fg, xs, sst, ssc)


def kernel(x_prompt, x_sample, state_hgrn, state_conv, meta_tokens, hg_lb, norm1_g, w_in, hg_onorm_g, conv_w, conv_b,
           conv_ln_g, conv_ln_b, w_out, norm2_g, w_up, w_down, final_g):
    assert state_hgrn.shape[0] == 1, "single-layer stack"
    bsz, seq, d = x_prompt.shape
    row = lambda a: a.reshape(1, -1)
    n1g, og, cb, lng, lnb, n2g, fg = (row(norm1_g[0]), row(hg_onorm_g[0]), row(conv_b[0]), row(conv_ln_g[0]),
                                      row(conv_ln_b[0]), row(norm2_g[0]), row(final_g))
    win, wout, wup, wdn = (w_in[0].astype(BF16), w_out[0].astype(BF16), w_up[0].astype(BF16),
                           w_down[0].astype(BF16))
    cw = conv_w[0]

    assert x_sample.shape[1] == 1, "one new token per running sequence"
    x_s = x_sample.reshape(x_sample.shape[0], d)
    y_p, s_p, c_p, s_s, c_s, y_s = _layer(x_prompt, meta_tokens, hg_lb, n1g, win, og, cw, cb, lng, lnb, wout, n2g, wup,
                                          wdn, fg, x_s, state_hgrn[0], jnp.swapaxes(state_conv[0], 0, 1))

    return (y_p, y_s.reshape(x_sample.shape), s_p[None], c_p[None], s_s[None], jnp.swapaxes(c_s, 0, 1)[None])
```

```python
import functools

import jax
import jax.numpy as jnp
from jax import lax
from jax.experimental import pallas as pl
from jax.experimental.pallas import tpu as pltpu

F32 = jnp.float32
BF16 = jnp.bfloat16

N_META = 16
HG_HEADS = 4
HG_DK = 128
HG_DV = 128
HG_WIDTH = HG_HEADS * HG_DK
CV_WIDTH = 512
CONV_K = 31
CHUNK = 64
EPS = 1e-6
C_Q, C_F, C_I, C_G, C_A, C_B, C_END = (0, HG_WIDTH, 2 * HG_WIDTH, 3 * HG_WIDTH, 4 * HG_WIDTH,
                                       4 * HG_WIDTH + CV_WIDTH, 4 * HG_WIDTH + 2 * CV_WIDTH)
HIST = 32
MXU_COLS = 256
CUMSUM_ROWS = 256
TILE_T = 512
SAMPLE_SEQS = 8
SAMPLE_HEADS = 2
VMEM_LIMIT = 58 * 1024 * 1024


def _rms(x, g):
    return x * lax.rsqrt(jnp.mean(x * x, axis=-1, keepdims=True) + EPS) * g


def _layernorm(x, g, b):
    xc = x - jnp.mean(x, axis=-1, keepdims=True)
    return xc * lax.rsqrt(jnp.mean(xc * xc, axis=-1, keepdims=True) + EPS) * g + b


def _silu(x):
    return x * jax.nn.sigmoid(x)


def _lower_bound(lbp):
    e = jnp.exp(lbp - jnp.max(lbp, axis=0, keepdims=True))
    return e[0:1] / jnp.sum(e, axis=0, keepdims=True)


def _project(x, n1g, win_ref):
    hn = _rms(x, n1g).astype(BF16)
    return jnp.dot(hn, win_ref[...], preferred_element_type=F32)


def _tril(n):
    r = lax.broadcasted_iota(jnp.int32, (n, n), 0)
    c = lax.broadcasted_iota(jnp.int32, (n, n), 1)
    return r >= c


def _cumsum_rows(x):
    n = x.shape[0]
    return jnp.dot(_tril(n).astype(F32), x, precision=lax.Precision.HIGHEST, preferred_element_type=F32)


def _chunk_cumsum(x):
    n = x.shape[0]
    r = lax.broadcasted_iota(jnp.int32, (n, n), 0)
    c = lax.broadcasted_iota(jnp.int32, (n, n), 1)
    tri = ((r >= c) & (r // CHUNK == c // CHUNK)).astype(BF16)
    hi = x.astype(BF16)
    rest = x - hi.astype(F32)
    mid = rest.astype(BF16)
    lo = (rest - mid.astype(F32)).astype(BF16)
    return (jnp.dot(tri, hi, preferred_element_type=F32) + jnp.dot(tri, mid, preferred_element_type=F32)
            + jnp.dot(tri, lo, preferred_element_type=F32))


def _head(h):
    return slice(h * HG_DK, (h + 1) * HG_DK)


def _dot_nt(a, b):
    return lax.dot_general(a, b, (((1,), (1,)), ((), ())), preferred_element_type=F32)


def _dot_tn(a, b):
    return lax.dot_general(a, b, (((0,), (0,)), ((), ())), preferred_element_type=F32)


def _mlp(h, n2g, wup_ref, wdn_ref, fg):
    hn = _rms(h, n2g).astype(BF16)
    u = jnp.dot(hn, wup_ref[...], preferred_element_type=F32)
    a = jnp.square(jnp.maximum(u, 0.0)).astype(BF16)
    h2 = h + jnp.dot(a, wdn_ref[...], preferred_element_type=F32)
    return _rms(h2, fg)


def _dwconv_lane_group(e_ref, cw_ref, bias, base, rows, g):
    sub = 8
    lanes = slice(g * 128, (g + 1) * 128)
    ext = e_ref[base - HIST:base + rows, lanes]
    acc = None
    for r in range(sub):
        u = None
        for a in range(-(-CONV_K // sub)):
            d = sub * a + r
            if d < CONV_K:
                lo = HIST - sub - sub * a
                term = cw_ref[CONV_K - 1 - d:CONV_K - d, lanes] * ext[lo:lo + rows + sub]
                u = term if u is None else u + term
        shifted = u[sub:] if r == 0 else pltpu.roll(u, r, 0)[sub:]
        acc = shifted if acc is None else acc + shifted
    return acc + bias[:, lanes]


def _layer_kernel(x_ref, meta_ref, lbp_ref, n1g_ref, win_ref, og_ref, cw_ref, cb_ref, lng_ref, lnb_ref,
                  wout_ref, n2g_ref, wup_ref, wdn_ref, fg_ref, sx_ref, sst_ref, ssc_ref,
                  y_out_ref, sfin_ref, cfin_ref, snst_ref, snsc_ref, ys_ref,
                  z_ref, b_ref, y_ref, st_ref, e_ref, h_ref, hn_ref, a_ref,
                  sft_ref, sqt_ref, sv_ref, sglu_ref, sgate_ref, so_ref, sdw_ref, *, tile_t, nt):
    s = pl.program_id(0)
    ns = pl.num_programs(0) - 1
    mixing = s < ns
    t = lax.rem(jnp.minimum(s, ns - 1), nt)
    lb = _lower_bound(lbp_ref[...])
    n1g = n1g_ref[...]

    @pl.when(s == 0)
    def _first_step():
        h_ref[...] = jnp.zeros(h_ref.shape, F32)
        zs = _project(sx_ref[...], n1g, win_ref)
        fs = lb + (1.0 - lb) * jax.nn.sigmoid(zs[:, C_F:C_I])
        qs = _silu(zs[:, C_Q:C_F])
        for h in range(HG_HEADS):
            sft_ref[h] = fs[:, _head(h)].T
            sqt_ref[h] = qs[:, _head(h)].T
        pair = SAMPLE_HEADS * HG_DV
        for p in range(HG_HEADS // SAMPLE_HEADS):
            sv_ref[p] = zs[:, C_I + p * pair:C_I + (p + 1) * pair]
        sglu_ref[...] = zs[:, C_A:C_B] * jax.nn.sigmoid(zs[:, C_B:C_END])
        sgate_ref[...] = _silu(zs[:, C_G:C_A])

    @pl.when(mixing & (t == 0))
    def _meta_tokens():
        zm = _project(meta_ref[...], n1g, win_ref)
        f = lb + (1.0 - lb) * jax.nn.sigmoid(zm[:, C_F:C_I])
        b = _cumsum_rows(jnp.log(f))
        k_end = ((1.0 - f) * jnp.exp(b[N_META - 1:N_META] - b)).astype(BF16)
        v = zm[:, C_I:C_G].astype(BF16)
        for h in range(HG_HEADS):
            st_ref[h] = _dot_tn(v[:, _head(h)], k_end[:, _head(h)])
        e_ref[0:HIST - N_META, :] = jnp.zeros((HIST - N_META, CV_WIDTH), F32)
        e_ref[HIST - N_META:HIST, :] = zm[:, C_A:C_B] * jax.nn.sigmoid(zm[:, C_B:C_END])

    cb = cb_ref[...]
    x = x_ref[0]
    z_ref[...] = _project(x, n1g, win_ref)
    _sample_state_update(s, sst_ref, ssc_ref, snst_ref, snsc_ref, cw_ref, cb, sft_ref, sqt_ref, sv_ref, sglu_ref,
                         so_ref, sdw_ref)
    h_prev = h_ref[...]
    hn_ref[...] = _rms(h_prev, n2g_ref[...]).astype(BF16)
    y_out_ref[0] = h_prev
    for r0 in range(0, tile_t, CUMSUM_ROWS):
        rows = slice(r0, r0 + CUMSUM_ROWS)
        f = lb + (1.0 - lb) * jax.nn.sigmoid(z_ref[rows, C_F:C_I])
        z_ref[rows, C_F:C_I] = f
        b_ref[rows, :] = _chunk_cumsum(jnp.log(f))
    causal = _tril(CHUNK)
    og = og_ref[...]
    n_chunks = tile_t // CHUNK
    ff = wup_ref.shape[1] // n_chunks
    fft = max(ff // HG_HEADS, MXU_COLS)
    dt = y_out_ref.shape[2] // (CV_WIDTH // 128)

    for c in range(n_chunks):
        rows = slice(c * CHUNK, (c + 1) * CHUNK)
        f = z_ref[rows, C_F:C_I]
        k = 1.0 - f
        b = b_ref[rows, :]
        b_end = b[CHUNK - 1:CHUNK]
        q_dec = (_silu(z_ref[rows, C_Q:C_F]) * jnp.exp(b)).astype(BF16)
        k_inv = (k * jnp.exp(-b)).astype(BF16)
        k_end = (k * jnp.exp(b_end - b)).astype(BF16)
        decay = jnp.exp(b_end)
        v = z_ref[rows, C_I:C_G].astype(BF16)
        gate = _silu(z_ref[rows, C_G:C_A])
        e_ref[HIST + c * CHUNK:HIST + (c + 1) * CHUNK, :] = z_ref[rows, C_A:C_B] * jax.nn.sigmoid(z_ref[rows, C_B:C_END])
        for h in range(HG_HEADS):
            if (h + 1) * fft <= ff:
                ucols = slice(c * ff + h * fft, c * ff + (h + 1) * fft)
                u = jnp.dot(hn_ref[...], wup_ref[:, ucols], preferred_element_type=F32)
                a_ref[:, h * fft:(h + 1) * fft] = jnp.square(jnp.maximum(u, 0.0)).astype(BF16)
            hs = _head(h)
            scores = jnp.where(causal, _dot_nt(q_dec[:, hs], k_inv[:, hs]), 0.0)
            s_t = st_ref[h]
            o = (jnp.dot(scores.astype(BF16), v[:, hs], preferred_element_type=F32)
                 + _dot_nt(q_dec[:, hs], s_t.astype(BF16)))
            st_ref[h] = s_t * decay[:, hs] + _dot_tn(v[:, hs], k_end[:, hs])
            y_ref[rows, hs] = (_rms(o, og) * gate[:, hs]).astype(BF16)
        dw = []
        for g in range(CV_WIDTH // 128):
            dw.append(_dwconv_lane_group(e_ref, cw_ref, cb, HIST + c * CHUNK, CHUNK, g))
            dcols = slice(g * dt, (g + 1) * dt)
            y_out_ref[0, :, dcols] += jnp.dot(a_ref[...], wdn_ref[c * ff:(c + 1) * ff, dcols],
                                              preferred_element_type=F32)
        dw = jnp.concatenate(dw, axis=-1)
        y_ref[rows, HG_WIDTH:] = _silu(_layernorm(dw, lng_ref[...], lnb_ref[...])).astype(BF16)

    y_out_ref[0] = _rms(y_out_ref[0], fg_ref[...])
    h_ref[...] = x + jnp.dot(y_ref[...], wout_ref[...], preferred_element_type=F32)
    e_ref[0:HIST, :] = e_ref[tile_t:tile_t + HIST, :]

    @pl.when(mixing & (t == nt - 1))
    def _final_state():
        for h in range(HG_HEADS):
            sfin_ref[0, h] = st_ref[h].T
        cfin_ref[0] = e_ref[tile_t + HIST - (CONV_K - 1):tile_t + HIST, :]

    @pl.when(s == ns)
    def _sample_outputs():
        y_hg = []
        for h in range(HG_HEADS):
            p, hh = divmod(h, SAMPLE_HEADS)
            o_h = so_ref[p, :, hh * HG_DV:(hh + 1) * HG_DV]
            y_hg.append((_rms(o_h, og) * sgate_ref[:, _head(h)]).astype(BF16))
        y_cv = _silu(_layernorm(sdw_ref[...], lng_ref[...], lnb_ref[...])).astype(BF16)
        ycat = jnp.concatenate(y_hg + [y_cv], axis=-1)
        hs = sx_ref[...] + jnp.dot(ycat, wout_ref[...], preferred_element_type=F32)
        ys_ref[...] = _mlp(hs, n2g_ref[...], wup_ref, wdn_ref, fg_ref[...])


def _sample_units(s, nseq):
    n_pairs = HG_HEADS // SAMPLE_HEADS
    unit = jnp.minimum(s, nseq // SAMPLE_SEQS * n_pairs - 1)
    return unit // n_pairs, unit % n_pairs, jnp.minimum(s, nseq // SAMPLE_SEQS - 1)


def _sample_state_update(s, sst_ref, ssc_ref, snst_ref, snsc_ref, cw_ref, cb, sft_ref, sqt_ref, sv_ref, sglu_ref,
                         so_ref, sdw_ref):
    nseq = sft_ref.shape[2]
    sb, hp, cblk = _sample_units(s, nseq)
    r0 = pl.multiple_of(sb * SAMPLE_SEQS, SAMPLE_SEQS)
    shift = lax.rem(nseq - r0, nseq)
    v_rows = sv_ref[hp, pl.ds(r0, SAMPLE_SEQS), :]
    seq_row = lax.broadcasted_iota(jnp.int32, (SAMPLE_SEQS, HG_DV), 0)
    for hh in range(SAMPLE_HEADS):
        h = hp * SAMPLE_HEADS + hh
        f_t = pltpu.roll(sft_ref[h], shift, 1)
        q_t = pltpu.roll(sqt_ref[h], shift, 1)
        o = jnp.zeros((SAMPLE_SEQS, HG_DV), F32)
        for j in range(SAMPLE_SEQS):
            f_b = jnp.broadcast_to(f_t[:, j:j + 1], (HG_DK, HG_DV))
            s_new = f_b * sst_ref[j, hh] + (1.0 - f_b) * v_rows[j:j + 1, hh * HG_DV:(hh + 1) * HG_DV]
            snst_ref[j, hh] = s_new
            q_b = jnp.broadcast_to(q_t[:, j:j + 1], (HG_DK, HG_DV))
            o = jnp.where(seq_row == j, jnp.sum(q_b * s_new, axis=0, keepdims=True), o)
        so_ref[hp, pl.ds(r0, SAMPLE_SEQS), hh * HG_DV:(hh + 1) * HG_DV] = o
    c0 = pl.multiple_of(cblk * SAMPLE_SEQS, SAMPLE_SEQS)
    glu = sglu_ref[pl.ds(c0, SAMPLE_SEQS), :]
    taps = CONV_K - 1
    dw = cb + cw_ref[taps:taps + 1, :] * glu
    for j in range(taps):
        dw = dw + cw_ref[j:j + 1, :] * ssc_ref[j]
    sdw_ref[pl.ds(c0, SAMPLE_SEQS), :] = dw
    snsc_ref[0:taps - 1] = ssc_ref[1:taps]
    snsc_ref[taps - 1] = glu


def _const_spec(shape, single=False):
    zeros = (0,) * len(shape)
    if single:
        return pl.BlockSpec(shape, lambda *_: zeros, pipeline_mode=pl.Buffered(1))
    return pl.BlockSpec(shape, lambda *_: zeros)


def _layer(x, meta, lbp, n1g, win, og, cw, cb, lng, lnb, wout, n2g, wup, wdn, fg, xs, sst, ssc):
    bsz, seq, d = x.shape
    nt = seq // TILE_T
    ns = bsz * nt
    nseq = xs.shape[0]
    n_pairs = HG_HEADS // SAMPLE_HEADS
    assert nseq // SAMPLE_SEQS * n_pairs <= ns, "one (sequence block, head pair) per grid step"

    def mix_tile(s):
        s1 = jnp.minimum(s, ns - 1)
        return s1 // nt, s1 % nt

    def mlp_tile(s):
        s2 = jnp.maximum(s - 1, 0)
        return s2 // nt, s2 % nt

    state_spec = pl.BlockSpec((SAMPLE_SEQS, SAMPLE_HEADS, HG_DK, HG_DV),
                              lambda s: (*_sample_units(s, nseq)[:2], 0, 0))
    conv_spec = pl.BlockSpec((CONV_K - 1, SAMPLE_SEQS, CV_WIDTH), lambda s: (0, _sample_units(s, nseq)[2], 0))
    pair = SAMPLE_HEADS * HG_DV

    return pl.pallas_call(
        functools.partial(_layer_kernel, tile_t=TILE_T, nt=nt),
        grid=(ns + 1,),
        in_specs=[
            pl.BlockSpec((1, TILE_T, d), lambda s: (*mix_tile(s), 0)),
            _const_spec(meta.shape), _const_spec(lbp.shape), _const_spec(n1g.shape),
            _const_spec(win.shape, single=True), _const_spec(og.shape), _const_spec(cw.shape),
            _const_spec(cb.shape), _const_spec(lng.shape), _const_spec(lnb.shape),
            _const_spec(wout.shape, single=True), _const_spec(n2g.shape),
            _const_spec(wup.shape, single=True), _const_spec(wdn.shape, single=True), _const_spec(fg.shape),
            _const_spec(xs.shape, single=True), state_spec, conv_spec,
        ],
        out_specs=[
            pl.BlockSpec((1, TILE_T, d), lambda s: (*mlp_tile(s), 0)),
            pl.BlockSpec((1, HG_HEADS, HG_DK, HG_DV), lambda s: (mix_tile(s)[0], 0, 0, 0)),
            pl.BlockSpec((1, CONV_K - 1, CV_WIDTH), lambda s: (mix_tile(s)[0], 0, 0)),
            state_spec, conv_spec, _const_spec(xs.shape),
        ],
        out_shape=[
            jax.ShapeDtypeStruct((bsz, seq, d), F32),
            jax.ShapeDtypeStruct((bsz, HG_HEADS, HG_DK, HG_DV), F32),
            jax.ShapeDtypeStruct((bsz, CONV_K - 1, CV_WIDTH), F32),
            jax.ShapeDtypeStruct(sst.shape, F32),
            jax.ShapeDtypeStruct(ssc.shape, F32),
            jax.ShapeDtypeStruct(xs.shape, F32),
        ],
        scratch_shapes=[
            pltpu.VMEM((TILE_T, C_END), F32),
            pltpu.VMEM((TILE_T, HG_WIDTH), F32),
            pltpu.VMEM((TILE_T, HG_WIDTH + CV_WIDTH), BF16),
            pltpu.VMEM((HG_HEADS, HG_DV, HG_DK), F32),
            pltpu.VMEM((HIST + TILE_T, CV_WIDTH), F32),
            pltpu.VMEM((TILE_T, d), F32),
            pltpu.VMEM((TILE_T, d), BF16),
            pltpu.VMEM((TILE_T, wup.shape[1] // (TILE_T // CHUNK)), BF16),
            pltpu.VMEM((HG_HEADS, HG_DK, nseq), F32),
            pltpu.VMEM((HG_HEADS, HG_DK, nseq), F32),
            pltpu.VMEM((n_pairs, nseq, pair), F32),
            pltpu.VMEM((nseq, CV_WIDTH), F32),
            pltpu.VMEM((nseq, HG_WIDTH), F32),
            pltpu.VMEM((n_pairs, nseq, pair), F32),
            pltpu.VMEM((nseq, CV_WIDTH), F32),
        ],
        compiler_params=pltpu.CompilerParams(dimension_semantics=("arbitrary",), vmem_limit_bytes=VMEM_LIMIT),
        name="layer",
    )(x, meta, lbp, n1g, win, og, cw, cb, lng, lnb, wout, n2g, wup, wdn, fg, xs, sst, ssc)


def kernel(x_prompt, x_sample, state_hgrn, state_conv, meta_tokens, hg_lb, norm1_g, w_in, hg_onorm_g, conv_w, conv_b,
           conv_ln_g, conv_ln_b, w_out, norm2_g, w_up, w_down, final_g):
    assert state_hgrn.shape[0] == 1, "single-layer stack"
    bsz, seq, d = x_prompt.shape
    row = lambda a: a.reshape(1, -1)
    n1g, og, cb, lng, lnb, n2g, fg = (row(norm1_g[0]), row(hg_onorm_g[0]), row(conv_b[0]), row(conv_ln_g[0]),
                                      row(conv_ln_b[0]), row(norm2_g[0]), row(final_g))
    win, wout, wup, wdn = (w_in[0].astype(BF16), w_out[0].astype(BF16), w_up[0].astype(BF16),
                           w_down[0].astype(BF16))
    cw = conv_w[0]

    assert x_sample.shape[1] == 1, "one new token per running sequence"
    x_s = x_sample.reshape(x_sample.shape[0], d)
    y_p, s_p, c_p, s_s, c_s, y_s = _layer(x_prompt, meta_tokens, hg_lb, n1g, win, og, cw, cb, lng, lnb, wout, n2g, wup,
                                          wdn, fg, x_s, state_hgrn[0], jnp.swapaxes(state_conv[0], 0, 1))

    return (y_p, y_s.reshape(x_sample.shape), s_p[None], c_p[None], s_s[None], jnp.swapaxes(c_s, 0, 1)[None])
```

```python
import functools

import jax
import jax.numpy as jnp
from jax import lax
from jax.experimental import pallas as pl
from jax.experimental.pallas import tpu as pltpu

F32 = jnp.float32
BF16 = jnp.bfloat16

N_META = 16
HG_HEADS = 4
HG_DK = 128
HG_DV = 128
HG_WIDTH = HG_HEADS * HG_DK
CV_WIDTH = 512
CONV_K = 31
CHUNK = 64
EPS = 1e-6
C_Q, C_F, C_I, C_G, C_A, C_B, C_END = (0, HG_WIDTH, 2 * HG_WIDTH, 3 * HG_WIDTH, 4 * HG_WIDTH,
                                       4 * HG_WIDTH + CV_WIDTH, 4 * HG_WIDTH + 2 * CV_WIDTH)
HIST = 32
MXU_COLS = 256
CUMSUM_ROWS = 256
W_ROWS, W_COLS = 256, 1024
TILE_T = 512
SAMPLE_SEQS = 8
SAMPLE_HEADS = 2
VMEM_LIMIT = 60 * 1024 * 1024


def _rms(x, g):
    return x * lax.rsqrt(jnp.mean(x * x, axis=-1, keepdims=True) + EPS) * g


def _layernorm(x, g, b):
    xc = x - jnp.mean(x, axis=-1, keepdims=True)
    return xc * lax.rsqrt(jnp.mean(xc * xc, axis=-1, keepdims=True) + EPS) * g + b


def _silu(x):
    return x * jax.nn.sigmoid(x)


def _lower_bound(lbp):
    e = jnp.exp(lbp - jnp.max(lbp, axis=0, keepdims=True))
    return e[0:1] / jnp.sum(e, axis=0, keepdims=True)


def _project(x, n1g, win_ref):
    hn = _rms(x, n1g).astype(BF16)
    return jnp.dot(hn, win_ref[...], preferred_element_type=F32)


def _tril(n):
    r = lax.broadcasted_iota(jnp.int32, (n, n), 0)
    c = lax.broadcasted_iota(jnp.int32, (n, n), 1)
    return r >= c


def _cumsum_rows(x):
    n = x.shape[0]
    return jnp.dot(_tril(n).astype(F32), x, precision=lax.Precision.HIGHEST, preferred_element_type=F32)


def _chunk_cumsum(x):
    n = x.shape[0]
    r = lax.broadcasted_iota(jnp.int32, (n, n), 0)
    c = lax.broadcasted_iota(jnp.int32, (n, n), 1)
    tri = ((r >= c) & (r // CHUNK == c // CHUNK)).astype(BF16)
    hi = x.astype(BF16)
    rest = x - hi.astype(F32)
    mid = rest.astype(BF16)
    lo = (rest - mid.astype(F32)).astype(BF16)
    return (jnp.dot(tri, hi, preferred_element_type=F32) + jnp.dot(tri, mid, preferred_element_type=F32)
            + jnp.dot(tri, lo, preferred_element_type=F32))


def _head(h):
    return slice(h * HG_DK, (h + 1) * HG_DK)


def _dot_nt(a, b):
    return lax.dot_general(a, b, (((1,), (1,)), ((), ())), preferred_element_type=F32)


def _dot_tn(a, b):
    return lax.dot_general(a, b, (((0,), (0,)), ((), ())), preferred_element_type=F32)


def _mlp(h, n2g, wup_ref, wdn_ref, fg):
    hn = _rms(h, n2g).astype(BF16)
    u = jnp.dot(hn, wup_ref[...], preferred_element_type=F32)
    a = jnp.square(jnp.maximum(u, 0.0)).astype(BF16)
    h2 = h + jnp.dot(a, wdn_ref[...], preferred_element_type=F32)
    return _rms(h2, fg)


def _dwconv_lane_group(e_ref, cw_ref, bias, base, rows, g):
    sub = 8
    lanes = slice(g * 128, (g + 1) * 128)
    ext = e_ref[base - HIST:base + rows, lanes]
    acc = None
    for r in range(sub):
        u = None
        for a in range(-(-CONV_K // sub)):
            d = sub * a + r
            if d < CONV_K:
                lo = HIST - sub - sub * a
                term = cw_ref[CONV_K - 1 - d:CONV_K - d, lanes] * ext[lo:lo + rows + sub]
                u = term if u is None else u + term
        shifted = u[sub:] if r == 0 else pltpu.roll(u, r, 0)[sub:]
        acc = shifted if acc is None else acc + shifted
    return acc + bias[:, lanes]


def _layer_kernel(x_ref, meta_ref, lbp_ref, n1g_ref, win_hbm, og_ref, cw_ref, cb_ref, lng_ref, lnb_ref,
                  wout_hbm, n2g_ref, wup_hbm, wdn_hbm, fg_ref, sx_ref, sst_ref, ssc_ref,
                  y_out_ref, sfin_ref, cfin_ref, snst_ref, snsc_ref, ys_ref,
                  win_ref, wout_ref, wup_ref, wdn_ref, wsem,
                  z_ref, b_ref, y_ref, st_ref, e_ref, h_ref, hn_ref, a_ref,
                  sft_ref, sqt_ref, sv_ref, sglu_ref, sgate_ref, so_ref, sdw_ref, *, tile_t, nt):
    s = pl.program_id(0)
    ns = pl.num_programs(0) - 1
    mixing = s < ns
    t = lax.rem(jnp.minimum(s, ns - 1), nt)
    lb = _lower_bound(lbp_ref[...])
    n1g = n1g_ref[...]

    @pl.when(s == 0)
    def _first_step():
        _load_weights_as_bf16(((win_hbm, win_ref), (wout_hbm, wout_ref), (wup_hbm, wup_ref), (wdn_hbm, wdn_ref)),
                              z_ref, wsem)
        h_ref[...] = jnp.zeros(h_ref.shape, F32)
        zs = _project(sx_ref[...], n1g, win_ref)
        fs = lb + (1.0 - lb) * jax.nn.sigmoid(zs[:, C_F:C_I])
        qs = _silu(zs[:, C_Q:C_F])
        for h in range(HG_HEADS):
            sft_ref[h] = fs[:, _head(h)].T
            sqt_ref[h] = qs[:, _head(h)].T
        pair = SAMPLE_HEADS * HG_DV
        for p in range(HG_HEADS // SAMPLE_HEADS):
            sv_ref[p] = zs[:, C_I + p * pair:C_I + (p + 1) * pair]
        sglu_ref[...] = zs[:, C_A:C_B] * jax.nn.sigmoid(zs[:, C_B:C_END])
        sgate_ref[...] = _silu(zs[:, C_G:C_A])

    @pl.when(mixing & (t == 0))
    def _meta_tokens():
        zm = _project(meta_ref[...], n1g, win_ref)
        f = lb + (1.0 - lb) * jax.nn.sigmoid(zm[:, C_F:C_I])
        b = _cumsum_rows(jnp.log(f))
        k_end = ((1.0 - f) * jnp.exp(b[N_META - 1:N_META] - b)).astype(BF16)
        v = zm[:, C_I:C_G].astype(BF16)
        for h in range(HG_HEADS):
            st_ref[h] = _dot_tn(v[:, _head(h)], k_end[:, _head(h)])
        e_ref[0:HIST - N_META, :] = jnp.zeros((HIST - N_META, CV_WIDTH), F32)
        e_ref[HIST - N_META:HIST, :] = zm[:, C_A:C_B] * jax.nn.sigmoid(zm[:, C_B:C_END])

    cb = cb_ref[...]
    z_ref[...] = _project(x_ref[0], n1g, win_ref)
    _sample_state_update(s, sst_ref, ssc_ref, snst_ref, snsc_ref, cw_ref, cb, sft_ref, sqt_ref, sv_ref, sglu_ref,
                         so_ref, sdw_ref)
    hn_ref[...] = _rms(h_ref[...], n2g_ref[...]).astype(BF16)
    y_out_ref[0] = h_ref[...]
    for r0 in range(0, tile_t, CUMSUM_ROWS):
        rows = slice(r0, r0 + CUMSUM_ROWS)
        f = lb + (1.0 - lb) * jax.nn.sigmoid(z_ref[rows, C_F:C_I])
        z_ref[rows, C_F:C_I] = f
        b_ref[rows, :] = _chunk_cumsum(jnp.log(f))
    causal = _tril(CHUNK)
    og = og_ref[...]
    n_chunks = tile_t // CHUNK
    ff = wup_ref.shape[1] // n_chunks
    fft = max(ff // HG_HEADS, MXU_COLS)
    dt = y_out_ref.shape[2] // (CV_WIDTH // 128)

    for c in range(n_chunks):
        rows = slice(c * CHUNK, (c + 1) * CHUNK)
        f = z_ref[rows, C_F:C_I]
        k = 1.0 - f
        b = b_ref[rows, :]
        b_end = b[CHUNK - 1:CHUNK]
        q_dec = (_silu(z_ref[rows, C_Q:C_F]) * jnp.exp(b)).astype(BF16)
        k_inv = (k * jnp.exp(-b)).astype(BF16)
        k_end = (k * jnp.exp(b_end - b)).astype(BF16)
        decay = jnp.exp(b_end)
        v = z_ref[rows, C_I:C_G].astype(BF16)
        gate = _silu(z_ref[rows, C_G:C_A])
        e_ref[HIST + c * CHUNK:HIST + (c + 1) * CHUNK, :] = z_ref[rows, C_A:C_B] * jax.nn.sigmoid(z_ref[rows, C_B:C_END])
        for h in range(HG_HEADS):
            if (h + 1) * fft <= ff:
                ucols = slice(c * ff + h * fft, c * ff + (h + 1) * fft)
                u = jnp.dot(hn_ref[...], wup_ref[:, ucols], preferred_element_type=F32)
                a_ref[:, h * fft:(h + 1) * fft] = jnp.square(jnp.maximum(u, 0.0)).astype(BF16)
            hs = _head(h)
            scores = jnp.where(causal, _dot_nt(q_dec[:, hs], k_inv[:, hs]), 0.0)
            s_t = st_ref[h]
            o = (jnp.dot(scores.astype(BF16), v[:, hs], preferred_element_type=F32)
                 + _dot_nt(q_dec[:, hs], s_t.astype(BF16)))
            st_ref[h] = s_t * decay[:, hs] + _dot_tn(v[:, hs], k_end[:, hs])
            y_ref[rows, hs] = (_rms(o, og) * gate[:, hs]).astype(BF16)
        dw = []
        for g in range(CV_WIDTH // 128):
            dw.append(_dwconv_lane_group(e_ref, cw_ref, cb, HIST + c * CHUNK, CHUNK, g))
            dcols = slice(g * dt, (g + 1) * dt)
            y_out_ref[0, :, dcols] += jnp.dot(a_ref[...], wdn_ref[c * ff:(c + 1) * ff, dcols],
                                              preferred_element_type=F32)
        dw = jnp.concatenate(dw, axis=-1)
        y_ref[rows, HG_WIDTH:] = _silu(_layernorm(dw, lng_ref[...], lnb_ref[...])).astype(BF16)

    y_out_ref[0] = _rms(y_out_ref[0], fg_ref[...])
    h_ref[...] = x_ref[0] + jnp.dot(y_ref[...], wout_ref[...], preferred_element_type=F32)
    e_ref[0:HIST, :] = e_ref[tile_t:tile_t + HIST, :]

    @pl.when(mixing & (t == nt - 1))
    def _final_state():
        for h in range(HG_HEADS):
            sfin_ref[0, h] = st_ref[h].T
        cfin_ref[0] = e_ref[tile_t + HIST - (CONV_K - 1):tile_t + HIST, :]

    @pl.when(s == ns)
    def _sample_outputs():
        y_hg = []
        for h in range(HG_HEADS):
            p, hh = divmod(h, SAMPLE_HEADS)
            o_h = so_ref[p, :, hh * HG_DV:(hh + 1) * HG_DV]
            y_hg.append((_rms(o_h, og) * sgate_ref[:, _head(h)]).astype(BF16))
        y_cv = _silu(_layernorm(sdw_ref[...], lng_ref[...], lnb_ref[...])).astype(BF16)
        ycat = jnp.concatenate(y_hg + [y_cv], axis=-1)
        hs = sx_ref[...] + jnp.dot(ycat, wout_ref[...], preferred_element_type=F32)
        ys_ref[...] = _mlp(hs, n2g_ref[...], wup_ref, wdn_ref, fg_ref[...])


def _weight_slots(stage_ref):
    per_row = stage_ref.shape[1] // W_COLS
    n_slot = (stage_ref.shape[0] // W_ROWS) * per_row

    def slot(k):
        r, c = divmod(k, per_row)
        return stage_ref.at[pl.ds(r * W_ROWS, W_ROWS), pl.ds(c * W_COLS, W_COLS)]

    return n_slot, slot


def _load_weights_as_bf16(pairs, stage_ref, sem):
    n_slot, slot = _weight_slots(stage_ref)
    blocks = [(src, dst, r, c) for src, dst in pairs
              for r in range(0, dst.shape[0], W_ROWS) for c in range(0, dst.shape[1], W_COLS)]

    def copy(i):
        src, _, r, c = blocks[i]
        k = i % n_slot
        return pltpu.make_async_copy(src.at[0, pl.ds(r, W_ROWS), pl.ds(c, W_COLS)], slot(k), sem.at[k])

    for i in range(min(n_slot, len(blocks))):
        copy(i).start()
    for i, (_, dst, r, c) in enumerate(blocks):
        copy(i).wait()
        dst[r:r + W_ROWS, c:c + W_COLS] = slot(i % n_slot)[...].astype(BF16)
        if i + n_slot < len(blocks):
            copy(i + n_slot).start()


def _sample_units(s, nseq):
    n_pairs = HG_HEADS // SAMPLE_HEADS
    unit = jnp.minimum(s, nseq // SAMPLE_SEQS * n_pairs - 1)
    return unit // n_pairs, unit % n_pairs, jnp.minimum(s, nseq // SAMPLE_SEQS - 1)


def _sample_state_update(s, sst_ref, ssc_ref, snst_ref, snsc_ref, cw_ref, cb, sft_ref, sqt_ref, sv_ref, sglu_ref,
                         so_ref, sdw_ref):
    nseq = sft_ref.shape[2]
    sb, hp, cblk = _sample_units(s, nseq)
    r0 = pl.multiple_of(sb * SAMPLE_SEQS, SAMPLE_SEQS)
    shift = lax.rem(nseq - r0, nseq)
    v_rows = sv_ref[hp, pl.ds(r0, SAMPLE_SEQS), :]
    seq_row = lax.broadcasted_iota(jnp.int32, (SAMPLE_SEQS, HG_DV), 0)
    for hh in range(SAMPLE_HEADS):
        h = hp * SAMPLE_HEADS + hh
        f_t = pltpu.roll(sft_ref[h], shift, 1)
        q_t = pltpu.roll(sqt_ref[h], shift, 1)
        o = jnp.zeros((SAMPLE_SEQS, HG_DV), F32)
        for j in range(SAMPLE_SEQS):
            f_b = jnp.broadcast_to(f_t[:, j:j + 1], (HG_DK, HG_DV))
            s_new = f_b * sst_ref[j, hh] + (1.0 - f_b) * v_rows[j:j + 1, hh * HG_DV:(hh + 1) * HG_DV]
            snst_ref[j, hh] = s_new
            q_b = jnp.broadcast_to(q_t[:, j:j + 1], (HG_DK, HG_DV))
            o = jnp.where(seq_row == j, jnp.sum(q_b * s_new, axis=0, keepdims=True), o)
        so_ref[hp, pl.ds(r0, SAMPLE_SEQS), hh * HG_DV:(hh + 1) * HG_DV] = o
    c0 = pl.multiple_of(cblk * SAMPLE_SEQS, SAMPLE_SEQS)
    glu = sglu_ref[pl.ds(c0, SAMPLE_SEQS), :]
    taps = CONV_K - 1
    dw = cb + cw_ref[taps:taps + 1, :] * glu
    for j in range(taps):
        dw = dw + cw_ref[j:j + 1, :] * ssc_ref[j]
    sdw_ref[pl.ds(c0, SAMPLE_SEQS), :] = dw
    snsc_ref[0:taps - 1] = ssc_ref[1:taps]
    snsc_ref[taps - 1] = glu


def _const_spec(shape, single=False):
    zeros = (0,) * len(shape)
    if single:
        return pl.BlockSpec(shape, lambda *_: zeros, pipeline_mode=pl.Buffered(1))
    return pl.BlockSpec(shape, lambda *_: zeros)


def _layer(x, meta, lbp, n1g, win, og, cw, cb, lng, lnb, wout, n2g, wup, wdn, fg, xs, sst, ssc):
    bsz, seq, d = x.shape
    nt = seq // TILE_T
    ns = bsz * nt
    nseq = xs.shape[0]
    n_pairs = HG_HEADS // SAMPLE_HEADS
    assert nseq // SAMPLE_SEQS * n_pairs <= ns, "one (sequence block, head pair) per grid step"

    def mix_tile(s):
        s1 = jnp.minimum(s, ns - 1)
        return s1 // nt, s1 % nt

    def mlp_tile(s):
        s2 = jnp.maximum(s - 1, 0)
        return s2 // nt, s2 % nt

    state_spec = pl.BlockSpec((SAMPLE_SEQS, SAMPLE_HEADS, HG_DK, HG_DV),
                              lambda s: (*_sample_units(s, nseq)[:2], 0, 0))
    conv_spec = pl.BlockSpec((CONV_K - 1, SAMPLE_SEQS, CV_WIDTH), lambda s: (0, _sample_units(s, nseq)[2], 0))
    pair = SAMPLE_HEADS * HG_DV
    hbm = pl.BlockSpec(memory_space=pl.ANY)
    n_slot = (TILE_T // W_ROWS) * (C_END // W_COLS)

    return pl.pallas_call(
        functools.partial(_layer_kernel, tile_t=TILE_T, nt=nt),
        grid=(ns + 1,),
        in_specs=[
            pl.BlockSpec((1, TILE_T, d), lambda s: (*mix_tile(s), 0)),
            _const_spec(meta.shape), _const_spec(lbp.shape), _const_spec(n1g.shape),
            hbm, _const_spec(og.shape), _const_spec(cw.shape),
            _const_spec(cb.shape), _const_spec(lng.shape), _const_spec(lnb.shape),
            hbm, _const_spec(n2g.shape), hbm, hbm, _const_spec(fg.shape),
            _const_spec(xs.shape, single=True), state_spec, conv_spec,
        ],
        out_specs=[
            pl.BlockSpec((1, TILE_T, d), lambda s: (*mlp_tile(s), 0)),
            pl.BlockSpec((1, HG_HEADS, HG_DK, HG_DV), lambda s: (mix_tile(s)[0], 0, 0, 0)),
            pl.BlockSpec((1, CONV_K - 1, CV_WIDTH), lambda s: (mix_tile(s)[0], 0, 0)),
            state_spec, conv_spec, _const_spec(xs.shape),
        ],
        out_shape=[
            jax.ShapeDtypeStruct((bsz, seq, d), F32),
            jax.ShapeDtypeStruct((bsz, HG_HEADS, HG_DK, HG_DV), F32),
            jax.ShapeDtypeStruct((bsz, CONV_K - 1, CV_WIDTH), F32),
            jax.ShapeDtypeStruct(sst.shape, F32),
            jax.ShapeDtypeStruct(ssc.shape, F32),
            jax.ShapeDtypeStruct(xs.shape, F32),
        ],
        scratch_shapes=[
            pltpu.VMEM(win.shape[1:], BF16), pltpu.VMEM(wout.shape[1:], BF16),
            pltpu.VMEM(wup.shape[1:], BF16), pltpu.VMEM(wdn.shape[1:], BF16),
            pltpu.SemaphoreType.DMA((n_slot,)),
            pltpu.VMEM((TILE_T, C_END), F32),
            pltpu.VMEM((TILE_T, HG_WIDTH), F32),
            pltpu.VMEM((TILE_T, HG_WIDTH + CV_WIDTH), BF16),
            pltpu.VMEM((HG_HEADS, HG_DV, HG_DK), F32),
            pltpu.VMEM((HIST + TILE_T, CV_WIDTH), F32),
            pltpu.VMEM((TILE_T, d), F32),
            pltpu.VMEM((TILE_T, d), BF16),
            pltpu.VMEM((TILE_T, wup.shape[2] // (TILE_T // CHUNK)), BF16),
            pltpu.VMEM((HG_HEADS, HG_DK, nseq), F32),
            pltpu.VMEM((HG_HEADS, HG_DK, nseq), F32),
            pltpu.VMEM((n_pairs, nseq, pair), F32),
            pltpu.VMEM((nseq, CV_WIDTH), F32),
            pltpu.VMEM((nseq, HG_WIDTH), F32),
            pltpu.VMEM((n_pairs, nseq, pair), F32),
            pltpu.VMEM((nseq, CV_WIDTH), F32),
        ],
        compiler_params=pltpu.CompilerParams(dimension_semantics=("arbitrary",), vmem_limit_bytes=VMEM_LIMIT),
        name="layer",
    )(x, meta, lbp, n1g, win, og, cw, cb, lng, lnb, wout, n2g, wup, wdn, fg, xs, sst, ssc)


def kernel(x_prompt, x_sample, state_hgrn, state_conv, meta_tokens, hg_lb, norm1_g, w_in, hg_onorm_g, conv_w, conv_b,
           conv_ln_g, conv_ln_b, w_out, norm2_g, w_up, w_down, final_g):
    assert state_hgrn.shape[0] == 1, "single-layer stack"
    bsz, seq, d = x_prompt.shape
    row = lambda a: a.reshape(1, -1)
    n1g, og, cb, lng, lnb, n2g, fg = (row(norm1_g[0]), row(hg_onorm_g[0]), row(conv_b[0]), row(conv_ln_g[0]),
                                      row(conv_ln_b[0]), row(norm2_g[0]), row(final_g))
    cw = conv_w[0]

    assert x_sample.shape[1] == 1, "one new token per running sequence"
    x_s = x_sample.reshape(x_sample.shape[0], d)
    y_p, s_p, c_p, s_s, c_s, y_s = _layer(x_prompt, meta_tokens, hg_lb, n1g, w_in, og, cw, cb, lng, lnb, w_out, n2g,
                                          w_up, w_down, fg, x_s, state_hgrn[0], jnp.swapaxes(state_conv[0], 0, 1))

    return (y_p, y_s.reshape(x_sample.shape), s_p[None], c_p[None], s_s[None], jnp.swapaxes(c_s, 0, 1)[None])
```

```python
import functools

import jax
import jax.numpy as jnp
from jax import lax
from jax.experimental import pallas as pl
from jax.experimental.pallas import tpu as pltpu

F32 = jnp.float32
BF16 = jnp.bfloat16

N_META = 16
HG_HEADS = 4
HG_DK = 128
HG_DV = 128
HG_WIDTH = HG_HEADS * HG_DK
CV_WIDTH = 512
CONV_K = 31
CHUNK = 64
EPS = 1e-6
C_Q, C_F, C_I, C_G, C_A, C_B, C_END = (0, HG_WIDTH, 2 * HG_WIDTH, 3 * HG_WIDTH, 4 * HG_WIDTH,
                                       4 * HG_WIDTH + CV_WIDTH, 4 * HG_WIDTH + 2 * CV_WIDTH)
HIST = 32
MXU_COLS = 256
CUMSUM_ROWS = 256
W_ROWS, W_COLS = 256, 1024
TILE_T = 512
SAMPLE_SEQS = 8
SAMPLE_HEADS = 2
VMEM_LIMIT = 62 * 1024 * 1024


def _rms(x, g):
    return x * lax.rsqrt(jnp.mean(x * x, axis=-1, keepdims=True) + EPS) * g


def _layernorm(x, g, b):
    xc = x - jnp.mean(x, axis=-1, keepdims=True)
    return xc * lax.rsqrt(jnp.mean(xc * xc, axis=-1, keepdims=True) + EPS) * g + b


def _silu(x):
    return x * jax.nn.sigmoid(x)


def _lower_bound(lbp):
    e = jnp.exp(lbp - jnp.max(lbp, axis=0, keepdims=True))
    return e[0:1] / jnp.sum(e, axis=0, keepdims=True)


def _project(x, n1g, win_ref):
    hn = _rms(x, n1g).astype(BF16)
    return jnp.dot(hn, win_ref[...], preferred_element_type=F32)


def _tril(n):
    r = lax.broadcasted_iota(jnp.int32, (n, n), 0)
    c = lax.broadcasted_iota(jnp.int32, (n, n), 1)
    return r >= c


def _cumsum_rows(x):
    n = x.shape[0]
    return jnp.dot(_tril(n).astype(F32), x, precision=lax.Precision.HIGHEST, preferred_element_type=F32)


def _chunk_cumsum(x):
    n = x.shape[0]
    r = lax.broadcasted_iota(jnp.int32, (n, n), 0)
    c = lax.broadcasted_iota(jnp.int32, (n, n), 1)
    tri = ((r >= c) & (r // CHUNK == c // CHUNK)).astype(BF16)
    hi = x.astype(BF16)
    rest = x - hi.astype(F32)
    mid = rest.astype(BF16)
    lo = (rest - mid.astype(F32)).astype(BF16)
    return (jnp.dot(tri, hi, preferred_element_type=F32) + jnp.dot(tri, mid, preferred_element_type=F32)
            + jnp.dot(tri, lo, preferred_element_type=F32))


def _head(h):
    return slice(h * HG_DK, (h + 1) * HG_DK)


def _dot_nt(a, b):
    return lax.dot_general(a, b, (((1,), (1,)), ((), ())), preferred_element_type=F32)


def _dot_tn(a, b):
    return lax.dot_general(a, b, (((0,), (0,)), ((), ())), preferred_element_type=F32)


def _mlp(h, n2g, wup_ref, wdn_ref, fg):
    hn = _rms(h, n2g).astype(BF16)
    u = jnp.dot(hn, wup_ref[...], preferred_element_type=F32)
    a = jnp.square(jnp.maximum(u, 0.0)).astype(BF16)
    h2 = h + jnp.dot(a, wdn_ref[...], preferred_element_type=F32)
    return _rms(h2, fg)


def _dwconv_lane_group(e_ref, cw_ref, bias, base, rows, g):
    sub = 8
    lanes = slice(g * 128, (g + 1) * 128)
    ext = e_ref[base - HIST:base + rows, lanes]
    acc = None
    for r in range(sub):
        u = None
        for a in range(-(-CONV_K // sub)):
            d = sub * a + r
            if d < CONV_K:
                lo = HIST - sub - sub * a
                term = cw_ref[CONV_K - 1 - d:CONV_K - d, lanes] * ext[lo:lo + rows + sub]
                u = term if u is None else u + term
        shifted = u[sub:] if r == 0 else pltpu.roll(u, r, 0)[sub:]
        acc = shifted if acc is None else acc + shifted
    return acc + bias[:, lanes]


def _layer_kernel(x_ref, meta_ref, lbp_ref, n1g_ref, win_hbm, og_ref, cw_ref, cb_ref, lng_ref, lnb_ref,
                  wout_hbm, n2g_ref, wup_hbm, wdn_hbm, fg_ref, sx_ref, sst_ref, ssc_ref,
                  y_out_ref, sfin_ref, cfin_ref, snst_ref, snsc_ref, ys_ref,
                  win_ref, wout_ref, wup_ref, wdn_ref, wsem,
                  z_ref, b_ref, y_ref, st_ref, e_ref, h_ref, hn_ref, a_ref,
                  sft_ref, sqt_ref, sv_ref, sglu_ref, sgate_ref, so_ref, sdw_ref, *, tile_t, nt):
    s = pl.program_id(0)
    ns = pl.num_programs(0) - 1
    mixing = s < ns
    t = lax.rem(jnp.minimum(s, ns - 1), nt)
    lb = _lower_bound(lbp_ref[...])
    n1g = n1g_ref[...]

    @pl.when(s == 0)
    def _first_step():
        _load_weights_as_bf16(((win_hbm, win_ref), (wout_hbm, wout_ref), (wup_hbm, wup_ref), (wdn_hbm, wdn_ref)),
                              z_ref, wsem)
        h_ref[...] = jnp.zeros(h_ref.shape, F32)
        zs = _project(sx_ref[...], n1g, win_ref)
        fs = lb + (1.0 - lb) * jax.nn.sigmoid(zs[:, C_F:C_I])
        qs = _silu(zs[:, C_Q:C_F])
        for h in range(HG_HEADS):
            sft_ref[h] = fs[:, _head(h)].T
            sqt_ref[h] = qs[:, _head(h)].T
        pair = SAMPLE_HEADS * HG_DV
        for p in range(HG_HEADS // SAMPLE_HEADS):
            sv_ref[p] = zs[:, C_I + p * pair:C_I + (p + 1) * pair]
        sglu_ref[...] = zs[:, C_A:C_B] * jax.nn.sigmoid(zs[:, C_B:C_END])
        sgate_ref[...] = _silu(zs[:, C_G:C_A])

    @pl.when(mixing & (t == 0))
    def _meta_tokens():
        zm = _project(meta_ref[...], n1g, win_ref)
        f = lb + (1.0 - lb) * jax.nn.sigmoid(zm[:, C_F:C_I])
        b = _cumsum_rows(jnp.log(f))
        k_end = ((1.0 - f) * jnp.exp(b[N_META - 1:N_META] - b)).astype(BF16)
        v = zm[:, C_I:C_G].astype(BF16)
        for h in range(HG_HEADS):
            st_ref[h] = _dot_tn(v[:, _head(h)], k_end[:, _head(h)])
        e_ref[0:HIST - N_META, :] = jnp.zeros((HIST - N_META, CV_WIDTH), F32)
        e_ref[HIST - N_META:HIST, :] = zm[:, C_A:C_B] * jax.nn.sigmoid(zm[:, C_B:C_END])

    og = og_ref[...]
    n_chunks = tile_t // CHUNK
    ff = wup_ref.shape[1] // n_chunks

    @pl.when(mixing)
    def _tile_step():
        cb = cb_ref[...]
        z_ref[...] = _project(x_ref[0], n1g, win_ref)
        _sample_state_update(s, sst_ref, ssc_ref, snst_ref, snsc_ref, cw_ref, cb, sft_ref, sqt_ref, sv_ref, sglu_ref,
                             so_ref, sdw_ref)
        hn_ref[...] = _rms(h_ref[...], n2g_ref[...]).astype(BF16)
        y_out_ref[0] = h_ref[...]
        for r0 in range(0, tile_t, CUMSUM_ROWS):
            rows = slice(r0, r0 + CUMSUM_ROWS)
            f = lb + (1.0 - lb) * jax.nn.sigmoid(z_ref[rows, C_F:C_I])
            z_ref[rows, C_F:C_I] = f
            b_ref[rows, :] = _chunk_cumsum(jnp.log(f))
        causal = _tril(CHUNK)
        fft = max(ff // HG_HEADS, MXU_COLS)
        dt = y_out_ref.shape[2] // (CV_WIDTH // 128)

        for c in range(n_chunks):
            rows = slice(c * CHUNK, (c + 1) * CHUNK)
            f = z_ref[rows, C_F:C_I]
            k = 1.0 - f
            b = b_ref[rows, :]
            b_end = b[CHUNK - 1:CHUNK]
            q_dec = (_silu(z_ref[rows, C_Q:C_F]) * jnp.exp(b)).astype(BF16)
            k_inv = (k * jnp.exp(-b)).astype(BF16)
            k_end = (k * jnp.exp(b_end - b)).astype(BF16)
            decay = jnp.exp(b_end)
            v = z_ref[rows, C_I:C_G].astype(BF16)
            gate = _silu(z_ref[rows, C_G:C_A])
            e_ref[HIST + c * CHUNK:HIST + (c + 1) * CHUNK, :] = (z_ref[rows, C_A:C_B]
                                                                 * jax.nn.sigmoid(z_ref[rows, C_B:C_END]))
            for h in range(HG_HEADS):
                if (h + 1) * fft <= ff:
                    ucols = slice(c * ff + h * fft, c * ff + (h + 1) * fft)
                    u = jnp.dot(hn_ref[...], wup_ref[:, ucols], preferred_element_type=F32)
                    a_ref[:, h * fft:(h + 1) * fft] = jnp.square(jnp.maximum(u, 0.0)).astype(BF16)
                hs = _head(h)
                scores = jnp.where(causal, _dot_nt(q_dec[:, hs], k_inv[:, hs]), 0.0)
                s_t = st_ref[h]
                o = (jnp.dot(scores.astype(BF16), v[:, hs], preferred_element_type=F32)
                     + _dot_nt(q_dec[:, hs], s_t.astype(BF16)))
                st_ref[h] = s_t * decay[:, hs] + _dot_tn(v[:, hs], k_end[:, hs])
                y_ref[rows, hs] = (_rms(o, og) * gate[:, hs]).astype(BF16)
            dw = []
            for g in range(CV_WIDTH // 128):
                dw.append(_dwconv_lane_group(e_ref, cw_ref, cb, HIST + c * CHUNK, CHUNK, g))
                dcols = slice(g * dt, (g + 1) * dt)
                y_out_ref[0, :, dcols] += jnp.dot(a_ref[...], wdn_ref[c * ff:(c + 1) * ff, dcols],
                                                  preferred_element_type=F32)
            dw = jnp.concatenate(dw, axis=-1)
            y_ref[rows, HG_WIDTH:] = _silu(_layernorm(dw, lng_ref[...], lnb_ref[...])).astype(BF16)

        y_out_ref[0] = _rms(y_out_ref[0], fg_ref[...])
        h_ref[...] = x_ref[0] + jnp.dot(y_ref[...], wout_ref[...], preferred_element_type=F32)
        e_ref[0:HIST, :] = e_ref[tile_t:tile_t + HIST, :]

    @pl.when(s == ns)
    def _last_tile_mlp():
        hn_ref[...] = _rms(h_ref[...], n2g_ref[...]).astype(BF16)
        y_out_ref[0] = h_ref[...]
        for c in range(n_chunks):
            cols = slice(c * ff, (c + 1) * ff)
            u = jnp.dot(hn_ref[...], wup_ref[:, cols], preferred_element_type=F32)
            a_ref[...] = jnp.square(jnp.maximum(u, 0.0)).astype(BF16)
            y_out_ref[0] += jnp.dot(a_ref[...], wdn_ref[cols, :], preferred_element_type=F32)
        y_out_ref[0] = _rms(y_out_ref[0], fg_ref[...])

    @pl.when(mixing & (t == nt - 1))
    def _final_state():
        for h in range(HG_HEADS):
            sfin_ref[0, h] = st_ref[h].T
        cfin_ref[0] = e_ref[tile_t + HIST - (CONV_K - 1):tile_t + HIST, :]

    @pl.when(s == ns)
    def _sample_outputs():
        y_hg = []
        for h in range(HG_HEADS):
            p, hh = divmod(h, SAMPLE_HEADS)
            o_h = so_ref[p, :, hh * HG_DV:(hh + 1) * HG_DV]
            y_hg.append((_rms(o_h, og) * sgate_ref[:, _head(h)]).astype(BF16))
        y_cv = _silu(_layernorm(sdw_ref[...], lng_ref[...], lnb_ref[...])).astype(BF16)
        ycat = jnp.concatenate(y_hg + [y_cv], axis=-1)
        hs = sx_ref[...] + jnp.dot(ycat, wout_ref[...], preferred_element_type=F32)
        ys_ref[...] = _mlp(hs, n2g_ref[...], wup_ref, wdn_ref, fg_ref[...])


def _weight_slots(stage_ref):
    per_row = stage_ref.shape[1] // W_COLS
    n_slot = (stage_ref.shape[0] // W_ROWS) * per_row

    def slot(k):
        r, c = divmod(k, per_row)
        return stage_ref.at[pl.ds(r * W_ROWS, W_ROWS), pl.ds(c * W_COLS, W_COLS)]

    return n_slot, slot


def _load_weights_as_bf16(pairs, stage_ref, sem):
    n_slot, slot = _weight_slots(stage_ref)
    blocks = [(src, dst, r, c) for src, dst in pairs
              for r in range(0, dst.shape[0], W_ROWS) for c in range(0, dst.shape[1], W_COLS)]

    def copy(i):
        src, _, r, c = blocks[i]
        k = i % n_slot
        return pltpu.make_async_copy(src.at[0, pl.ds(r, W_ROWS), pl.ds(c, W_COLS)], slot(k), sem.at[k])

    for i in range(min(n_slot, len(blocks))):
        copy(i).start()
    for i, (_, dst, r, c) in enumerate(blocks):
        copy(i).wait()
        dst[r:r + W_ROWS, c:c + W_COLS] = slot(i % n_slot)[...].astype(BF16)
        if i + n_slot < len(blocks):
            copy(i + n_slot).start()


def _sample_units(s, nseq):
    n_pairs = HG_HEADS // SAMPLE_HEADS
    unit = jnp.minimum(s, nseq // SAMPLE_SEQS * n_pairs - 1)
    return unit // n_pairs, unit % n_pairs, jnp.minimum(s, nseq // SAMPLE_SEQS - 1)


def _sample_state_update(s, sst_ref, ssc_ref, snst_ref, snsc_ref, cw_ref, cb, sft_ref, sqt_ref, sv_ref, sglu_ref,
                         so_ref, sdw_ref):
    nseq = sft_ref.shape[2]
    sb, hp, cblk = _sample_units(s, nseq)
    r0 = pl.multiple_of(sb * SAMPLE_SEQS, SAMPLE_SEQS)
    shift = lax.rem(nseq - r0, nseq)
    v_rows = sv_ref[hp, pl.ds(r0, SAMPLE_SEQS), :]
    seq_row = lax.broadcasted_iota(jnp.int32, (SAMPLE_SEQS, HG_DV), 0)
    for hh in range(SAMPLE_HEADS):
        h = hp * SAMPLE_HEADS + hh
        f_t = pltpu.roll(sft_ref[h], shift, 1)
        q_t = pltpu.roll(sqt_ref[h], shift, 1)
        o = jnp.zeros((SAMPLE_SEQS, HG_DV), F32)
        for j in range(SAMPLE_SEQS):
            f_b = jnp.broadcast_to(f_t[:, j:j + 1], (HG_DK, HG_DV))
            s_new = f_b * sst_ref[j, hh] + (1.0 - f_b) * v_rows[j:j + 1, hh * HG_DV:(hh + 1) * HG_DV]
            snst_ref[j, hh] = s_new
            q_b = jnp.broadcast_to(q_t[:, j:j + 1], (HG_DK, HG_DV))
            o = jnp.where(seq_row == j, jnp.sum(q_b * s_new, axis=0, keepdims=True), o)
        so_ref[hp, pl.ds(r0, SAMPLE_SEQS), hh * HG_DV:(hh + 1) * HG_DV] = o
    c0 = pl.multiple_of(cblk * SAMPLE_SEQS, SAMPLE_SEQS)
    glu = sglu_ref[pl.ds(c0, SAMPLE_SEQS), :]
    taps = CONV_K - 1
    dw = cb + cw_ref[taps:taps + 1, :] * glu
    for j in range(taps):
        dw = dw + cw_ref[j:j + 1, :] * ssc_ref[j]
    sdw_ref[pl.ds(c0, SAMPLE_SEQS), :] = dw
    snsc_ref[0:taps - 1] = ssc_ref[1:taps]
    snsc_ref[taps - 1] = glu


def _const_spec(shape, single=False):
    zeros = (0,) * len(shape)
    if single:
        return pl.BlockSpec(shape, lambda *_: zeros, pipeline_mode=pl.Buffered(1))
    return pl.BlockSpec(shape, lambda *_: zeros)


def _layer(x, meta, lbp, n1g, win, og, cw, cb, lng, lnb, wout, n2g, wup, wdn, fg, xs, sst, ssc):
    bsz, seq, d = x.shape
    nt = seq // TILE_T
    ns = bsz * nt
    nseq = xs.shape[0]
    n_pairs = HG_HEADS // SAMPLE_HEADS
    assert nseq // SAMPLE_SEQS * n_pairs <= ns, "one (sequence block, head pair) per grid step"

    def mix_tile(s):
        s1 = jnp.minimum(s, ns - 1)
        return s1 // nt, s1 % nt

    def mlp_tile(s):
        s2 = jnp.maximum(s - 1, 0)
        return s2 // nt, s2 % nt

    state_spec = pl.BlockSpec((SAMPLE_SEQS, SAMPLE_HEADS, HG_DK, HG_DV),
                              lambda s: (*_sample_units(s, nseq)[:2], 0, 0))
    conv_spec = pl.BlockSpec((CONV_K - 1, SAMPLE_SEQS, CV_WIDTH), lambda s: (0, _sample_units(s, nseq)[2], 0))
    pair = SAMPLE_HEADS * HG_DV
    hbm = pl.BlockSpec(memory_space=pl.ANY)
    n_slot = (TILE_T // W_ROWS) * (C_END // W_COLS)

    return pl.pallas_call(
        functools.partial(_layer_kernel, tile_t=TILE_T, nt=nt),
        grid=(ns + 1,),
        in_specs=[
            pl.BlockSpec((1, TILE_T, d), lambda s: (*mix_tile(s), 0)),
            _const_spec(meta.shape), _const_spec(lbp.shape), _const_spec(n1g.shape),
            hbm, _const_spec(og.shape), _const_spec(cw.shape),
            _const_spec(cb.shape), _const_spec(lng.shape), _const_spec(lnb.shape),
            hbm, _const_spec(n2g.shape), hbm, hbm, _const_spec(fg.shape),
            _const_spec(xs.shape, single=True), state_spec, conv_spec,
        ],
        out_specs=[
            pl.BlockSpec((1, TILE_T, d), lambda s: (*mlp_tile(s), 0)),
            pl.BlockSpec((1, HG_HEADS, HG_DK, HG_DV), lambda s: (mix_tile(s)[0], 0, 0, 0)),
            pl.BlockSpec((1, CONV_K - 1, CV_WIDTH), lambda s: (mix_tile(s)[0], 0, 0)),
            state_spec, conv_spec, _const_spec(xs.shape),
        ],
        out_shape=[
            jax.ShapeDtypeStruct((bsz, seq, d), F32),
            jax.ShapeDtypeStruct((bsz, HG_HEADS, HG_DK, HG_DV), F32),
            jax.ShapeDtypeStruct((bsz, CONV_K - 1, CV_WIDTH), F32),
            jax.ShapeDtypeStruct(sst.shape, F32),
            jax.ShapeDtypeStruct(ssc.shape, F32),
            jax.ShapeDtypeStruct(xs.shape, F32),
        ],
        scratch_shapes=[
            pltpu.VMEM(win.shape[1:], BF16), pltpu.VMEM(wout.shape[1:], BF16),
            pltpu.VMEM(wup.shape[1:], BF16), pltpu.VMEM(wdn.shape[1:], BF16),
            pltpu.SemaphoreType.DMA((n_slot,)),
            pltpu.VMEM((TILE_T, C_END), F32),
            pltpu.VMEM((TILE_T, HG_WIDTH), F32),
            pltpu.VMEM((TILE_T, HG_WIDTH + CV_WIDTH), BF16),
            pltpu.VMEM((HG_HEADS, HG_DV, HG_DK), F32),
            pltpu.VMEM((HIST + TILE_T, CV_WIDTH), F32),
            pltpu.VMEM((TILE_T, d), F32),
            pltpu.VMEM((TILE_T, d), BF16),
            pltpu.VMEM((TILE_T, wup.shape[2] // (TILE_T // CHUNK)), BF16),
            pltpu.VMEM((HG_HEADS, HG_DK, nseq), F32),
            pltpu.VMEM((HG_HEADS, HG_DK, nseq), F32),
            pltpu.VMEM((n_pairs, nseq, pair), F32),
            pltpu.VMEM((nseq, CV_WIDTH), F32),
            pltpu.VMEM((nseq, HG_WIDTH), F32),
            pltpu.VMEM((n_pairs, nseq, pair), F32),
            pltpu.VMEM((nseq, CV_WIDTH), F32),
        ],
        compiler_params=pltpu.CompilerParams(dimension_semantics=("arbitrary",), vmem_limit_bytes=VMEM_LIMIT),
        name="layer",
    )(x, meta, lbp, n1g, win, og, cw, cb, lng, lnb, wout, n2g, wup, wdn, fg, xs, sst, ssc)


def kernel(x_prompt, x_sample, state_hgrn, state_conv, meta_tokens, hg_lb, norm1_g, w_in, hg_onorm_g, conv_w, conv_b,
           conv_ln_g, conv_ln_b, w_out, norm2_g, w_up, w_down, final_g):
    assert state_hgrn.shape[0] == 1, "single-layer stack"
    bsz, seq, d = x_prompt.shape
    row = lambda a: a.reshape(1, -1)
    n1g, og, cb, lng, lnb, n2g, fg = (row(norm1_g[0]), row(hg_onorm_g[0]), row(conv_b[0]), row(conv_ln_g[0]),
                                      row(conv_ln_b[0]), row(norm2_g[0]), row(final_g))
    cw = conv_w[0]

    assert x_sample.shape[1] == 1, "one new token per running sequence"
    x_s = x_sample.reshape(x_sample.shape[0], d)
    y_p, s_p, c_p, s_s, c_s, y_s = _layer(x_prompt, meta_tokens, hg_lb, n1g, w_in, og, cw, cb, lng, lnb, w_out, n2g,
                                          w_up, w_down, fg, x_s, state_hgrn[0], jnp.swapaxes(state_conv[0], 0, 1))

    return (y_p, y_s.reshape(x_sample.shape), s_p[None], c_p[None], s_s[None], jnp.swapaxes(c_s, 0, 1)[None])
```

```python
import functools

import jax
import jax.numpy as jnp
from jax import lax
from jax.experimental import pallas as pl
from jax.experimental.pallas import tpu as pltpu

F32 = jnp.float32
BF16 = jnp.bfloat16

N_META = 16
HG_HEADS = 4
HG_DK = 128
HG_DV = 128
HG_WIDTH = HG_HEADS * HG_DK
CV_WIDTH = 512
CONV_K = 31
CHUNK = 64
EPS = 1e-6
C_Q, C_F, C_I, C_G, C_A, C_B, C_END = (0, HG_WIDTH, 2 * HG_WIDTH, 3 * HG_WIDTH, 4 * HG_WIDTH,
                                       4 * HG_WIDTH + CV_WIDTH, 4 * HG_WIDTH + 2 * CV_WIDTH)
HIST = 32
MXU_COLS = 256
CUMSUM_ROWS = 256
W_ROWS, W_COLS = 256, 1024
TILE_T = 512
SAMPLE_SEQS = 8
SAMPLE_HEADS = 2
VMEM_LIMIT = 60 * 1024 * 1024


def _rms(x, g):
    return x * lax.rsqrt(jnp.mean(x * x, axis=-1, keepdims=True) + EPS) * g


def _layernorm(x, g, b):
    xc = x - jnp.mean(x, axis=-1, keepdims=True)
    return xc * lax.rsqrt(jnp.mean(xc * xc, axis=-1, keepdims=True) + EPS) * g + b


def _silu(x):
    return x * jax.nn.sigmoid(x)


def _lower_bound(lbp):
    e = jnp.exp(lbp - jnp.max(lbp, axis=0, keepdims=True))
    return e[0:1] / jnp.sum(e, axis=0, keepdims=True)


def _project(x, n1g, win_ref):
    hn = _rms(x, n1g).astype(BF16)
    return jnp.dot(hn, win_ref[...], preferred_element_type=F32)


def _tril(n):
    r = lax.broadcasted_iota(jnp.int32, (n, n), 0)
    c = lax.broadcasted_iota(jnp.int32, (n, n), 1)
    return r >= c


def _cumsum_rows(x):
    n = x.shape[0]
    return jnp.dot(_tril(n).astype(F32), x, precision=lax.Precision.HIGHEST, preferred_element_type=F32)


def _chunk_cumsum(x):
    n = x.shape[0]
    r = lax.broadcasted_iota(jnp.int32, (n, n), 0)
    c = lax.broadcasted_iota(jnp.int32, (n, n), 1)
    tri = ((r >= c) & (r // CHUNK == c // CHUNK)).astype(BF16)
    hi = x.astype(BF16)
    rest = x - hi.astype(F32)
    mid = rest.astype(BF16)
    lo = (rest - mid.astype(F32)).astype(BF16)
    return (jnp.dot(tri, hi, preferred_element_type=F32) + jnp.dot(tri, mid, preferred_element_type=F32)
            + jnp.dot(tri, lo, preferred_element_type=F32))


def _head(h):
    return slice(h * HG_DK, (h + 1) * HG_DK)


def _dot_nt(a, b):
    return lax.dot_general(a, b, (((1,), (1,)), ((), ())), preferred_element_type=F32)


def _dot_tn(a, b):
    return lax.dot_general(a, b, (((0,), (0,)), ((), ())), preferred_element_type=F32)


def _mlp(h, n2g, wup_ref, wdn_ref, fg):
    hn = _rms(h, n2g).astype(BF16)
    h2 = h
    step = 4 * MXU_COLS
    for c0 in range(0, wup_ref.shape[1], step):
        u = jnp.dot(hn, wup_ref[:, c0:c0 + step], preferred_element_type=F32)
        a = jnp.square(jnp.maximum(u, 0.0)).astype(BF16)
        h2 = h2 + jnp.dot(a, wdn_ref[c0:c0 + step, :], preferred_element_type=F32)
    return _rms(h2, fg)


def _dwconv_lane_group(e_ref, cw_ref, bias, base, rows, g):
    sub = 8
    lanes = slice(g * 128, (g + 1) * 128)
    ext = e_ref[base - HIST:base + rows, lanes]
    acc = None
    for r in range(sub):
        u = None
        for a in range(-(-CONV_K // sub)):
            d = sub * a + r
            if d < CONV_K:
                lo = HIST - sub - sub * a
                term = cw_ref[CONV_K - 1 - d:CONV_K - d, lanes] * ext[lo:lo + rows + sub]
                u = term if u is None else u + term
        shifted = u[sub:] if r == 0 else pltpu.roll(u, r, 0)[sub:]
        acc = shifted if acc is None else acc + shifted
    return acc + bias[:, lanes]


def _layer_kernel(x_ref, meta_ref, lbp_ref, n1g_ref, win_hbm, og_ref, cw_ref, cb_ref, lng_ref, lnb_ref,
                  wout_hbm, n2g_ref, wup_hbm, wdn_hbm, fg_ref, sx_ref, sst_ref, ssc_ref,
                  y_out_ref, sfin_ref, cfin_ref, snst_ref, snsc_ref, ys_ref,
                  win_ref, wout_ref, wup_ref, wdn_ref, wsem,
                  z_ref, b_ref, y_ref, st_ref, e_ref, h_ref, hn_ref, a_ref,
                  sft_ref, sqt_ref, sv_ref, sglu_ref, sgate_ref, so_ref, sdw_ref, *, tile_t, nt):
    s = pl.program_id(0)
    ns = pl.num_programs(0) - 1
    mixing = s < ns
    t = lax.rem(jnp.minimum(s, ns - 1), nt)
    lb = _lower_bound(lbp_ref[...])
    n1g = n1g_ref[...]

    @pl.when(s == 0)
    def _first_step():
        _load_weights_as_bf16(((win_hbm, win_ref), (wout_hbm, wout_ref), (wup_hbm, wup_ref), (wdn_hbm, wdn_ref)),
                              z_ref, wsem)
        h_ref[...] = jnp.zeros(h_ref.shape, F32)
        zs = _project(sx_ref[...], n1g, win_ref)
        fs = lb + (1.0 - lb) * jax.nn.sigmoid(zs[:, C_F:C_I])
        qs = _silu(zs[:, C_Q:C_F])
        for h in range(HG_HEADS):
            sft_ref[h] = fs[:, _head(h)].T
            sqt_ref[h] = qs[:, _head(h)].T
        pair = SAMPLE_HEADS * HG_DV
        for p in range(HG_HEADS // SAMPLE_HEADS):
            sv_ref[p] = zs[:, C_I + p * pair:C_I + (p + 1) * pair]
        sglu_ref[...] = zs[:, C_A:C_B] * jax.nn.sigmoid(zs[:, C_B:C_END])
        sgate_ref[...] = _silu(zs[:, C_G:C_A])

    @pl.when(mixing & (t == 0))
    def _meta_tokens():
        zm = _project(meta_ref[...], n1g, win_ref)
        f = lb + (1.0 - lb) * jax.nn.sigmoid(zm[:, C_F:C_I])
        b = _cumsum_rows(jnp.log(f))
        k_end = ((1.0 - f) * jnp.exp(b[N_META - 1:N_META] - b)).astype(BF16)
        v = zm[:, C_I:C_G].astype(BF16)
        for h in range(HG_HEADS):
            st_ref[h] = _dot_tn(v[:, _head(h)], k_end[:, _head(h)])
        e_ref[0:HIST - N_META, :] = jnp.zeros((HIST - N_META, CV_WIDTH), F32)
        e_ref[HIST - N_META:HIST, :] = zm[:, C_A:C_B] * jax.nn.sigmoid(zm[:, C_B:C_END])

    og = og_ref[...]
    n_chunks = tile_t // CHUNK
    ff = wup_ref.shape[1] // n_chunks

    def _tile_step():
        cb = cb_ref[...]
        z_ref[...] = _project(x_ref[0], n1g, win_ref)
        _sample_state_update(s, sst_ref, ssc_ref, snst_ref, snsc_ref, cw_ref, cb, sft_ref, sqt_ref, sv_ref, sglu_ref,
                             so_ref, sdw_ref)
        hn_ref[...] = _rms(h_ref[...], n2g_ref[...]).astype(BF16)
        y_out_ref[0] = h_ref[...]
        for r0 in range(0, tile_t, CUMSUM_ROWS):
            rows = slice(r0, r0 + CUMSUM_ROWS)
            f = lb + (1.0 - lb) * jax.nn.sigmoid(z_ref[rows, C_F:C_I])
            z_ref[rows, C_F:C_I] = f
            b_ref[rows, :] = _chunk_cumsum(jnp.log(f))
        causal = _tril(CHUNK)
        fft = max(ff // HG_HEADS, MXU_COLS)
        dt = y_out_ref.shape[2] // (CV_WIDTH // 128)

        for c in range(n_chunks):
            rows = slice(c * CHUNK, (c + 1) * CHUNK)
            f = z_ref[rows, C_F:C_I]
            k = 1.0 - f
            b = b_ref[rows, :]
            b_end = b[CHUNK - 1:CHUNK]
            q_dec = (_silu(z_ref[rows, C_Q:C_F]) * jnp.exp(b)).astype(BF16)
            k_inv = k * jnp.exp(-b)
            k_end = (k * jnp.exp(b_end - b)).astype(BF16)
            decay = jnp.exp(b_end)
            v = z_ref[rows, C_I:C_G].astype(BF16)
            gate = _silu(z_ref[rows, C_G:C_A])
            e_ref[HIST + c * CHUNK:HIST + (c + 1) * CHUNK, :] = (z_ref[rows, C_A:C_B]
                                                                 * jax.nn.sigmoid(z_ref[rows, C_B:C_END]))
            for h in range(HG_HEADS):
                if (h + 1) * fft <= ff:
                    ucols = slice(c * ff + h * fft, c * ff + (h + 1) * fft)
                    u = jnp.dot(hn_ref[...], wup_ref[:, ucols], preferred_element_type=F32)
                    a_ref[:, h * fft:(h + 1) * fft] = jnp.square(jnp.maximum(u, 0.0)).astype(BF16)
                hs = _head(h)
                scores = jnp.where(causal, jnp.dot(q_dec[:, hs], k_inv[:, hs].T.astype(BF16),
                                                   preferred_element_type=F32), 0.0)
                s_t = st_ref[h]
                o = (jnp.dot(scores.astype(BF16), v[:, hs], preferred_element_type=F32)
                     + jnp.dot(q_dec[:, hs], s_t.T.astype(BF16), preferred_element_type=F32))
                st_ref[h] = s_t * decay[:, hs] + _dot_tn(v[:, hs], k_end[:, hs])
                y_ref[rows, hs] = (_rms(o, og) * gate[:, hs]).astype(BF16)
            dw = []
            for g in range(CV_WIDTH // 128):
                dw.append(_dwconv_lane_group(e_ref, cw_ref, cb, HIST + c * CHUNK, CHUNK, g))
                dcols = slice(g * dt, (g + 1) * dt)
                y_out_ref[0, :, dcols] += jnp.dot(a_ref[...], wdn_ref[c * ff:(c + 1) * ff, dcols],
                                                  preferred_element_type=F32)
            dw = jnp.concatenate(dw, axis=-1)
            y_ref[rows, HG_WIDTH:] = _silu(_layernorm(dw, lng_ref[...], lnb_ref[...])).astype(BF16)

        y_out_ref[0] = _rms(y_out_ref[0], fg_ref[...])
        h_ref[...] = x_ref[0] + jnp.dot(y_ref[...], wout_ref[...], preferred_element_type=F32)
        e_ref[0:HIST, :] = e_ref[tile_t:tile_t + HIST, :]

    _tile_step()

    @pl.when(mixing & (t == nt - 1))
    def _final_state():
        for h in range(HG_HEADS):
            sfin_ref[0, h] = st_ref[h].T
        cfin_ref[0] = e_ref[tile_t + HIST - (CONV_K - 1):tile_t + HIST, :]

    @pl.when(s == ns)
    def _sample_outputs():
        y_hg = []
        for h in range(HG_HEADS):
            p, hh = divmod(h, SAMPLE_HEADS)
            o_h = so_ref[p, :, hh * HG_DV:(hh + 1) * HG_DV]
            y_hg.append((_rms(o_h, og) * sgate_ref[:, _head(h)]).astype(BF16))
        y_cv = _silu(_layernorm(sdw_ref[...], lng_ref[...], lnb_ref[...])).astype(BF16)
        ycat = jnp.concatenate(y_hg + [y_cv], axis=-1)
        hs = sx_ref[...] + jnp.dot(ycat, wout_ref[...], preferred_element_type=F32)
        ys_ref[...] = _mlp(hs, n2g_ref[...], wup_ref, wdn_ref, fg_ref[...])


def _weight_slots(stage_ref):
    per_row = stage_ref.shape[1] // W_COLS
    n_slot = (stage_ref.shape[0] // W_ROWS) * per_row

    def slot(k):
        r, c = divmod(k, per_row)
        return stage_ref.at[pl.ds(r * W_ROWS, W_ROWS), pl.ds(c * W_COLS, W_COLS)]

    return n_slot, slot


def _load_weights_as_bf16(pairs, stage_ref, sem):
    n_slot, slot = _weight_slots(stage_ref)
    blocks = [(src, dst, r, c) for src, dst in pairs
              for r in range(0, dst.shape[0], W_ROWS) for c in range(0, dst.shape[1], W_COLS)]

    def copy(i):
        src, _, r, c = blocks[i]
        k = i % n_slot
        return pltpu.make_async_copy(src.at[0, pl.ds(r, W_ROWS), pl.ds(c, W_COLS)], slot(k), sem.at[k])

    for i in range(min(n_slot, len(blocks))):
        copy(i).start()
    for i, (_, dst, r, c) in enumerate(blocks):
        copy(i).wait()
        dst[r:r + W_ROWS, c:c + W_COLS] = slot(i % n_slot)[...].astype(BF16)
        if i + n_slot < len(blocks):
            copy(i + n_slot).start()


def _sample_units(s, nseq):
    n_pairs = HG_HEADS // SAMPLE_HEADS
    unit = jnp.minimum(s, nseq // SAMPLE_SEQS * n_pairs - 1)
    return unit // n_pairs, unit % n_pairs, jnp.minimum(s, nseq // SAMPLE_SEQS - 1)


def _sample_state_update(s, sst_ref, ssc_ref, snst_ref, snsc_ref, cw_ref, cb, sft_ref, sqt_ref, sv_ref, sglu_ref,
                         so_ref, sdw_ref):
    nseq = sft_ref.shape[2]
    sb, hp, cblk = _sample_units(s, nseq)
    r0 = pl.multiple_of(sb * SAMPLE_SEQS, SAMPLE_SEQS)
    shift = lax.rem(nseq - r0, nseq)
    v_rows = sv_ref[hp, pl.ds(r0, SAMPLE_SEQS), :]
    seq_row = lax.broadcasted_iota(jnp.int32, (SAMPLE_SEQS, HG_DV), 0)
    for hh in range(SAMPLE_HEADS):
        h = hp * SAMPLE_HEADS + hh
        f_t = pltpu.roll(sft_ref[h], shift, 1)
        q_t = pltpu.roll(sqt_ref[h], shift, 1)
        o = jnp.zeros((SAMPLE_SEQS, HG_DV), F32)
        for j in range(SAMPLE_SEQS):
            f_b = jnp.broadcast_to(f_t[:, j:j + 1], (HG_DK, HG_DV))
            s_new = f_b * sst_ref[j, hh] + (1.0 - f_b) * v_rows[j:j + 1, hh * HG_DV:(hh + 1) * HG_DV]
            snst_ref[j, hh] = s_new
            q_b = jnp.broadcast_to(q_t[:, j:j + 1], (HG_DK, HG_DV))
            o = jnp.where(seq_row == j, jnp.sum(q_b * s_new, axis=0, keepdims=True), o)
        so_ref[hp, pl.ds(r0, SAMPLE_SEQS), hh * HG_DV:(hh + 1) * HG_DV] = o
    c0 = pl.multiple_of(cblk * SAMPLE_SEQS, SAMPLE_SEQS)
    glu = sglu_ref[pl.ds(c0, SAMPLE_SEQS), :]
    taps = CONV_K - 1
    dw = cb + cw_ref[taps:taps + 1, :] * glu
    for j in range(taps):
        dw = dw + cw_ref[j:j + 1, :] * ssc_ref[j]
    sdw_ref[pl.ds(c0, SAMPLE_SEQS), :] = dw
    snsc_ref[0:taps - 1] = ssc_ref[1:taps]
    snsc_ref[taps - 1] = glu


def _const_spec(shape, single=False):
    zeros = (0,) * len(shape)
    if single:
        return pl.BlockSpec(shape, lambda *_: zeros, pipeline_mode=pl.Buffered(1))
    return pl.BlockSpec(shape, lambda *_: zeros)


def _layer(x, meta, lbp, n1g, win, og, cw, cb, lng, lnb, wout, n2g, wup, wdn, fg, xs, sst, ssc):
    bsz, seq, d = x.shape
    nt = seq // TILE_T
    ns = bsz * nt
    nseq = xs.shape[0]
    n_pairs = HG_HEADS // SAMPLE_HEADS
    assert nseq // SAMPLE_SEQS * n_pairs <= ns, "one (sequence block, head pair) per grid step"

    def mix_tile(s):
        s1 = jnp.minimum(s, ns - 1)
        return s1 // nt, s1 % nt

    def mlp_tile(s):
        s2 = jnp.maximum(s - 1, 0)
        return s2 // nt, s2 % nt

    state_spec = pl.BlockSpec((SAMPLE_SEQS, SAMPLE_HEADS, HG_DK, HG_DV),
                              lambda s: (*_sample_units(s, nseq)[:2], 0, 0))
    conv_spec = pl.BlockSpec((CONV_K - 1, SAMPLE_SEQS, CV_WIDTH), lambda s: (0, _sample_units(s, nseq)[2], 0))
    pair = SAMPLE_HEADS * HG_DV
    hbm = pl.BlockSpec(memory_space=pl.ANY)
    n_slot = (TILE_T // W_ROWS) * (C_END // W_COLS)

    return pl.pallas_call(
        functools.partial(_layer_kernel, tile_t=TILE_T, nt=nt),
        grid=(ns + 1,),
        in_specs=[
            pl.BlockSpec((1, TILE_T, d), lambda s: (*mix_tile(s), 0)),
            _const_spec(meta.shape), _const_spec(lbp.shape), _const_spec(n1g.shape),
            hbm, _const_spec(og.shape), _const_spec(cw.shape),
            _const_spec(cb.shape), _const_spec(lng.shape), _const_spec(lnb.shape),
            hbm, _const_spec(n2g.shape), hbm, hbm, _const_spec(fg.shape),
            _const_spec(xs.shape, single=True), state_spec, conv_spec,
        ],
        out_specs=[
            pl.BlockSpec((1, TILE_T, d), lambda s: (*mlp_tile(s), 0)),
            pl.BlockSpec((1, HG_HEADS, HG_DK, HG_DV), lambda s: (mix_tile(s)[0], 0, 0, 0)),
            pl.BlockSpec((1, CONV_K - 1, CV_WIDTH), lambda s: (mix_tile(s)[0], 0, 0)),
            state_spec, conv_spec, _const_spec(xs.shape),
        ],
        out_shape=[
            jax.ShapeDtypeStruct((bsz, seq, d), F32),
            jax.ShapeDtypeStruct((bsz, HG_HEADS, HG_DK, HG_DV), F32),
            jax.ShapeDtypeStruct((bsz, CONV_K - 1, CV_WIDTH), F32),
            jax.ShapeDtypeStruct(sst.shape, F32),
            jax.ShapeDtypeStruct(ssc.shape, F32),
            jax.ShapeDtypeStruct(xs.shape, F32),
        ],
        scratch_shapes=[
            pltpu.VMEM(win.shape[1:], BF16), pltpu.VMEM(wout.shape[1:], BF16),
            pltpu.VMEM(wup.shape[1:], BF16), pltpu.VMEM(wdn.shape[1:], BF16),
            pltpu.SemaphoreType.DMA((n_slot,)),
            pltpu.VMEM((TILE_T, C_END), F32),
            pltpu.VMEM((TILE_T, HG_WIDTH), F32),
            pltpu.VMEM((TILE_T, HG_WIDTH + CV_WIDTH), BF16),
            pltpu.VMEM((HG_HEADS, HG_DV, HG_DK), F32),
            pltpu.VMEM((HIST + TILE_T, CV_WIDTH), F32),
            pltpu.VMEM((TILE_T, d), F32),
            pltpu.VMEM((TILE_T, d), BF16),
            pltpu.VMEM((TILE_T, wup.shape[2] // (TILE_T // CHUNK)), BF16),
            pltpu.VMEM((HG_HEADS, HG_DK, nseq), F32),
            pltpu.VMEM((HG_HEADS, HG_DK, nseq), F32),
            pltpu.VMEM((n_pairs, nseq, pair), F32),
            pltpu.VMEM((nseq, CV_WIDTH), F32),
            pltpu.VMEM((nseq, HG_WIDTH), F32),
            pltpu.VMEM((n_pairs, nseq, pair), F32),
            pltpu.VMEM((nseq, CV_WIDTH), F32),
        ],
        compiler_params=pltpu.CompilerParams(dimension_semantics=("arbitrary",), vmem_limit_bytes=VMEM_LIMIT),
        name="layer",
    )(x, meta, lbp, n1g, win, og, cw, cb, lng, lnb, wout, n2g, wup, wdn, fg, xs, sst, ssc)


def kernel(x_prompt, x_sample, state_hgrn, state_conv, meta_tokens, hg_lb, norm1_g, w_in, hg_onorm_g, conv_w, conv_b,
           conv_ln_g, conv_ln_b, w_out, norm2_g, w_up, w_down, final_g):
    assert state_hgrn.shape[0] == 1, "single-layer stack"
    bsz, seq, d = x_prompt.shape
    row = lambda a: a.reshape(1, -1)
    n1g, og, cb, lng, lnb, n2g, fg = (row(norm1_g[0]), row(hg_onorm_g[0]), row(conv_b[0]), row(conv_ln_g[0]),
                                      row(conv_ln_b[0]), row(norm2_g[0]), row(final_g))
    cw = conv_w[0]

    assert x_sample.shape[1] == 1, "one new token per running sequence"
    x_s = x_sample.reshape(x_sample.shape[0], d)
    y_p, s_p, c_p, s_s, c_s, y_s = _layer(x_prompt, meta_tokens, hg_lb, n1g, w_in, og, cw, cb, lng, lnb, w_out, n2g,
                                          w_up, w_down, fg, x_s, state_hgrn[0], jnp.swapaxes(state_conv[0], 0, 1))

    return (y_p, y_s.reshape(x_sample.shape), s_p[None], c_p[None], s_s[None], jnp.swapaxes(c_s, 0, 1)[None])
```

```python
import functools

import jax
import jax.numpy as jnp
from jax import lax
from jax.experimental import pallas as pl
from jax.experimental.pallas import tpu as pltpu

F32 = jnp.float32
BF16 = jnp.bfloat16

N_META = 16
HG_HEADS = 4
HG_DK = 128
HG_DV = 128
HG_WIDTH = HG_HEADS * HG_DK
CV_WIDTH = 512
CONV_K = 31
CHUNK = 64
HG_CHUNK = 128
EPS = 1e-6
C_Q, C_F, C_I, C_G, C_A, C_B, C_END = (0, HG_WIDTH, 2 * HG_WIDTH, 3 * HG_WIDTH, 4 * HG_WIDTH,
                                       4 * HG_WIDTH + CV_WIDTH, 4 * HG_WIDTH + 2 * CV_WIDTH)
HIST = 32
MXU_COLS = 256
CUMSUM_ROWS = 256
W_ROWS, W_COLS = 256, 1024
TILE_T = 512
SAMPLE_SEQS = 8
SAMPLE_HEADS = 2
VMEM_LIMIT = 62 * 1024 * 1024


def _rms(x, g):
    return x * lax.rsqrt(jnp.mean(x * x, axis=-1, keepdims=True) + EPS) * g


def _layernorm(x, g, b):
    xc = x - jnp.mean(x, axis=-1, keepdims=True)
    return xc * lax.rsqrt(jnp.mean(xc * xc, axis=-1, keepdims=True) + EPS) * g + b


def _silu(x):
    return x * jax.nn.sigmoid(x)


def _lower_bound(lbp):
    e = jnp.exp(lbp - jnp.max(lbp, axis=0, keepdims=True))
    return e[0:1] / jnp.sum(e, axis=0, keepdims=True)


def _project(x, n1g, win_ref):
    hn = _rms(x, n1g).astype(BF16)
    return jnp.dot(hn, win_ref[...], preferred_element_type=F32)


def _tril(n):
    r = lax.broadcasted_iota(jnp.int32, (n, n), 0)
    c = lax.broadcasted_iota(jnp.int32, (n, n), 1)
    return r >= c


def _cumsum_rows(x):
    n = x.shape[0]
    return jnp.dot(_tril(n).astype(F32), x, precision=lax.Precision.HIGHEST, preferred_element_type=F32)


def _chunk_cumsum(x):
    n = x.shape[0]
    r = lax.broadcasted_iota(jnp.int32, (n, n), 0)
    c = lax.broadcasted_iota(jnp.int32, (n, n), 1)
    tri = ((r >= c) & (r // HG_CHUNK == c // HG_CHUNK)).astype(BF16)
    hi = x.astype(BF16)
    rest = x - hi.astype(F32)
    mid = rest.astype(BF16)
    lo = (rest - mid.astype(F32)).astype(BF16)
    return (jnp.dot(tri, hi, preferred_element_type=F32) + jnp.dot(tri, mid, preferred_element_type=F32)
            + jnp.dot(tri, lo, preferred_element_type=F32))


def _head(h):
    return slice(h * HG_DK, (h + 1) * HG_DK)


def _dot_nt(a, b):
    return lax.dot_general(a, b, (((1,), (1,)), ((), ())), preferred_element_type=F32)


def _dot_tn(a, b):
    return lax.dot_general(a, b, (((0,), (0,)), ((), ())), preferred_element_type=F32)


def _mlp(h, n2g, wup_ref, wdn_ref, fg):
    hn = _rms(h, n2g).astype(BF16)
    u = jnp.dot(hn, wup_ref[...], preferred_element_type=F32)
    a = jnp.square(jnp.maximum(u, 0.0)).astype(BF16)
    h2 = h + jnp.dot(a, wdn_ref[...], preferred_element_type=F32)
    return _rms(h2, fg)


def _dwconv_lane_group(e_ref, cw_ref, bias, base, rows, g):
    sub = 8
    lanes = slice(g * 128, (g + 1) * 128)
    ext = e_ref[base - HIST:base + rows, lanes]
    acc = None
    for r in range(sub):
        u = None
        for a in range(-(-CONV_K // sub)):
            d = sub * a + r
            if d < CONV_K:
                lo = HIST - sub - sub * a
                term = cw_ref[CONV_K - 1 - d:CONV_K - d, lanes] * ext[lo:lo + rows + sub]
                u = term if u is None else u + term
        shifted = u[sub:] if r == 0 else pltpu.roll(u, r, 0)[sub:]
        acc = shifted if acc is None else acc + shifted
    return acc + bias[:, lanes]


def _layer_kernel(x_ref, meta_ref, lbp_ref, n1g_ref, win_hbm, og_ref, cw_ref, cb_ref, lng_ref, lnb_ref,
                  wout_hbm, n2g_ref, wup_hbm, wdn_hbm, fg_ref, sx_ref, sst_ref, ssc_ref,
                  y_out_ref, sfin_ref, cfin_ref, snst_ref, snsc_ref, ys_ref,
                  win_ref, wout_ref, wup_ref, wdn_ref, wsem,
                  z_ref, b_ref, y_ref, st_ref, e_ref, h_ref, hn_ref, a_ref,
                  sft_ref, sqt_ref, sv_ref, sglu_ref, sgate_ref, so_ref, sdw_ref, *, tile_t, nt):
    s = pl.program_id(0)
    ns = pl.num_programs(0) - 1
    mixing = s < ns
    t = lax.rem(jnp.minimum(s, ns - 1), nt)
    lb = _lower_bound(lbp_ref[...])
    n1g = n1g_ref[...]

    @pl.when(s == 0)
    def _first_step():
        _load_weights_as_bf16(((win_hbm, win_ref), (wout_hbm, wout_ref), (wup_hbm, wup_ref), (wdn_hbm, wdn_ref)),
                              z_ref, wsem)
        h_ref[...] = jnp.zeros(h_ref.shape, F32)
        zs = _project(sx_ref[...], n1g, win_ref)
        fs = lb + (1.0 - lb) * jax.nn.sigmoid(zs[:, C_F:C_I])
        qs = _silu(zs[:, C_Q:C_F])
        for h in range(HG_HEADS):
            sft_ref[h] = fs[:, _head(h)].T
            sqt_ref[h] = qs[:, _head(h)].T
        pair = SAMPLE_HEADS * HG_DV
        for p in range(HG_HEADS // SAMPLE_HEADS):
            sv_ref[p] = zs[:, C_I + p * pair:C_I + (p + 1) * pair]
        sglu_ref[...] = zs[:, C_A:C_B] * jax.nn.sigmoid(zs[:, C_B:C_END])
        sgate_ref[...] = _silu(zs[:, C_G:C_A])

    @pl.when(mixing & (t == 0))
    def _meta_tokens():
        zm = _project(meta_ref[...], n1g, win_ref)
        f = lb + (1.0 - lb) * jax.nn.sigmoid(zm[:, C_F:C_I])
        b = _cumsum_rows(jnp.log(f))
        k_end = ((1.0 - f) * jnp.exp(b[N_META - 1:N_META] - b)).astype(BF16)
        v = zm[:, C_I:C_G].astype(BF16)
        for h in range(HG_HEADS):
            st_ref[h] = _dot_tn(v[:, _head(h)], k_end[:, _head(h)])
        e_ref[0:HIST - N_META, :] = jnp.zeros((HIST - N_META, CV_WIDTH), F32)
        e_ref[HIST - N_META:HIST, :] = zm[:, C_A:C_B] * jax.nn.sigmoid(zm[:, C_B:C_END])

    cb = cb_ref[...]
    z_ref[...] = _project(x_ref[0], n1g, win_ref)
    _sample_state_update(s, sst_ref, ssc_ref, snst_ref, snsc_ref, cw_ref, cb, sft_ref, sqt_ref, sv_ref, sglu_ref,
                         so_ref, sdw_ref)
    hn_ref[...] = _rms(h_ref[...], n2g_ref[...]).astype(BF16)
    y_out_ref[0] = h_ref[...]
    for r0 in range(0, tile_t, CUMSUM_ROWS):
        rows = slice(r0, r0 + CUMSUM_ROWS)
        f = lb + (1.0 - lb) * jax.nn.sigmoid(z_ref[rows, C_F:C_I])
        z_ref[rows, C_F:C_I] = f
        b_ref[rows, :] = _chunk_cumsum(jnp.log(f))
    causal = _tril(HG_CHUNK)
    og = og_ref[...]
    ff = wup_ref.shape[1] // (tile_t // CHUNK)
    dt = y_out_ref.shape[2] // (CV_WIDTH // 128)

    for c in range(tile_t // HG_CHUNK):
        rows = slice(c * HG_CHUNK, (c + 1) * HG_CHUNK)
        f = z_ref[rows, C_F:C_I]
        k = 1.0 - f
        b = b_ref[rows, :]
        b_mid = b[HG_CHUNK // 2 - 1:HG_CHUNK // 2]
        b_end = b[HG_CHUNK - 1:HG_CHUNK]
        q = _silu(z_ref[rows, C_Q:C_F])
        q_rel = (q * jnp.exp(b - b_mid)).astype(BF16)
        k_rel = (k * jnp.exp(b_mid - b)).astype(BF16)
        q_abs = (q * jnp.exp(b)).astype(BF16)
        k_end = (k * jnp.exp(b_end - b)).astype(BF16)
        decay = jnp.exp(b_end)
        v = z_ref[rows, C_I:C_G].astype(BF16)
        gate = _silu(z_ref[rows, C_G:C_A])
        for h in range(HG_HEADS):
            hs = _head(h)
            scores = jnp.where(causal, _dot_nt(q_rel[:, hs], k_rel[:, hs]), 0.0)
            s_t = st_ref[h]
            o = (jnp.dot(scores.astype(BF16), v[:, hs], preferred_element_type=F32)
                 + _dot_nt(q_abs[:, hs], s_t.astype(BF16)))
            st_ref[h] = s_t * decay[:, hs] + _dot_tn(v[:, hs], k_end[:, hs])
            y_ref[rows, hs] = (_rms(o, og) * gate[:, hs]).astype(BF16)
        for blk in range(c * (HG_CHUNK // CHUNK), (c + 1) * (HG_CHUNK // CHUNK)):
            brows = slice(blk * CHUNK, (blk + 1) * CHUNK)
            for n in range(ff // MXU_COLS):
                ucols = slice(blk * ff + n * MXU_COLS, blk * ff + (n + 1) * MXU_COLS)
                u = jnp.dot(hn_ref[...], wup_ref[:, ucols], preferred_element_type=F32)
                a_ref[:, n * MXU_COLS:(n + 1) * MXU_COLS] = jnp.square(jnp.maximum(u, 0.0)).astype(BF16)
            e_ref[HIST + blk * CHUNK:HIST + (blk + 1) * CHUNK, :] = (z_ref[brows, C_A:C_B]
                                                                     * jax.nn.sigmoid(z_ref[brows, C_B:C_END]))
            dw = []
            for g in range(CV_WIDTH // 128):
                dw.append(_dwconv_lane_group(e_ref, cw_ref, cb, HIST + blk * CHUNK, CHUNK, g))
                dcols = slice(g * dt, (g + 1) * dt)
                y_out_ref[0, :, dcols] += jnp.dot(a_ref[...], wdn_ref[blk * ff:(blk + 1) * ff, dcols],
                                                  preferred_element_type=F32)
            dw = jnp.concatenate(dw, axis=-1)
            y_ref[brows, HG_WIDTH:] = _silu(_layernorm(dw, lng_ref[...], lnb_ref[...])).astype(BF16)

    y_out_ref[0] = _rms(y_out_ref[0], fg_ref[...])
    h_ref[...] = x_ref[0] + jnp.dot(y_ref[...], wout_ref[...], preferred_element_type=F32)
    e_ref[0:HIST, :] = e_ref[tile_t:tile_t + HIST, :]

    @pl.when(mixing & (t == nt - 1))
    def _final_state():
        for h in range(HG_HEADS):
            sfin_ref[0, h] = st_ref[h].T
        cfin_ref[0] = e_ref[tile_t + HIST - (CONV_K - 1):tile_t + HIST, :]

    @pl.when(s == ns)
    def _sample_outputs():
        y_hg = []
        for h in range(HG_HEADS):
            p, hh = divmod(h, SAMPLE_HEADS)
            o_h = so_ref[p, :, hh * HG_DV:(hh + 1) * HG_DV]
            y_hg.append((_rms(o_h, og) * sgate_ref[:, _head(h)]).astype(BF16))
        y_cv = _silu(_layernorm(sdw_ref[...], lng_ref[...], lnb_ref[...])).astype(BF16)
        ycat = jnp.concatenate(y_hg + [y_cv], axis=-1)
        hs = sx_ref[...] + jnp.dot(ycat, wout_ref[...], preferred_element_type=F32)
        ys_ref[...] = _mlp(hs, n2g_ref[...], wup_ref, wdn_ref, fg_ref[...])


def _weight_slots(stage_ref):
    per_row = stage_ref.shape[1] // W_COLS
    n_slot = (stage_ref.shape[0] // W_ROWS) * per_row

    def slot(k):
        r, c = divmod(k, per_row)
        return stage_ref.at[pl.ds(r * W_ROWS, W_ROWS), pl.ds(c * W_COLS, W_COLS)]

    return n_slot, slot


def _load_weights_as_bf16(pairs, stage_ref, sem):
    n_slot, slot = _weight_slots(stage_ref)
    blocks = [(src, dst, r, c) for src, dst in pairs
              for r in range(0, dst.shape[0], W_ROWS) for c in range(0, dst.shape[1], W_COLS)]

    def copy(i):
        src, _, r, c = blocks[i]
        k = i % n_slot
        return pltpu.make_async_copy(src.at[0, pl.ds(r, W_ROWS), pl.ds(c, W_COLS)], slot(k), sem.at[k])

    for i in range(min(n_slot, len(blocks))):
        copy(i).start()
    for i, (_, dst, r, c) in enumerate(blocks):
        copy(i).wait()
        dst[r:r + W_ROWS, c:c + W_COLS] = slot(i % n_slot)[...].astype(BF16)
        if i + n_slot < len(blocks):
            copy(i + n_slot).start()


def _sample_units(s, nseq):
    n_pairs = HG_HEADS // SAMPLE_HEADS
    unit = jnp.minimum(s, nseq // SAMPLE_SEQS * n_pairs - 1)
    return unit // n_pairs, unit % n_pairs, jnp.minimum(s, nseq // SAMPLE_SEQS - 1)


def _sample_state_update(s, sst_ref, ssc_ref, snst_ref, snsc_ref, cw_ref, cb, sft_ref, sqt_ref, sv_ref, sglu_ref,
                         so_ref, sdw_ref):
    nseq = sft_ref.shape[2]
    sb, hp, cblk = _sample_units(s, nseq)
    r0 = pl.multiple_of(sb * SAMPLE_SEQS, SAMPLE_SEQS)
    shift = lax.rem(nseq - r0, nseq)
    v_rows = sv_ref[hp, pl.ds(r0, SAMPLE_SEQS), :]
    seq_row = lax.broadcasted_iota(jnp.int32, (SAMPLE_SEQS, HG_DV), 0)
    for hh in range(SAMPLE_HEADS):
        h = hp * SAMPLE_HEADS + hh
        f_t = pltpu.roll(sft_ref[h], shift, 1)
        q_t = pltpu.roll(sqt_ref[h], shift, 1)
        o = jnp.zeros((SAMPLE_SEQS, HG_DV), F32)
        for j in range(SAMPLE_SEQS):
            f_b = jnp.broadcast_to(f_t[:, j:j + 1], (HG_DK, HG_DV))
            s_new = f_b * sst_ref[j, hh] + (1.0 - f_b) * v_rows[j:j + 1, hh * HG_DV:(hh + 1) * HG_DV]
            snst_ref[j, hh] = s_new
            q_b = jnp.broadcast_to(q_t[:, j:j + 1], (HG_DK, HG_DV))
            o = jnp.where(seq_row == j, jnp.sum(q_b * s_new, axis=0, keepdims=True), o)
        so_ref[hp, pl.ds(r0, SAMPLE_SEQS), hh * HG_DV:(hh + 1) * HG_DV] = o
    c0 = pl.multiple_of(cblk * SAMPLE_SEQS, SAMPLE_SEQS)
    glu = sglu_ref[pl.ds(c0, SAMPLE_SEQS), :]
    taps = CONV_K - 1
    dw = cb + cw_ref[taps:taps + 1, :] * glu
    for j in range(taps):
        dw = dw + cw_ref[j:j + 1, :] * ssc_ref[j]
    sdw_ref[pl.ds(c0, SAMPLE_SEQS), :] = dw
    snsc_ref[0:taps - 1] = ssc_ref[1:taps]
    snsc_ref[taps - 1] = glu


def _const_spec(shape, single=False):
    zeros = (0,) * len(shape)
    if single:
        return pl.BlockSpec(shape, lambda *_: zeros, pipeline_mode=pl.Buffered(1))
    return pl.BlockSpec(shape, lambda *_: zeros)


def _layer(x, meta, lbp, n1g, win, og, cw, cb, lng, lnb, wout, n2g, wup, wdn, fg, xs, sst, ssc):
    bsz, seq, d = x.shape
    nt = seq // TILE_T
    ns = bsz * nt
    nseq = xs.shape[0]
    n_pairs = HG_HEADS // SAMPLE_HEADS
    assert nseq // SAMPLE_SEQS * n_pairs <= ns, "one (sequence block, head pair) per grid step"

    def mix_tile(s):
        s1 = jnp.minimum(s, ns - 1)
        return s1 // nt, s1 % nt

    def mlp_tile(s):
        s2 = jnp.maximum(s - 1, 0)
        return s2 // nt, s2 % nt

    state_spec = pl.BlockSpec((SAMPLE_SEQS, SAMPLE_HEADS, HG_DK, HG_DV),
                              lambda s: (*_sample_units(s, nseq)[:2], 0, 0))
    conv_spec = pl.BlockSpec((CONV_K - 1, SAMPLE_SEQS, CV_WIDTH), lambda s: (0, _sample_units(s, nseq)[2], 0))
    pair = SAMPLE_HEADS * HG_DV
    hbm = pl.BlockSpec(memory_space=pl.ANY)
    n_slot = (TILE_T // W_ROWS) * (C_END // W_COLS)

    return pl.pallas_call(
        functools.partial(_layer_kernel, tile_t=TILE_T, nt=nt),
        grid=(ns + 1,),
        in_specs=[
            pl.BlockSpec((1, TILE_T, d), lambda s: (*mix_tile(s), 0)),
            _const_spec(meta.shape), _const_spec(lbp.shape), _const_spec(n1g.shape),
            hbm, _const_spec(og.shape), _const_spec(cw.shape),
            _const_spec(cb.shape), _const_spec(lng.shape), _const_spec(lnb.shape),
            hbm, _const_spec(n2g.shape), hbm, hbm, _const_spec(fg.shape),
            _const_spec(xs.shape, single=True), state_spec, conv_spec,
        ],
        out_specs=[
            pl.BlockSpec((1, TILE_T, d), lambda s: (*mlp_tile(s), 0)),
            pl.BlockSpec((1, HG_HEADS, HG_DK, HG_DV), lambda s: (mix_tile(s)[0], 0, 0, 0)),
            pl.BlockSpec((1, CONV_K - 1, CV_WIDTH), lambda s: (mix_tile(s)[0], 0, 0)),
            state_spec, conv_spec, _const_spec(xs.shape),
        ],
        out_shape=[
            jax.ShapeDtypeStruct((bsz, seq, d), F32),
            jax.ShapeDtypeStruct((bsz, HG_HEADS, HG_DK, HG_DV), F32),
            jax.ShapeDtypeStruct((bsz, CONV_K - 1, CV_WIDTH), F32),
            jax.ShapeDtypeStruct(sst.shape, F32),
            jax.ShapeDtypeStruct(ssc.shape, F32),
            jax.ShapeDtypeStruct(xs.shape, F32),
        ],
        scratch_shapes=[
            pltpu.VMEM(win.shape[1:], BF16), pltpu.VMEM(wout.shape[1:], BF16),
            pltpu.VMEM(wup.shape[1:], BF16), pltpu.VMEM(wdn.shape[1:], BF16),
            pltpu.SemaphoreType.DMA((n_slot,)),
            pltpu.VMEM((TILE_T, C_END), F32),
            pltpu.VMEM((TILE_T, HG_WIDTH), F32),
            pltpu.VMEM((TILE_T, HG_WIDTH + CV_WIDTH), BF16),
            pltpu.VMEM((HG_HEADS, HG_DV, HG_DK), F32),
            pltpu.VMEM((HIST + TILE_T, CV_WIDTH), F32),
            pltpu.VMEM((TILE_T, d), F32),
            pltpu.VMEM((TILE_T, d), BF16),
            pltpu.VMEM((TILE_T, wup.shape[2] // (TILE_T // CHUNK)), BF16),
            pltpu.VMEM((HG_HEADS, HG_DK, nseq), F32),
            pltpu.VMEM((HG_HEADS, HG_DK, nseq), F32),
            pltpu.VMEM((n_pairs, nseq, pair), F32),
            pltpu.VMEM((nseq, CV_WIDTH), F32),
            pltpu.VMEM((nseq, HG_WIDTH), F32),
            pltpu.VMEM((n_pairs, nseq, pair), F32),
            pltpu.VMEM((nseq, CV_WIDTH), F32),
        ],
        compiler_params=pltpu.CompilerParams(dimension_semantics=("arbitrary",), vmem_limit_bytes=VMEM_LIMIT),
        name="layer",
    )(x, meta, lbp, n1g, win, og, cw, cb, lng, lnb, wout, n2g, wup, wdn, fg, xs, sst, ssc)


def kernel(x_prompt, x_sample, state_hgrn, state_conv, meta_tokens, hg_lb, norm1_g, w_in, hg_onorm_g, conv_w, conv_b,
           conv_ln_g, conv_ln_b, w_out, norm2_g, w_up, w_down, final_g):
    assert state_hgrn.shape[0] == 1, "single-layer stack"
    bsz, seq, d = x_prompt.shape
    row = lambda a: a.reshape(1, -1)
    n1g, og, cb, lng, lnb, n2g, fg = (row(norm1_g[0]), row(hg_onorm_g[0]), row(conv_b[0]), row(conv_ln_g[0]),
                                      row(conv_ln_b[0]), row(norm2_g[0]), row(final_g))
    cw = conv_w[0]

    assert x_sample.shape[1] == 1, "one new token per running sequence"
    x_s = x_sample.reshape(x_sample.shape[0], d)
    y_p, s_p, c_p, s_s, c_s, y_s = _layer(x_prompt, meta_tokens, hg_lb, n1g, w_in, og, cw, cb, lng, lnb, w_out, n2g,
                                          w_up, w_down, fg, x_s, state_hgrn[0], jnp.swapaxes(state_conv[0], 0, 1))

    return (y_p, y_s.reshape(x_sample.shape), s_p[None], c_p[None], s_s[None], jnp.swapaxes(c_s, 0, 1)[None])
```

```python
import functools

import jax
import jax.numpy as jnp
from jax import lax
from jax.experimental import pallas as pl
from jax.experimental.pallas import tpu as pltpu

F32 = jnp.float32
BF16 = jnp.bfloat16

N_META = 16
HG_HEADS = 4
HG_DK = 128
HG_DV = 128
HG_WIDTH = HG_HEADS * HG_DK
CV_WIDTH = 512
CONV_K = 31
CHUNK = 64
HG_CHUNK = 128
EPS = 1e-6
C_Q, C_F, C_I, C_G, C_A, C_B, C_END = (0, HG_WIDTH, 2 * HG_WIDTH, 3 * HG_WIDTH, 4 * HG_WIDTH,
                                       4 * HG_WIDTH + CV_WIDTH, 4 * HG_WIDTH + 2 * CV_WIDTH)
HIST = 32
MXU_COLS = 256
CUMSUM_ROWS = 256
W_ROWS, W_COLS = 256, 1024
TILE_T = 512
SAMPLE_SEQS = 8
SAMPLE_HEADS = 2
VMEM_LIMIT = 62 * 1024 * 1024


def _rms(x, g):
    return x * lax.rsqrt(jnp.mean(x * x, axis=-1, keepdims=True) + EPS) * g


def _layernorm(x, g, b):
    xc = x - jnp.mean(x, axis=-1, keepdims=True)
    return xc * lax.rsqrt(jnp.mean(xc * xc, axis=-1, keepdims=True) + EPS) * g + b


def _silu(x):
    return x * jax.nn.sigmoid(x)


def _lower_bound(lbp):
    e = jnp.exp(lbp - jnp.max(lbp, axis=0, keepdims=True))
    return e[0:1] / jnp.sum(e, axis=0, keepdims=True)


def _project(x, n1g, win_ref):
    hn = _rms(x, n1g).astype(BF16)
    return jnp.dot(hn, win_ref[...], preferred_element_type=F32)


def _tril(n):
    r = lax.broadcasted_iota(jnp.int32, (n, n), 0)
    c = lax.broadcasted_iota(jnp.int32, (n, n), 1)
    return r >= c


def _cumsum_rows(x):
    n = x.shape[0]
    return jnp.dot(_tril(n).astype(F32), x, precision=lax.Precision.HIGHEST, preferred_element_type=F32)


def _chunk_cumsum(x):
    n = x.shape[0]
    r = lax.broadcasted_iota(jnp.int32, (n, n), 0)
    c = lax.broadcasted_iota(jnp.int32, (n, n), 1)
    tri = ((r >= c) & (r // HG_CHUNK == c // HG_CHUNK)).astype(BF16)
    hi = x.astype(BF16)
    rest = x - hi.astype(F32)
    mid = rest.astype(BF16)
    lo = (rest - mid.astype(F32)).astype(BF16)
    return (jnp.dot(tri, hi, preferred_element_type=F32) + jnp.dot(tri, mid, preferred_element_type=F32)
            + jnp.dot(tri, lo, preferred_element_type=F32))


def _head(h):
    return slice(h * HG_DK, (h + 1) * HG_DK)


def _dot_nt(a, b):
    return lax.dot_general(a, b, (((1,), (1,)), ((), ())), preferred_element_type=F32)


def _dot_tn(a, b):
    return lax.dot_general(a, b, (((0,), (0,)), ((), ())), preferred_element_type=F32)


def _mlp(h, n2g, wup_ref, wdn_ref, fg):
    hn = _rms(h, n2g).astype(BF16)
    u = jnp.dot(hn, wup_ref[...], preferred_element_type=F32)
    a = jnp.square(jnp.maximum(u, 0.0)).astype(BF16)
    h2 = h + jnp.dot(a, wdn_ref[...], preferred_element_type=F32)
    return _rms(h2, fg)


def _dwconv_lane_group(e_ref, cw_ref, bias, base, rows, g):
    sub = 8
    lanes = slice(g * 128, (g + 1) * 128)
    ext = e_ref[base - HIST:base + rows, lanes]
    acc = None
    for r in range(sub):
        u = None
        for a in range(-(-CONV_K // sub)):
            d = sub * a + r
            if d < CONV_K:
                lo = HIST - sub - sub * a
                term = cw_ref[CONV_K - 1 - d:CONV_K - d, lanes] * ext[lo:lo + rows + sub]
                u = term if u is None else u + term
        shifted = u[sub:] if r == 0 else pltpu.roll(u, r, 0)[sub:]
        acc = shifted if acc is None else acc + shifted
    return acc + bias[:, lanes]


def _layer_kernel(x_ref, meta_ref, lbp_ref, n1g_ref, win_hbm, og_ref, cw_ref, cb_ref, lng_ref, lnb_ref,
                  wout_hbm, n2g_ref, wup_hbm, wdn_hbm, fg_ref, sx_ref, sst_ref, ssc_ref,
                  y_out_ref, sfin_ref, cfin_ref, snst_ref, snsc_ref, ys_ref,
                  win_ref, wout_ref, wup_ref, wdn_ref, wsem,
                  z_ref, b_ref, y_ref, st_ref, e_ref, h_ref, hn_ref, a_ref,
                  sft_ref, sqt_ref, sv_ref, sglu_ref, sgate_ref, so_ref, sdw_ref, *, tile_t, nt):
    s = pl.program_id(0)
    ns = pl.num_programs(0) - 1
    mixing = s < ns
    t = lax.rem(jnp.minimum(s, ns - 1), nt)
    lb = _lower_bound(lbp_ref[...])
    n1g = n1g_ref[...]

    @pl.when(s == 0)
    def _first_step():
        _load_weights_as_bf16(((win_hbm, win_ref), (wout_hbm, wout_ref), (wup_hbm, wup_ref), (wdn_hbm, wdn_ref)),
                              z_ref, wsem)
        h_ref[...] = jnp.zeros(h_ref.shape, F32)
        zs = _project(sx_ref[...], n1g, win_ref)
        fs = lb + (1.0 - lb) * jax.nn.sigmoid(zs[:, C_F:C_I])
        qs = _silu(zs[:, C_Q:C_F])
        for h in range(HG_HEADS):
            sft_ref[h] = fs[:, _head(h)].T
            sqt_ref[h] = qs[:, _head(h)].T
        pair = SAMPLE_HEADS * HG_DV
        for p in range(HG_HEADS // SAMPLE_HEADS):
            sv_ref[p] = zs[:, C_I + p * pair:C_I + (p + 1) * pair]
        sglu_ref[...] = zs[:, C_A:C_B] * jax.nn.sigmoid(zs[:, C_B:C_END])
        sgate_ref[...] = _silu(zs[:, C_G:C_A])

    @pl.when(mixing & (t == 0))
    def _meta_tokens():
        zm = _project(meta_ref[...], n1g, win_ref)
        f = lb + (1.0 - lb) * jax.nn.sigmoid(zm[:, C_F:C_I])
        b = _cumsum_rows(jnp.log(f))
        k_end = ((1.0 - f) * jnp.exp(b[N_META - 1:N_META] - b)).astype(BF16)
        v = zm[:, C_I:C_G].astype(BF16)
        for h in range(HG_HEADS):
            st_ref[h] = _dot_tn(v[:, _head(h)], k_end[:, _head(h)])
        e_ref[0:HIST - N_META, :] = jnp.zeros((HIST - N_META, CV_WIDTH), F32)
        e_ref[HIST - N_META:HIST, :] = zm[:, C_A:C_B] * jax.nn.sigmoid(zm[:, C_B:C_END])

    cb = cb_ref[...]
    z_ref[...] = _project(x_ref[0], n1g, win_ref)
    _sample_state_update(s, sst_ref, ssc_ref, snst_ref, snsc_ref, cw_ref, cb, sft_ref, sqt_ref, sv_ref, sglu_ref,
                         so_ref, sdw_ref)
    hn_ref[...] = _rms(h_ref[...], n2g_ref[...]).astype(BF16)
    y_out_ref[0] = h_ref[...]
    for r0 in range(0, tile_t, CUMSUM_ROWS):
        rows = slice(r0, r0 + CUMSUM_ROWS)
        f = lb + (1.0 - lb) * jax.nn.sigmoid(z_ref[rows, C_F:C_I])
        z_ref[rows, C_F:C_I] = f
        b_ref[rows, :] = _chunk_cumsum(jnp.log(f))
    causal = _tril(HG_CHUNK)
    og = og_ref[...]
    ff = wup_ref.shape[1] // (tile_t // CHUNK)
    dt = y_out_ref.shape[2] // (CV_WIDTH // 128)

    for c in range(tile_t // HG_CHUNK):
        rows = slice(c * HG_CHUNK, (c + 1) * HG_CHUNK)
        f = z_ref[rows, C_F:C_I]
        k = 1.0 - f
        b = b_ref[rows, :]
        b_mid = b[HG_CHUNK // 2 - 1:HG_CHUNK // 2]
        b_end = b[HG_CHUNK - 1:HG_CHUNK]
        q = _silu(z_ref[rows, C_Q:C_F])
        q_rel = (q * jnp.exp(b - b_mid)).astype(BF16)
        k_rel = (k * jnp.exp(b_mid - b)).astype(BF16)
        q_abs = (q * jnp.exp(b)).astype(BF16)
        k_end = (k * jnp.exp(b_end - b)).astype(BF16)
        decay = jnp.exp(b_end)
        v = z_ref[rows, C_I:C_G].astype(BF16)
        gate = _silu(z_ref[rows, C_G:C_A])
        for h in range(HG_HEADS):
            hs = _head(h)
            scores = jnp.where(causal, _dot_nt(q_rel[:, hs], k_rel[:, hs]), 0.0)
            s_t = st_ref[h]
            o = (jnp.dot(scores.astype(BF16), v[:, hs], preferred_element_type=F32)
                 + _dot_nt(q_abs[:, hs], s_t.astype(BF16)))
            st_ref[h] = s_t * decay[:, hs] + _dot_tn(v[:, hs], k_end[:, hs])
            y_ref[rows, hs] = (_rms(o, og) * gate[:, hs]).astype(BF16)
        for blk in range(c * (HG_CHUNK // CHUNK), (c + 1) * (HG_CHUNK // CHUNK)):
            brows = slice(blk * CHUNK, (blk + 1) * CHUNK)
            u = jnp.dot(hn_ref[...], wup_ref[:, blk * ff:(blk + 1) * ff], preferred_element_type=F32)
            a_ref[...] = jnp.square(jnp.maximum(u, 0.0)).astype(BF16)
            e_ref[HIST + blk * CHUNK:HIST + (blk + 1) * CHUNK, :] = (z_ref[brows, C_A:C_B]
                                                                     * jax.nn.sigmoid(z_ref[brows, C_B:C_END]))
            dw = []
            for g in range(CV_WIDTH // 128):
                dw.append(_dwconv_lane_group(e_ref, cw_ref, cb, HIST + blk * CHUNK, CHUNK, g))
                if g % 2 == 1:
                    dcols = slice((g // 2) * 2 * dt, (g // 2 + 1) * 2 * dt)
                    y_out_ref[0, :, dcols] += jnp.dot(a_ref[...], wdn_ref[blk * ff:(blk + 1) * ff, dcols],
                                                      preferred_element_type=F32)
            dw = jnp.concatenate(dw, axis=-1)
            y_ref[brows, HG_WIDTH:] = _silu(_layernorm(dw, lng_ref[...], lnb_ref[...])).astype(BF16)

    y_out_ref[0] = _rms(y_out_ref[0], fg_ref[...])
    h_ref[...] = x_ref[0] + jnp.dot(y_ref[...], wout_ref[...], preferred_element_type=F32)
    e_ref[0:HIST, :] = e_ref[tile_t:tile_t + HIST, :]

    @pl.when(mixing & (t == nt - 1))
    def _final_state():
        for h in range(HG_HEADS):
            sfin_ref[0, h] = st_ref[h].T
        cfin_ref[0] = e_ref[tile_t + HIST - (CONV_K - 1):tile_t + HIST, :]

    @pl.when(s == ns)
    def _sample_outputs():
        y_hg = []
        for h in range(HG_HEADS):
            p, hh = divmod(h, SAMPLE_HEADS)
            o_h = so_ref[p, :, hh * HG_DV:(hh + 1) * HG_DV]
            y_hg.append((_rms(o_h, og) * sgate_ref[:, _head(h)]).astype(BF16))
        y_cv = _silu(_layernorm(sdw_ref[...], lng_ref[...], lnb_ref[...])).astype(BF16)
        ycat = jnp.concatenate(y_hg + [y_cv], axis=-1)
        hs = sx_ref[...] + jnp.dot(ycat, wout_ref[...], preferred_element_type=F32)
        ys_ref[...] = _mlp(hs, n2g_ref[...], wup_ref, wdn_ref, fg_ref[...])


def _weight_slots(stage_ref):
    per_row = stage_ref.shape[1] // W_COLS
    n_slot = (stage_ref.shape[0] // W_ROWS) * per_row

    def slot(k):
        r, c = divmod(k, per_row)
        return stage_ref.at[pl.ds(r * W_ROWS, W_ROWS), pl.ds(c * W_COLS, W_COLS)]

    return n_slot, slot


def _load_weights_as_bf16(pairs, stage_ref, sem):
    n_slot, slot = _weight_slots(stage_ref)
    blocks = [(src, dst, r, c) for src, dst in pairs
              for r in range(0, dst.shape[0], W_ROWS) for c in range(0, dst.shape[1], W_COLS)]

    def copy(i):
        src, _, r, c = blocks[i]
        k = i % n_slot
        return pltpu.make_async_copy(src.at[0, pl.ds(r, W_ROWS), pl.ds(c, W_COLS)], slot(k), sem.at[k])

    for i in range(min(n_slot, len(blocks))):
        copy(i).start()
    for i, (_, dst, r, c) in enumerate(blocks):
        copy(i).wait()
        dst[r:r + W_ROWS, c:c + W_COLS] = slot(i % n_slot)[...].astype(BF16)
        if i + n_slot < len(blocks):
            copy(i + n_slot).start()


def _sample_units(s, nseq):
    n_pairs = HG_HEADS // SAMPLE_HEADS
    unit = jnp.minimum(s, nseq // SAMPLE_SEQS * n_pairs - 1)
    return unit // n_pairs, unit % n_pairs, jnp.minimum(s, nseq // SAMPLE_SEQS - 1)


def _sample_state_update(s, sst_ref, ssc_ref, snst_ref, snsc_ref, cw_ref, cb, sft_ref, sqt_ref, sv_ref, sglu_ref,
                         so_ref, sdw_ref):
    nseq = sft_ref.shape[2]
    sb, hp, cblk = _sample_units(s, nseq)
    r0 = pl.multiple_of(sb * SAMPLE_SEQS, SAMPLE_SEQS)
    shift = lax.rem(nseq - r0, nseq)
    v_rows = sv_ref[hp, pl.ds(r0, SAMPLE_SEQS), :]
    seq_row = lax.broadcasted_iota(jnp.int32, (SAMPLE_SEQS, HG_DV), 0)
    for hh in range(SAMPLE_HEADS):
        h = hp * SAMPLE_HEADS + hh
        f_t = pltpu.roll(sft_ref[h], shift, 1)
        q_t = pltpu.roll(sqt_ref[h], shift, 1)
        o = jnp.zeros((SAMPLE_SEQS, HG_DV), F32)
        for j in range(SAMPLE_SEQS):
            f_b = jnp.broadcast_to(f_t[:, j:j + 1], (HG_DK, HG_DV))
            s_new = f_b * sst_ref[j, hh] + (1.0 - f_b) * v_rows[j:j + 1, hh * HG_DV:(hh + 1) * HG_DV]
            snst_ref[j, hh] = s_new
            q_b = jnp.broadcast_to(q_t[:, j:j + 1], (HG_DK, HG_DV))
            o = jnp.where(seq_row == j, jnp.sum(q_b * s_new, axis=0, keepdims=True), o)
        so_ref[hp, pl.ds(r0, SAMPLE_SEQS), hh * HG_DV:(hh + 1) * HG_DV] = o
    c0 = pl.multiple_of(cblk * SAMPLE_SEQS, SAMPLE_SEQS)
    glu = sglu_ref[pl.ds(c0, SAMPLE_SEQS), :]
    taps = CONV_K - 1
    dw = cb + cw_ref[taps:taps + 1, :] * glu
    for j in range(taps):
        dw = dw + cw_ref[j:j + 1, :] * ssc_ref[j]
    sdw_ref[pl.ds(c0, SAMPLE_SEQS), :] = dw
    snsc_ref[0:taps - 1] = ssc_ref[1:taps]
    snsc_ref[taps - 1] = glu


def _const_spec(shape, single=False):
    zeros = (0,) * len(shape)
    if single:
        return pl.BlockSpec(shape, lambda *_: zeros, pipeline_mode=pl.Buffered(1))
    return pl.BlockSpec(shape, lambda *_: zeros)


def _layer(x, meta, lbp, n1g, win, og, cw, cb, lng, lnb, wout, n2g, wup, wdn, fg, xs, sst, ssc):
    bsz, seq, d = x.shape
    nt = seq // TILE_T
    ns = bsz * nt
    nseq = xs.shape[0]
    n_pairs = HG_HEADS // SAMPLE_HEADS
    assert nseq // SAMPLE_SEQS * n_pairs <= ns, "one (sequence block, head pair) per grid step"

    def mix_tile(s):
        s1 = jnp.minimum(s, ns - 1)
        return s1 // nt, s1 % nt

    def mlp_tile(s):
        s2 = jnp.maximum(s - 1, 0)
        return s2 // nt, s2 % nt

    state_spec = pl.BlockSpec((SAMPLE_SEQS, SAMPLE_HEADS, HG_DK, HG_DV),
                              lambda s: (*_sample_units(s, nseq)[:2], 0, 0))
    conv_spec = pl.BlockSpec((CONV_K - 1, SAMPLE_SEQS, CV_WIDTH), lambda s: (0, _sample_units(s, nseq)[2], 0))
    pair = SAMPLE_HEADS * HG_DV
    hbm = pl.BlockSpec(memory_space=pl.ANY)
    n_slot = (TILE_T // W_ROWS) * (C_END // W_COLS)

    return pl.pallas_call(
        functools.partial(_layer_kernel, tile_t=TILE_T, nt=nt),
        grid=(ns + 1,),
        in_specs=[
            pl.BlockSpec((1, TILE_T, d), lambda s: (*mix_tile(s), 0)),
            _const_spec(meta.shape), _const_spec(lbp.shape), _const_spec(n1g.shape),
            hbm, _const_spec(og.shape), _const_spec(cw.shape),
            _const_spec(cb.shape), _const_spec(lng.shape), _const_spec(lnb.shape),
            hbm, _const_spec(n2g.shape), hbm, hbm, _const_spec(fg.shape),
            _const_spec(xs.shape, single=True), state_spec, conv_spec,
        ],
        out_specs=[
            pl.BlockSpec((1, TILE_T, d), lambda s: (*mlp_tile(s), 0)),
            pl.BlockSpec((1, HG_HEADS, HG_DK, HG_DV), lambda s: (mix_tile(s)[0], 0, 0, 0)),
            pl.BlockSpec((1, CONV_K - 1, CV_WIDTH), lambda s: (mix_tile(s)[0], 0, 0)),
            state_spec, conv_spec, _const_spec(xs.shape),
        ],
        out_shape=[
            jax.ShapeDtypeStruct((bsz, seq, d), F32),
            jax.ShapeDtypeStruct((bsz, HG_HEADS, HG_DK, HG_DV), F32),
            jax.ShapeDtypeStruct((bsz, CONV_K - 1, CV_WIDTH), F32),
            jax.ShapeDtypeStruct(sst.shape, F32),
            jax.ShapeDtypeStruct(ssc.shape, F32),
            jax.ShapeDtypeStruct(xs.shape, F32),
        ],
        scratch_shapes=[
            pltpu.VMEM(win.shape[1:], BF16), pltpu.VMEM(wout.shape[1:], BF16),
            pltpu.VMEM(wup.shape[1:], BF16), pltpu.VMEM(wdn.shape[1:], BF16),
            pltpu.SemaphoreType.DMA((n_slot,)),
            pltpu.VMEM((TILE_T, C_END), F32),
            pltpu.VMEM((TILE_T, HG_WIDTH), F32),
            pltpu.VMEM((TILE_T, HG_WIDTH + CV_WIDTH), BF16),
            pltpu.VMEM((HG_HEADS, HG_DV, HG_DK), F32),
            pltpu.VMEM((HIST + TILE_T, CV_WIDTH), F32),
            pltpu.VMEM((TILE_T, d), F32),
            pltpu.VMEM((TILE_T, d), BF16),
            pltpu.VMEM((TILE_T, wup.shape[2] // (TILE_T // CHUNK)), BF16),
            pltpu.VMEM((HG_HEADS, HG_DK, nseq), F32),
            pltpu.VMEM((HG_HEADS, HG_DK, nseq), F32),
            pltpu.VMEM((n_pairs, nseq, pair), F32),
            pltpu.VMEM((nseq, CV_WIDTH), F32),
            pltpu.VMEM((nseq, HG_WIDTH), F32),
            pltpu.VMEM((n_pairs, nseq, pair), F32),
            pltpu.VMEM((nseq, CV_WIDTH), F32),
        ],
        compiler_params=pltpu.CompilerParams(dimension_semantics=("arbitrary",), vmem_limit_bytes=VMEM_LIMIT),
        name="layer",
    )(x, meta, lbp, n1g, win, og, cw, cb, lng, lnb, wout, n2g, wup, wdn, fg, xs, sst, ssc)


def kernel(x_prompt, x_sample, state_hgrn, state_conv, meta_tokens, hg_lb, norm1_g, w_in, hg_onorm_g, conv_w, conv_b,
           conv_ln_g, conv_ln_b, w_out, norm2_g, w_up, w_down, final_g):
    assert state_hgrn.shape[0] == 1, "single-layer stack"
    bsz, seq, d = x_prompt.shape
    row = lambda a: a.reshape(1, -1)
    n1g, og, cb, lng, lnb, n2g, fg = (row(norm1_g[0]), row(hg_onorm_g[0]), row(conv_b[0]), row(conv_ln_g[0]),
                                      row(conv_ln_b[0]), row(norm2_g[0]), row(final_g))
    cw = conv_w[0]

    assert x_sample.shape[1] == 1, "one new token per running sequence"
    x_s = x_sample.reshape(x_sample.shape[0], d)
    y_p, s_p, c_p, s_s, c_s, y_s = _layer(x_prompt, meta_tokens, hg_lb, n1g, w_in, og, cw, cb, lng, lnb, w_out, n2g,
                                          w_up, w_down, fg, x_s, state_hgrn[0], jnp.swapaxes(state_conv[0], 0, 1))

    return (y_p, y_s.reshape(x_sample.shape), s_p[None], c_p[None], s_s[None], jnp.swapaxes(c_s, 0, 1)[None])
```

```python
import functools

import jax
import jax.numpy as jnp
from jax import lax
from jax.experimental import pallas as pl
from jax.experimental.pallas import tpu as pltpu

F32 = jnp.float32
BF16 = jnp.bfloat16

N_META = 16
HG_HEADS = 4
HG_DK = 128
HG_DV = 128
HG_WIDTH = HG_HEADS * HG_DK
CV_WIDTH = 512
CONV_K = 31
CHUNK = 64
HG_CHUNK = 128
EPS = 1e-6
C_Q, C_F, C_I, C_G, C_A, C_B, C_END = (0, HG_WIDTH, 2 * HG_WIDTH, 3 * HG_WIDTH, 4 * HG_WIDTH,
                                       4 * HG_WIDTH + CV_WIDTH, 4 * HG_WIDTH + 2 * CV_WIDTH)
HIST = 32
MXU_COLS = 256
CUMSUM_ROWS = 256
W_ROWS, W_COLS = 256, 1024
TILE_T = 512
SAMPLE_SEQS = 8
SAMPLE_HEADS = 2
VMEM_LIMIT = 62 * 1024 * 1024


def _rms(x, g):
    return x * lax.rsqrt(jnp.mean(x * x, axis=-1, keepdims=True) + EPS) * g


def _layernorm(x, g, b):
    xc = x - jnp.mean(x, axis=-1, keepdims=True)
    return xc * lax.rsqrt(jnp.mean(xc * xc, axis=-1, keepdims=True) + EPS) * g + b


def _silu(x):
    return x * jax.nn.sigmoid(x)


def _lower_bound(lbp):
    e = jnp.exp(lbp - jnp.max(lbp, axis=0, keepdims=True))
    return e[0:1] / jnp.sum(e, axis=0, keepdims=True)


def _project(x, n1g, win_ref):
    hn = _rms(x, n1g).astype(BF16)
    return jnp.dot(hn, win_ref[...], preferred_element_type=F32)


def _tril(n):
    r = lax.broadcasted_iota(jnp.int32, (n, n), 0)
    c = lax.broadcasted_iota(jnp.int32, (n, n), 1)
    return r >= c


def _cumsum_rows(x):
    n = x.shape[0]
    return jnp.dot(_tril(n).astype(F32), x, precision=lax.Precision.HIGHEST, preferred_element_type=F32)


def _chunk_cumsum(x):
    n = x.shape[0]
    r = lax.broadcasted_iota(jnp.int32, (n, n), 0)
    c = lax.broadcasted_iota(jnp.int32, (n, n), 1)
    tri = ((r >= c) & (r // HG_CHUNK == c // HG_CHUNK)).astype(BF16)
    hi = x.astype(BF16)
    rest = x - hi.astype(F32)
    mid = rest.astype(BF16)
    lo = (rest - mid.astype(F32)).astype(BF16)
    return (jnp.dot(tri, hi, preferred_element_type=F32) + jnp.dot(tri, mid, preferred_element_type=F32)
            + jnp.dot(tri, lo, preferred_element_type=F32))


def _head(h):
    return slice(h * HG_DK, (h + 1) * HG_DK)


def _dot_nt(a, b):
    return lax.dot_general(a, b, (((1,), (1,)), ((), ())), preferred_element_type=F32)


def _dot_tn(a, b):
    return lax.dot_general(a, b, (((0,), (0,)), ((), ())), preferred_element_type=F32)


def _mlp(h, n2g, wup_ref, wdn_ref, fg):
    hn = _rms(h, n2g).astype(BF16)
    u = jnp.dot(hn, wup_ref[...], preferred_element_type=F32)
    a = jnp.square(jnp.maximum(u, 0.0)).astype(BF16)
    h2 = h + jnp.dot(a, wdn_ref[...], preferred_element_type=F32)
    return _rms(h2, fg)


def _dwconv_lane_group(e_ref, cw_ref, bias, base, rows, g):
    sub = 8
    lanes = slice(g * 128, (g + 1) * 128)
    ext = e_ref[base - HIST:base + rows, lanes]
    acc = None
    for r in range(sub):
        u = None
        for a in range(-(-CONV_K // sub)):
            d = sub * a + r
            if d < CONV_K:
                lo = HIST - sub - sub * a
                term = cw_ref[CONV_K - 1 - d:CONV_K - d, lanes] * ext[lo:lo + rows + sub]
                u = term if u is None else u + term
        shifted = u[sub:] if r == 0 else pltpu.roll(u, r, 0)[sub:]
        acc = shifted if acc is None else acc + shifted
    return acc + bias[:, lanes]


def _layer_kernel(x_ref, meta_ref, lbp_ref, n1g_ref, win_hbm, og_ref, cw_ref, cb_ref, lng_ref, lnb_ref,
                  wout_hbm, n2g_ref, wup_hbm, wdn_hbm, fg_ref, sx_ref, sst_ref, ssc_ref,
                  y_out_ref, sfin_ref, cfin_ref, snst_ref, snsc_ref, ys_ref,
                  win_ref, wout_ref, wup_ref, wdn_ref, wsem,
                  z_ref, b_ref, y_ref, st_ref, e_ref, h_ref, hn_ref, a_ref,
                  sft_ref, sqt_ref, sv_ref, sglu_ref, sgate_ref, so_ref, sdw_ref, *, tile_t, nt):
    s = pl.program_id(0)
    ns = pl.num_programs(0) - 1
    mixing = s < ns
    t = lax.rem(jnp.minimum(s, ns - 1), nt)
    lb = _lower_bound(lbp_ref[...])
    n1g = n1g_ref[...]

    @pl.when(s == 0)
    def _first_step():
        _load_weights_as_bf16(((win_hbm, win_ref), (wout_hbm, wout_ref), (wup_hbm, wup_ref), (wdn_hbm, wdn_ref)),
                              z_ref, wsem)
        h_ref[...] = jnp.zeros(h_ref.shape, F32)
        zs = _project(sx_ref[...], n1g, win_ref)
        fs = lb + (1.0 - lb) * jax.nn.sigmoid(zs[:, C_F:C_I])
        qs = _silu(zs[:, C_Q:C_F])
        for h in range(HG_HEADS):
            sft_ref[h] = fs[:, _head(h)].T
            sqt_ref[h] = qs[:, _head(h)].T
        pair = SAMPLE_HEADS * HG_DV
        for p in range(HG_HEADS // SAMPLE_HEADS):
            sv_ref[p] = zs[:, C_I + p * pair:C_I + (p + 1) * pair]
        sglu_ref[...] = zs[:, C_A:C_B] * jax.nn.sigmoid(zs[:, C_B:C_END])
        sgate_ref[...] = _silu(zs[:, C_G:C_A])

    @pl.when(mixing & (t == 0))
    def _meta_tokens():
        zm = _project(meta_ref[...], n1g, win_ref)
        f = lb + (1.0 - lb) * jax.nn.sigmoid(zm[:, C_F:C_I])
        b = _cumsum_rows(jnp.log(f))
        k_end = ((1.0 - f) * jnp.exp(b[N_META - 1:N_META] - b)).astype(BF16)
        v = zm[:, C_I:C_G].astype(BF16)
        for h in range(HG_HEADS):
            st_ref[h] = _dot_tn(v[:, _head(h)], k_end[:, _head(h)])
        e_ref[0:HIST - N_META, :] = jnp.zeros((HIST - N_META, CV_WIDTH), F32)
        e_ref[HIST - N_META:HIST, :] = zm[:, C_A:C_B] * jax.nn.sigmoid(zm[:, C_B:C_END])

    cb = cb_ref[...]
    z_ref[...] = _project(x_ref[0], n1g, win_ref)
    _sample_state_update(s, sst_ref, ssc_ref, snst_ref, snsc_ref, cw_ref, cb, sft_ref, sqt_ref, sv_ref, sglu_ref,
                         so_ref, sdw_ref)
    hn_ref[...] = _rms(h_ref[...], n2g_ref[...]).astype(BF16)
    y_out_ref[0] = h_ref[...]
    for r0 in range(0, tile_t, CUMSUM_ROWS):
        rows = slice(r0, r0 + CUMSUM_ROWS)
        f = lb + (1.0 - lb) * jax.nn.sigmoid(z_ref[rows, C_F:C_I])
        z_ref[rows, C_F:C_I] = f
        b_ref[rows, :] = _chunk_cumsum(jnp.log(f))
    causal = _tril(HG_CHUNK)
    og = og_ref[...]
    ff = wup_ref.shape[1] // (tile_t // CHUNK)
    dt = y_out_ref.shape[2] // (CV_WIDTH // 128)

    for c in range(tile_t // HG_CHUNK):
        rows = slice(c * HG_CHUNK, (c + 1) * HG_CHUNK)
        f = z_ref[rows, C_F:C_I]
        k = 1.0 - f
        b = b_ref[rows, :]
        b_mid = b[HG_CHUNK // 2 - 1:HG_CHUNK // 2]
        b_end = b[HG_CHUNK - 1:HG_CHUNK]
        q = _silu(z_ref[rows, C_Q:C_F])
        q_rel = (q * jnp.exp(b - b_mid)).astype(BF16)
        k_rel = (k * jnp.exp(b_mid - b)).astype(BF16)
        q_abs = (q * jnp.exp(b)).astype(BF16)
        k_end = (k * jnp.exp(b_end - b)).astype(BF16)
        decay = jnp.exp(b_end)
        v = z_ref[rows, C_I:C_G]
        gate = _silu(z_ref[rows, C_G:C_A])
        for h in range(HG_HEADS):
            hs = _head(h)
            scores = jnp.where(causal, _dot_nt(q_rel[:, hs], k_rel[:, hs]), 0.0)
            s_t = st_ref[h]
            v_t = v[:, hs].T.astype(BF16)
            o = _dot_nt(jnp.concatenate([scores.astype(BF16), q_abs[:, hs]], axis=1),
                        jnp.concatenate([v_t, s_t.astype(BF16)], axis=1))
            st_ref[h] = s_t * decay[:, hs] + jnp.dot(v_t, k_end[:, hs], preferred_element_type=F32)
            y_ref[rows, hs] = (_rms(o, og) * gate[:, hs]).astype(BF16)
        for blk in range(c * (HG_CHUNK // CHUNK), (c + 1) * (HG_CHUNK // CHUNK)):
            brows = slice(blk * CHUNK, (blk + 1) * CHUNK)
            u = jnp.dot(hn_ref[...], wup_ref[:, blk * ff:(blk + 1) * ff], preferred_element_type=F32)
            a_ref[...] = jnp.square(jnp.maximum(u, 0.0)).astype(BF16)
            e_ref[HIST + blk * CHUNK:HIST + (blk + 1) * CHUNK, :] = (z_ref[brows, C_A:C_B]
                                                                     * jax.nn.sigmoid(z_ref[brows, C_B:C_END]))
            dw = []
            for g in range(CV_WIDTH // 128):
                dw.append(_dwconv_lane_group(e_ref, cw_ref, cb, HIST + blk * CHUNK, CHUNK, g))
                if g % 2 == 1:
                    dcols = slice((g // 2) * 2 * dt, (g // 2 + 1) * 2 * dt)
                    y_out_ref[0, :, dcols] += jnp.dot(a_ref[...], wdn_ref[blk * ff:(blk + 1) * ff, dcols],
                                                      preferred_element_type=F32)
            dw = jnp.concatenate(dw, axis=-1)
            y_ref[brows, HG_WIDTH:] = _silu(_layernorm(dw, lng_ref[...], lnb_ref[...])).astype(BF16)

    y_out_ref[0] = _rms(y_out_ref[0], fg_ref[...])
    h_ref[...] = x_ref[0] + jnp.dot(y_ref[...], wout_ref[...], preferred_element_type=F32)
    e_ref[0:HIST, :] = e_ref[tile_t:tile_t + HIST, :]

    @pl.when(mixing & (t == nt - 1))
    def _final_state():
        for h in range(HG_HEADS):
            sfin_ref[0, h] = st_ref[h].T
        cfin_ref[0] = e_ref[tile_t + HIST - (CONV_K - 1):tile_t + HIST, :]

    @pl.when(s == ns)
    def _sample_outputs():
        y_hg = []
        for h in range(HG_HEADS):
            p, hh = divmod(h, SAMPLE_HEADS)
            o_h = so_ref[p, :, hh * HG_DV:(hh + 1) * HG_DV]
            y_hg.append((_rms(o_h, og) * sgate_ref[:, _head(h)]).astype(BF16))
        y_cv = _silu(_layernorm(sdw_ref[...], lng_ref[...], lnb_ref[...])).astype(BF16)
        ycat = jnp.concatenate(y_hg + [y_cv], axis=-1)
        hs = sx_ref[...] + jnp.dot(ycat, wout_ref[...], preferred_element_type=F32)
        ys_ref[...] = _mlp(hs, n2g_ref[...], wup_ref, wdn_ref, fg_ref[...])


def _weight_slots(stage_ref):
    per_row = stage_ref.shape[1] // W_COLS
    n_slot = (stage_ref.shape[0] // W_ROWS) * per_row

    def slot(k):
        r, c = divmod(k, per_row)
        return stage_ref.at[pl.ds(r * W_ROWS, W_ROWS), pl.ds(c * W_COLS, W_COLS)]

    return n_slot, slot


def _load_weights_as_bf16(pairs, stage_ref, sem):
    n_slot, slot = _weight_slots(stage_ref)
    blocks = [(src, dst, r, c) for src, dst in pairs
              for r in range(0, dst.shape[0], W_ROWS) for c in range(0, dst.shape[1], W_COLS)]

    def copy(i):
        src, _, r, c = blocks[i]
        k = i % n_slot
        return pltpu.make_async_copy(src.at[0, pl.ds(r, W_ROWS), pl.ds(c, W_COLS)], slot(k), sem.at[k])

    for i in range(min(n_slot, len(blocks))):
        copy(i).start()
    for i, (_, dst, r, c) in enumerate(blocks):
        copy(i).wait()
        dst[r:r + W_ROWS, c:c + W_COLS] = slot(i % n_slot)[...].astype(BF16)
        if i + n_slot < len(blocks):
            copy(i + n_slot).start()


def _sample_units(s, nseq):
    n_pairs = HG_HEADS // SAMPLE_HEADS
    unit = jnp.minimum(s, nseq // SAMPLE_SEQS * n_pairs - 1)
    return unit // n_pairs, unit % n_pairs, jnp.minimum(s, nseq // SAMPLE_SEQS - 1)


def _sample_state_update(s, sst_ref, ssc_ref, snst_ref, snsc_ref, cw_ref, cb, sft_ref, sqt_ref, sv_ref, sglu_ref,
                         so_ref, sdw_ref):
    nseq = sft_ref.shape[2]
    sb, hp, cblk = _sample_units(s, nseq)
    r0 = pl.multiple_of(sb * SAMPLE_SEQS, SAMPLE_SEQS)
    shift = lax.rem(nseq - r0, nseq)
    v_rows = sv_ref[hp, pl.ds(r0, SAMPLE_SEQS), :]
    seq_row = lax.broadcasted_iota(jnp.int32, (SAMPLE_SEQS, HG_DV), 0)
    for hh in range(SAMPLE_HEADS):
        h = hp * SAMPLE_HEADS + hh
        f_t = pltpu.roll(sft_ref[h], shift, 1)
        q_t = pltpu.roll(sqt_ref[h], shift, 1)
        o = jnp.zeros((SAMPLE_SEQS, HG_DV), F32)
        for j in range(SAMPLE_SEQS):
            f_b = jnp.broadcast_to(f_t[:, j:j + 1], (HG_DK, HG_DV))
            s_new = f_b * sst_ref[j, hh] + (1.0 - f_b) * v_rows[j:j + 1, hh * HG_DV:(hh + 1) * HG_DV]
            snst_ref[j, hh] = s_new
            q_b = jnp.broadcast_to(q_t[:, j:j + 1], (HG_DK, HG_DV))
            o = jnp.where(seq_row == j, jnp.sum(q_b * s_new, axis=0, keepdims=True), o)
        so_ref[hp, pl.ds(r0, SAMPLE_SEQS), hh * HG_DV:(hh + 1) * HG_DV] = o
    c0 = pl.multiple_of(cblk * SAMPLE_SEQS, SAMPLE_SEQS)
    glu = sglu_ref[pl.ds(c0, SAMPLE_SEQS), :]
    taps = CONV_K - 1
    dw = cb + cw_ref[taps:taps + 1, :] * glu
    for j in range(taps):
        dw = dw + cw_ref[j:j + 1, :] * ssc_ref[j]
    sdw_ref[pl.ds(c0, SAMPLE_SEQS), :] = dw
    snsc_ref[0:taps - 1] = ssc_ref[1:taps]
    snsc_ref[taps - 1] = glu


def _const_spec(shape, single=False):
    zeros = (0,) * len(shape)
    if single:
        return pl.BlockSpec(shape, lambda *_: zeros, pipeline_mode=pl.Buffered(1))
    return pl.BlockSpec(shape, lambda *_: zeros)


def _layer(x, meta, lbp, n1g, win, og, cw, cb, lng, lnb, wout, n2g, wup, wdn, fg, xs, sst, ssc):
    bsz, seq, d = x.shape
    nt = seq // TILE_T
    ns = bsz * nt
    nseq = xs.shape[0]
    n_pairs = HG_HEADS // SAMPLE_HEADS
    assert nseq // SAMPLE_SEQS * n_pairs <= ns, "one (sequence block, head pair) per grid step"

    def mix_tile(s):
        s1 = jnp.minimum(s, ns - 1)
        return s1 // nt, s1 % nt

    def mlp_tile(s):
        s2 = jnp.maximum(s - 1, 0)
        return s2 // nt, s2 % nt

    state_spec = pl.BlockSpec((SAMPLE_SEQS, SAMPLE_HEADS, HG_DK, HG_DV),
                              lambda s: (*_sample_units(s, nseq)[:2], 0, 0))
    conv_spec = pl.BlockSpec((CONV_K - 1, SAMPLE_SEQS, CV_WIDTH), lambda s: (0, _sample_units(s, nseq)[2], 0))
    pair = SAMPLE_HEADS * HG_DV
    hbm = pl.BlockSpec(memory_space=pl.ANY)
    n_slot = (TILE_T // W_ROWS) * (C_END // W_COLS)

    return pl.pallas_call(
        functools.partial(_layer_kernel, tile_t=TILE_T, nt=nt),
        grid=(ns + 1,),
        in_specs=[
            pl.BlockSpec((1, TILE_T, d), lambda s: (*mix_tile(s), 0)),
            _const_spec(meta.shape), _const_spec(lbp.shape), _const_spec(n1g.shape),
            hbm, _const_spec(og.shape), _const_spec(cw.shape),
            _const_spec(cb.shape), _const_spec(lng.shape), _const_spec(lnb.shape),
            hbm, _const_spec(n2g.shape), hbm, hbm, _const_spec(fg.shape),
            _const_spec(xs.shape, single=True), state_spec, conv_spec,
        ],
        out_specs=[
            pl.BlockSpec((1, TILE_T, d), lambda s: (*mlp_tile(s), 0)),
            pl.BlockSpec((1, HG_HEADS, HG_DK, HG_DV), lambda s: (mix_tile(s)[0], 0, 0, 0)),
            pl.BlockSpec((1, CONV_K - 1, CV_WIDTH), lambda s: (mix_tile(s)[0], 0, 0)),
            state_spec, conv_spec, _const_spec(xs.shape),
        ],
        out_shape=[
            jax.ShapeDtypeStruct((bsz, seq, d), F32),
            jax.ShapeDtypeStruct((bsz, HG_HEADS, HG_DK, HG_DV), F32),
            jax.ShapeDtypeStruct((bsz, CONV_K - 1, CV_WIDTH), F32),
            jax.ShapeDtypeStruct(sst.shape, F32),
            jax.ShapeDtypeStruct(ssc.shape, F32),
            jax.ShapeDtypeStruct(xs.shape, F32),
        ],
        scratch_shapes=[
            pltpu.VMEM(win.shape[1:], BF16), pltpu.VMEM(wout.shape[1:], BF16),
            pltpu.VMEM(wup.shape[1:], BF16), pltpu.VMEM(wdn.shape[1:], BF16),
            pltpu.SemaphoreType.DMA((n_slot,)),
            pltpu.VMEM((TILE_T, C_END), F32),
            pltpu.VMEM((TILE_T, HG_WIDTH), F32),
            pltpu.VMEM((TILE_T, HG_WIDTH + CV_WIDTH), BF16),
            pltpu.VMEM((HG_HEADS, HG_DV, HG_DK), F32),
            pltpu.VMEM((HIST + TILE_T, CV_WIDTH), F32),
            pltpu.VMEM((TILE_T, d), F32),
            pltpu.VMEM((TILE_T, d), BF16),
            pltpu.VMEM((TILE_T, wup.shape[2] // (TILE_T // CHUNK)), BF16),
            pltpu.VMEM((HG_HEADS, HG_DK, nseq), F32),
            pltpu.VMEM((HG_HEADS, HG_DK, nseq), F32),
            pltpu.VMEM((n_pairs, nseq, pair), F32),
            pltpu.VMEM((nseq, CV_WIDTH), F32),
            pltpu.VMEM((nseq, HG_WIDTH), F32),
            pltpu.VMEM((n_pairs, nseq, pair), F32),
            pltpu.VMEM((nseq, CV_WIDTH), F32),
        ],
        compiler_params=pltpu.CompilerParams(dimension_semantics=("arbitrary",), vmem_limit_bytes=VMEM_LIMIT),
        name="layer",
    )(x, meta, lbp, n1g, win, og, cw, cb, lng, lnb, wout, n2g, wup, wdn, fg, xs, sst, ssc)


def kernel(x_prompt, x_sample, state_hgrn, state_conv, meta_tokens, hg_lb, norm1_g, w_in, hg_onorm_g, conv_w, conv_b,
           conv_ln_g, conv_ln_b, w_out, norm2_g, w_up, w_down, final_g):
    assert state_hgrn.shape[0] == 1, "single-layer stack"
    bsz, seq, d = x_prompt.shape
    row = lambda a: a.reshape(1, -1)
    n1g, og, cb, lng, lnb, n2g, fg = (row(norm1_g[0]), row(hg_onorm_g[0]), row(conv_b[0]), row(conv_ln_g[0]),
                                      row(conv_ln_b[0]), row(norm2_g[0]), row(final_g))
    cw = conv_w[0]

    assert x_sample.shape[1] == 1, "one new token per running sequence"
    x_s = x_sample.reshape(x_sample.shape[0], d)
    y_p, s_p, c_p, s_s, c_s, y_s = _layer(x_prompt, meta_tokens, hg_lb, n1g, w_in, og, cw, cb, lng, lnb, w_out, n2g,
                                          w_up, w_down, fg, x_s, state_hgrn[0], jnp.swapaxes(state_conv[0], 0, 1))

    return (y_p, y_s.reshape(x_sample.shape), s_p[None], c_p[None], s_s[None], jnp.swapaxes(c_s, 0, 1)[None])
```

```python
import functools

import jax
import jax.numpy as jnp
from jax import lax
from jax.experimental import pallas as pl
from jax.experimental.pallas import tpu as pltpu

F32 = jnp.float32
BF16 = jnp.bfloat16

N_META = 16
HG_HEADS = 4
HG_DK = 128
HG_DV = 128
HG_WIDTH = HG_HEADS * HG_DK
CV_WIDTH = 512
CONV_K = 31
CHUNK = 64
HG_CHUNK = 128
EPS = 1e-6
C_Q, C_F, C_I, C_G, C_A, C_B, C_END = (0, HG_WIDTH, 2 * HG_WIDTH, 3 * HG_WIDTH, 4 * HG_WIDTH,
                                       4 * HG_WIDTH + CV_WIDTH, 4 * HG_WIDTH + 2 * CV_WIDTH)
HIST = 32
MXU_COLS = 256
CUMSUM_ROWS = 256
W_ROWS, W_COLS = 256, 1024
TILE_T = 512
SAMPLE_SEQS = 8
SAMPLE_HEADS = 2
VMEM_LIMIT = 62 * 1024 * 1024


def _rms(x, g):
    return x * lax.rsqrt(jnp.mean(x * x, axis=-1, keepdims=True) + EPS) * g


def _layernorm(x, g, b):
    xc = x - jnp.mean(x, axis=-1, keepdims=True)
    return xc * lax.rsqrt(jnp.mean(xc * xc, axis=-1, keepdims=True) + EPS) * g + b


def _silu(x):
    return x * jax.nn.sigmoid(x)


def _lower_bound(lbp):
    e = jnp.exp(lbp - jnp.max(lbp, axis=0, keepdims=True))
    return e[0:1] / jnp.sum(e, axis=0, keepdims=True)


def _project(x, n1g, win_ref):
    hn = _rms(x, n1g).astype(BF16)
    return jnp.dot(hn, win_ref[...], preferred_element_type=F32)


def _tril(n):
    r = lax.broadcasted_iota(jnp.int32, (n, n), 0)
    c = lax.broadcasted_iota(jnp.int32, (n, n), 1)
    return r >= c


def _cumsum_rows(x):
    n = x.shape[0]
    return jnp.dot(_tril(n).astype(F32), x, precision=lax.Precision.HIGHEST, preferred_element_type=F32)


def _chunk_cumsum(x):
    n = x.shape[0]
    r = lax.broadcasted_iota(jnp.int32, (n, n), 0)
    c = lax.broadcasted_iota(jnp.int32, (n, n), 1)
    tri = ((r >= c) & (r // HG_CHUNK == c // HG_CHUNK)).astype(BF16)
    hi = x.astype(BF16)
    rest = x - hi.astype(F32)
    mid = rest.astype(BF16)
    lo = (rest - mid.astype(F32)).astype(BF16)
    return (jnp.dot(tri, hi, preferred_element_type=F32) + jnp.dot(tri, mid, preferred_element_type=F32)
            + jnp.dot(tri, lo, preferred_element_type=F32))


def _head(h):
    return slice(h * HG_DK, (h + 1) * HG_DK)


def _dot_nt(a, b):
    return lax.dot_general(a, b, (((1,), (1,)), ((), ())), preferred_element_type=F32)


def _dot_tn(a, b):
    return lax.dot_general(a, b, (((0,), (0,)), ((), ())), preferred_element_type=F32)


def _mlp(h, n2g, wup_ref, wdn_ref, fg):
    hn = _rms(h, n2g).astype(BF16)
    u = jnp.dot(hn, wup_ref[...], preferred_element_type=F32)
    a = jnp.square(jnp.maximum(u, 0.0)).astype(BF16)
    h2 = h + jnp.dot(a, wdn_ref[...], preferred_element_type=F32)
    return _rms(h2, fg)


def _dwconv_lane_group(e_ref, cw_ref, bias, base, rows, g):
    sub = 8
    lanes = slice(g * 128, (g + 1) * 128)
    ext = e_ref[base - HIST:base + rows, lanes]
    acc = None
    for r in range(sub):
        u = None
        for a in range(-(-CONV_K // sub)):
            d = sub * a + r
            if d < CONV_K:
                lo = HIST - sub - sub * a
                term = cw_ref[CONV_K - 1 - d:CONV_K - d, lanes] * ext[lo:lo + rows + sub]
                u = term if u is None else u + term
        shifted = u[sub:] if r == 0 else pltpu.roll(u, r, 0)[sub:]
        acc = shifted if acc is None else acc + shifted
    return acc + bias[:, lanes]


def _layer_kernel(x_ref, meta_ref, lbp_ref, n1g_ref, win_hbm, og_ref, cw_ref, cb_ref, lng_ref, lnb_ref,
                  wout_hbm, n2g_ref, wup_hbm, wdn_hbm, fg_ref, sx_ref, sst_ref, ssc_ref,
                  y_out_ref, sfin_ref, cfin_ref, snst_ref, snsc_ref, ys_ref,
                  win_ref, wout_ref, wup_ref, wdn_ref, wsem,
                  z_ref, b_ref, y_ref, st_ref, e_ref, h_ref, hn_ref, a_ref,
                  sft_ref, sqt_ref, sv_ref, sglu_ref, sgate_ref, so_ref, sdw_ref, *, tile_t, nt):
    s = pl.program_id(0)
    ns = pl.num_programs(0) - 1
    mixing = s < ns
    t = lax.rem(jnp.minimum(s, ns - 1), nt)
    lb = _lower_bound(lbp_ref[...])
    n1g = n1g_ref[...]

    @pl.when(s == 0)
    def _first_step():
        _load_weights_as_bf16(((win_hbm, win_ref), (wout_hbm, wout_ref), (wup_hbm, wup_ref), (wdn_hbm, wdn_ref)),
                              z_ref, wsem)
        h_ref[...] = jnp.zeros(h_ref.shape, F32)
        zs = _project(sx_ref[...], n1g, win_ref)
        fs = lb + (1.0 - lb) * jax.nn.sigmoid(zs[:, C_F:C_I])
        qs = _silu(zs[:, C_Q:C_F])
        for h in range(HG_HEADS):
            sft_ref[h] = fs[:, _head(h)].T
            sqt_ref[h] = qs[:, _head(h)].T
        pair = SAMPLE_HEADS * HG_DV
        for p in range(HG_HEADS // SAMPLE_HEADS):
            sv_ref[p] = zs[:, C_I + p * pair:C_I + (p + 1) * pair]
        sglu_ref[...] = zs[:, C_A:C_B] * jax.nn.sigmoid(zs[:, C_B:C_END])
        sgate_ref[...] = _silu(zs[:, C_G:C_A])

    @pl.when(mixing & (t == 0))
    def _meta_tokens():
        zm = _project(meta_ref[...], n1g, win_ref)
        f = lb + (1.0 - lb) * jax.nn.sigmoid(zm[:, C_F:C_I])
        b = _cumsum_rows(jnp.log(f))
        k_end = ((1.0 - f) * jnp.exp(b[N_META - 1:N_META] - b)).astype(BF16)
        v = zm[:, C_I:C_G].astype(BF16)
        for h in range(HG_HEADS):
            st_ref[h] = _dot_tn(v[:, _head(h)], k_end[:, _head(h)])
        e_ref[0:HIST - N_META, :] = jnp.zeros((HIST - N_META, CV_WIDTH), F32)
        e_ref[HIST - N_META:HIST, :] = zm[:, C_A:C_B] * jax.nn.sigmoid(zm[:, C_B:C_END])

    cb = cb_ref[...]
    z_ref[...] = _project(x_ref[0], n1g, win_ref)
    _sample_state_update(s, sst_ref, ssc_ref, snst_ref, snsc_ref, cw_ref, cb, sft_ref, sqt_ref, sv_ref, sglu_ref,
                         so_ref, sdw_ref)
    hn_ref[...] = _rms(h_ref[...], n2g_ref[...]).astype(BF16)
    y_out_ref[0] = h_ref[...]
    for r0 in range(0, tile_t, CUMSUM_ROWS):
        rows = slice(r0, r0 + CUMSUM_ROWS)
        f = lb + (1.0 - lb) * jax.nn.sigmoid(z_ref[rows, C_F:C_I])
        z_ref[rows, C_F:C_I] = f
        b_ref[rows, :] = _chunk_cumsum(jnp.log(f))
    causal = _tril(HG_CHUNK)
    og = og_ref[...]
    ff = wup_ref.shape[1] // (tile_t // HG_CHUNK)
    dt = MXU_COLS
    assert (CV_WIDTH // 128) * dt == y_out_ref.shape[2]

    for c in range(tile_t // HG_CHUNK):
        rows = slice(c * HG_CHUNK, (c + 1) * HG_CHUNK)
        f = z_ref[rows, C_F:C_I]
        k = 1.0 - f
        b = b_ref[rows, :]
        b_mid = b[HG_CHUNK // 2 - 1:HG_CHUNK // 2]
        b_end = b[HG_CHUNK - 1:HG_CHUNK]
        q = _silu(z_ref[rows, C_Q:C_F])
        q_rel = (q * jnp.exp(b - b_mid)).astype(BF16)
        k_rel = (k * jnp.exp(b_mid - b)).astype(BF16)
        q_abs = (q * jnp.exp(b)).astype(BF16)
        k_end = (k * jnp.exp(b_end - b)).astype(BF16)
        decay = jnp.exp(b_end)
        v = z_ref[rows, C_I:C_G].astype(BF16)
        gate = _silu(z_ref[rows, C_G:C_A])
        for h in range(HG_HEADS):
            hs = _head(h)
            scores = jnp.where(causal, _dot_nt(q_rel[:, hs], k_rel[:, hs]), 0.0)
            s_t = st_ref[h]
            o = (jnp.dot(scores.astype(BF16), v[:, hs], preferred_element_type=F32)
                 + _dot_nt(q_abs[:, hs], s_t.astype(BF16)))
            st_ref[h] = s_t * decay[:, hs] + _dot_tn(v[:, hs], k_end[:, hs])
            y_ref[rows, hs] = (_rms(o, og) * gate[:, hs]).astype(BF16)
        for n in range(ff // MXU_COLS):
            ucols = slice(c * ff + n * MXU_COLS, c * ff + (n + 1) * MXU_COLS)
            u = jnp.dot(hn_ref[...], wup_ref[:, ucols], preferred_element_type=F32)
            a_ref[:, n * MXU_COLS:(n + 1) * MXU_COLS] = jnp.square(jnp.maximum(u, 0.0)).astype(BF16)
        e_ref[HIST + c * HG_CHUNK:HIST + (c + 1) * HG_CHUNK, :] = (z_ref[rows, C_A:C_B]
                                                                   * jax.nn.sigmoid(z_ref[rows, C_B:C_END]))
        dw = []
        for g in range(CV_WIDTH // 128):
            dw.append(_dwconv_lane_group(e_ref, cw_ref, cb, HIST + c * HG_CHUNK, HG_CHUNK, g))
            dcols = slice(g * dt, (g + 1) * dt)
            y_out_ref[0, :, dcols] += jnp.dot(a_ref[...], wdn_ref[c * ff:(c + 1) * ff, dcols],
                                              preferred_element_type=F32)
        dw = jnp.concatenate(dw, axis=-1)
        y_ref[rows, HG_WIDTH:] = _silu(_layernorm(dw, lng_ref[...], lnb_ref[...])).astype(BF16)

    y_out_ref[0] = _rms(y_out_ref[0], fg_ref[...])
    h_ref[...] = x_ref[0] + jnp.dot(y_ref[...], wout_ref[...], preferred_element_type=F32)
    e_ref[0:HIST, :] = e_ref[tile_t:tile_t + HIST, :]

    @pl.when(mixing & (t == nt - 1))
    def _final_state():
        for h in range(HG_HEADS):
            sfin_ref[0, h] = st_ref[h].T
        cfin_ref[0] = e_ref[tile_t + HIST - (CONV_K - 1):tile_t + HIST, :]

    @pl.when(s == ns)
    def _sample_outputs():
        y_hg = []
        for h in range(HG_HEADS):
            p, hh = divmod(h, SAMPLE_HEADS)
            o_h = so_ref[p, :, hh * HG_DV:(hh + 1) * HG_DV]
            y_hg.append((_rms(o_h, og) * sgate_ref[:, _head(h)]).astype(BF16))
        y_cv = _silu(_layernorm(sdw_ref[...], lng_ref[...], lnb_ref[...])).astype(BF16)
        ycat = jnp.concatenate(y_hg + [y_cv], axis=-1)
        hs = sx_ref[...] + jnp.dot(ycat, wout_ref[...], preferred_element_type=F32)
        ys_ref[...] = _mlp(hs, n2g_ref[...], wup_ref, wdn_ref, fg_ref[...])


def _weight_slots(stage_ref):
    per_row = stage_ref.shape[1] // W_COLS
    n_slot = (stage_ref.shape[0] // W_ROWS) * per_row

    def slot(k):
        r, c = divmod(k, per_row)
        return stage_ref.at[pl.ds(r * W_ROWS, W_ROWS), pl.ds(c * W_COLS, W_COLS)]

    return n_slot, slot


def _load_weights_as_bf16(pairs, stage_ref, sem):
    n_slot, slot = _weight_slots(stage_ref)
    blocks = [(src, dst, r, c) for src, dst in pairs
              for r in range(0, dst.shape[0], W_ROWS) for c in range(0, dst.shape[1], W_COLS)]

    def copy(i):
        src, _, r, c = blocks[i]
        k = i % n_slot
        return pltpu.make_async_copy(src.at[0, pl.ds(r, W_ROWS), pl.ds(c, W_COLS)], slot(k), sem.at[k])

    for i in range(min(n_slot, len(blocks))):
        copy(i).start()
    for i, (_, dst, r, c) in enumerate(blocks):
        copy(i).wait()
        dst[r:r + W_ROWS, c:c + W_COLS] = slot(i % n_slot)[...].astype(BF16)
        if i + n_slot < len(blocks):
            copy(i + n_slot).start()


def _sample_units(s, nseq):
    n_pairs = HG_HEADS // SAMPLE_HEADS
    unit = jnp.minimum(s, nseq // SAMPLE_SEQS * n_pairs - 1)
    return unit // n_pairs, unit % n_pairs, jnp.minimum(s, nseq // SAMPLE_SEQS - 1)


def _sample_state_update(s, sst_ref, ssc_ref, snst_ref, snsc_ref, cw_ref, cb, sft_ref, sqt_ref, sv_ref, sglu_ref,
                         so_ref, sdw_ref):
    nseq = sft_ref.shape[2]
    sb, hp, cblk = _sample_units(s, nseq)
    r0 = pl.multiple_of(sb * SAMPLE_SEQS, SAMPLE_SEQS)
    shift = lax.rem(nseq - r0, nseq)
    v_rows = sv_ref[hp, pl.ds(r0, SAMPLE_SEQS), :]
    seq_row = lax.broadcasted_iota(jnp.int32, (SAMPLE_SEQS, HG_DV), 0)
    for hh in range(SAMPLE_HEADS):
        h = hp * SAMPLE_HEADS + hh
        f_t = pltpu.roll(sft_ref[h], shift, 1)
        q_t = pltpu.roll(sqt_ref[h], shift, 1)
        o = jnp.zeros((SAMPLE_SEQS, HG_DV), F32)
        for j in range(SAMPLE_SEQS):
            f_b = jnp.broadcast_to(f_t[:, j:j + 1], (HG_DK, HG_DV))
            s_new = f_b * sst_ref[j, hh] + (1.0 - f_b) * v_rows[j:j + 1, hh * HG_DV:(hh + 1) * HG_DV]
            snst_ref[j, hh] = s_new
            q_b = jnp.broadcast_to(q_t[:, j:j + 1], (HG_DK, HG_DV))
            o = jnp.where(seq_row == j, jnp.sum(q_b * s_new, axis=0, keepdims=True), o)
        so_ref[hp, pl.ds(r0, SAMPLE_SEQS), hh * HG_DV:(hh + 1) * HG_DV] = o
    c0 = pl.multiple_of(cblk * SAMPLE_SEQS, SAMPLE_SEQS)
    glu = sglu_ref[pl.ds(c0, SAMPLE_SEQS), :]
    taps = CONV_K - 1
    dw = cb + cw_ref[taps:taps + 1, :] * glu
    for j in range(taps):
        dw = dw + cw_ref[j:j + 1, :] * ssc_ref[j]
    sdw_ref[pl.ds(c0, SAMPLE_SEQS), :] = dw
    snsc_ref[0:taps - 1] = ssc_ref[1:taps]
    snsc_ref[taps - 1] = glu


def _const_spec(shape, single=False):
    zeros = (0,) * len(shape)
    if single:
        return pl.BlockSpec(shape, lambda *_: zeros, pipeline_mode=pl.Buffered(1))
    return pl.BlockSpec(shape, lambda *_: zeros)


def _layer(x, meta, lbp, n1g, win, og, cw, cb, lng, lnb, wout, n2g, wup, wdn, fg, xs, sst, ssc):
    bsz, seq, d = x.shape
    nt = seq // TILE_T
    ns = bsz * nt
    nseq = xs.shape[0]
    n_pairs = HG_HEADS // SAMPLE_HEADS
    assert nseq // SAMPLE_SEQS * n_pairs <= ns, "one (sequence block, head pair) per grid step"

    def mix_tile(s):
        s1 = jnp.minimum(s, ns - 1)
        return s1 // nt, s1 % nt

    def mlp_tile(s):
        s2 = jnp.maximum(s - 1, 0)
        return s2 // nt, s2 % nt

    state_spec = pl.BlockSpec((SAMPLE_SEQS, SAMPLE_HEADS, HG_DK, HG_DV),
                              lambda s: (*_sample_units(s, nseq)[:2], 0, 0))
    conv_spec = pl.BlockSpec((CONV_K - 1, SAMPLE_SEQS, CV_WIDTH), lambda s: (0, _sample_units(s, nseq)[2], 0))
    pair = SAMPLE_HEADS * HG_DV
    hbm = pl.BlockSpec(memory_space=pl.ANY)
    n_slot = (TILE_T // W_ROWS) * (C_END // W_COLS)

    return pl.pallas_call(
        functools.partial(_layer_kernel, tile_t=TILE_T, nt=nt),
        grid=(ns + 1,),
        in_specs=[
            pl.BlockSpec((1, TILE_T, d), lambda s: (*mix_tile(s), 0)),
            _const_spec(meta.shape), _const_spec(lbp.shape), _const_spec(n1g.shape),
            hbm, _const_spec(og.shape), _const_spec(cw.shape),
            _const_spec(cb.shape), _const_spec(lng.shape), _const_spec(lnb.shape),
            hbm, _const_spec(n2g.shape), hbm, hbm, _const_spec(fg.shape),
            _const_spec(xs.shape, single=True), state_spec, conv_spec,
        ],
        out_specs=[
            pl.BlockSpec((1, TILE_T, d), lambda s: (*mlp_tile(s), 0)),
            pl.BlockSpec((1, HG_HEADS, HG_DK, HG_DV), lambda s: (mix_tile(s)[0], 0, 0, 0)),
            pl.BlockSpec((1, CONV_K - 1, CV_WIDTH), lambda s: (mix_tile(s)[0], 0, 0)),
            state_spec, conv_spec, _const_spec(xs.shape),
        ],
        out_shape=[
            jax.ShapeDtypeStruct((bsz, seq, d), F32),
            jax.ShapeDtypeStruct((bsz, HG_HEADS, HG_DK, HG_DV), F32),
            jax.ShapeDtypeStruct((bsz, CONV_K - 1, CV_WIDTH), F32),
            jax.ShapeDtypeStruct(sst.shape, F32),
            jax.ShapeDtypeStruct(ssc.shape, F32),
            jax.ShapeDtypeStruct(xs.shape, F32),
        ],
        scratch_shapes=[
            pltpu.VMEM(win.shape[1:], BF16), pltpu.VMEM(wout.shape[1:], BF16),
            pltpu.VMEM(wup.shape[1:], BF16), pltpu.VMEM(wdn.shape[1:], BF16),
            pltpu.SemaphoreType.DMA((n_slot,)),
            pltpu.VMEM((TILE_T, C_END), F32),
            pltpu.VMEM((TILE_T, HG_WIDTH), F32),
            pltpu.VMEM((TILE_T, HG_WIDTH + CV_WIDTH), BF16),
            pltpu.VMEM((HG_HEADS, HG_DV, HG_DK), F32),
            pltpu.VMEM((HIST + TILE_T, CV_WIDTH), F32),
            pltpu.VMEM((TILE_T, d), F32),
            pltpu.VMEM((TILE_T, d), BF16),
            pltpu.VMEM((TILE_T, wup.shape[2] // (TILE_T // HG_CHUNK)), BF16),
            pltpu.VMEM((HG_HEADS, HG_DK, nseq), F32),
            pltpu.VMEM((HG_HEADS, HG_DK, nseq), F32),
            pltpu.VMEM((n_pairs, nseq, pair), F32),
            pltpu.VMEM((nseq, CV_WIDTH), F32),
            pltpu.VMEM((nseq, HG_WIDTH), F32),
            pltpu.VMEM((n_pairs, nseq, pair), F32),
            pltpu.VMEM((nseq, CV_WIDTH), F32),
        ],
        compiler_params=pltpu.CompilerParams(dimension_semantics=("arbitrary",), vmem_limit_bytes=VMEM_LIMIT),
        name="layer",
    )(x, meta, lbp, n1g, win, og, cw, cb, lng, lnb, wout, n2g, wup, wdn, fg, xs, sst, ssc)


def kernel(x_prompt, x_sample, state_hgrn, state_conv, meta_tokens, hg_lb, norm1_g, w_in, hg_onorm_g, conv_w, conv_b,
           conv_ln_g, conv_ln_b, w_out, norm2_g, w_up, w_down, final_g):
    assert state_hgrn.shape[0] == 1, "single-layer stack"
    bsz, seq, d = x_prompt.shape
    row = lambda a: a.reshape(1, -1)
    n1g, og, cb, lng, lnb, n2g, fg = (row(norm1_g[0]), row(hg_onorm_g[0]), row(conv_b[0]), row(conv_ln_g[0]),
                                      row(conv_ln_b[0]), row(norm2_g[0]), row(final_g))
    cw = conv_w[0]

    assert x_sample.shape[1] == 1, "one new token per running sequence"
    x_s = x_sample.reshape(x_sample.shape[0], d)
    y_p, s_p, c_p, s_s, c_s, y_s = _layer(x_prompt, meta_tokens, hg_lb, n1g, w_in, og, cw, cb, lng, lnb, w_out, n2g,
                                          w_up, w_down, fg, x_s, state_hgrn[0], jnp.swapaxes(state_conv[0], 0, 1))

    return (y_p, y_s.reshape(x_sample.shape), s_p[None], c_p[None], s_s[None], jnp.swapaxes(c_s, 0, 1)[None])
```

```python
import functools

import jax
import jax.numpy as jnp
from jax import lax
from jax.experimental import pallas as pl
from jax.experimental.pallas import tpu as pltpu

F32 = jnp.float32
BF16 = jnp.bfloat16

N_META = 16
HG_HEADS = 4
HG_DK = 128
HG_DV = 128
HG_WIDTH = HG_HEADS * HG_DK
CV_WIDTH = 512
CONV_K = 31
HG_CHUNK = 128
EPS = 1e-6
C_Q, C_F, C_I, C_G, C_A, C_B, C_END = (0, HG_WIDTH, 2 * HG_WIDTH, 3 * HG_WIDTH, 4 * HG_WIDTH,
                                       4 * HG_WIDTH + CV_WIDTH, 4 * HG_WIDTH + 2 * CV_WIDTH)
HIST = 32
MXU_COLS = 256
CUMSUM_ROWS = 256
W_ROWS, W_COLS = 256, 1024
TILE_T = 512
SAMPLE_SEQS = 8
SAMPLE_HEADS = 2
VMEM_LIMIT = 62 * 1024 * 1024


def _rms(x, g):
    return x * lax.rsqrt(jnp.mean(x * x, axis=-1, keepdims=True) + EPS) * g


def _layernorm(x, g, b):
    xc = x - jnp.mean(x, axis=-1, keepdims=True)
    return xc * lax.rsqrt(jnp.mean(xc * xc, axis=-1, keepdims=True) + EPS) * g + b


def _silu(x):
    return x * jax.nn.sigmoid(x)


def _lower_bound(lbp):
    e = jnp.exp(lbp - jnp.max(lbp, axis=0, keepdims=True))
    return e[0:1] / jnp.sum(e, axis=0, keepdims=True)


def _project(x, n1g, win_ref):
    hn = _rms(x, n1g).astype(BF16)
    return jnp.dot(hn, win_ref[...], preferred_element_type=F32)


def _tril(n):
    r = lax.broadcasted_iota(jnp.int32, (n, n), 0)
    c = lax.broadcasted_iota(jnp.int32, (n, n), 1)
    return r >= c


def _cumsum_rows(x):
    n = x.shape[0]
    return jnp.dot(_tril(n).astype(F32), x, precision=lax.Precision.HIGHEST, preferred_element_type=F32)


def _chunk_cumsum(x):
    n = x.shape[0]
    r = lax.broadcasted_iota(jnp.int32, (n, n), 0)
    c = lax.broadcasted_iota(jnp.int32, (n, n), 1)
    tri = ((r >= c) & (r // HG_CHUNK == c // HG_CHUNK)).astype(BF16)
    hi = x.astype(BF16)
    rest = x - hi.astype(F32)
    mid = rest.astype(BF16)
    lo = (rest - mid.astype(F32)).astype(BF16)
    return (jnp.dot(tri, hi, preferred_element_type=F32) + jnp.dot(tri, mid, preferred_element_type=F32)
            + jnp.dot(tri, lo, preferred_element_type=F32))


def _head(h):
    return slice(h * HG_DK, (h + 1) * HG_DK)


def _dot_nt(a, b):
    return lax.dot_general(a, b, (((1,), (1,)), ((), ())), preferred_element_type=F32)


def _dot_tn(a, b):
    return lax.dot_general(a, b, (((0,), (0,)), ((), ())), preferred_element_type=F32)


def _mlp(h, n2g, wup_ref, wdn_ref, fg):
    hn = _rms(h, n2g).astype(BF16)
    u = jnp.dot(hn, wup_ref[...], preferred_element_type=F32)
    a = jnp.square(jnp.maximum(u, 0.0)).astype(BF16)
    h2 = h + jnp.dot(a, wdn_ref[...], preferred_element_type=F32)
    return _rms(h2, fg)


def _dwconv_lane_group(e_ref, cw_ref, bias, base, rows, g):
    sub = 8
    lanes = slice(g * 128, (g + 1) * 128)
    ext = e_ref[base - HIST:base + rows, lanes]
    acc = None
    for r in range(sub):
        u = None
        for a in range(-(-CONV_K // sub)):
            d = sub * a + r
            if d < CONV_K:
                lo = HIST - sub - sub * a
                term = cw_ref[CONV_K - 1 - d:CONV_K - d, lanes] * ext[lo:lo + rows + sub]
                u = term if u is None else u + term
        shifted = u[sub:] if r == 0 else pltpu.roll(u, r, 0)[sub:]
        acc = shifted if acc is None else acc + shifted
    return acc + bias[:, lanes]


def _layer_kernel(x_ref, meta_ref, lbp_ref, n1g_ref, win_hbm, og_ref, cw_ref, cb_ref, lng_ref, lnb_ref,
                  wout_hbm, n2g_ref, wup_hbm, wdn_hbm, fg_ref, sx_ref, sst_ref, ssc_ref,
                  y_out_ref, sfin_ref, cfin_ref, snst_ref, snsc_ref, ys_ref,
                  win_ref, wout_ref, wup_ref, wdn_ref, wsem,
                  z_ref, b_ref, y_ref, st_ref, mst_ref, mglu_ref, e_ref, h_ref, hn_ref, a_ref,
                  sft_ref, sqt_ref, sv_ref, sglu_ref, sgate_ref, so_ref, sdw_ref, *, tile_t, nt):
    s = pl.program_id(0)
    ns = pl.num_programs(0) - 1
    mixing = s < ns
    t = lax.rem(jnp.minimum(s, ns - 1), nt)
    lb = _lower_bound(lbp_ref[...])
    n1g = n1g_ref[...]

    @pl.when(s == 0)
    def _first_step():
        _load_weights_as_bf16(((win_hbm, win_ref), (wout_hbm, wout_ref), (wup_hbm, wup_ref), (wdn_hbm, wdn_ref)),
                              z_ref, wsem)
        h_ref[...] = jnp.zeros(h_ref.shape, F32)
        zm = _project(meta_ref[...], n1g, win_ref)
        fm = lb + (1.0 - lb) * jax.nn.sigmoid(zm[:, C_F:C_I])
        bm = _cumsum_rows(jnp.log(fm))
        k_end = ((1.0 - fm) * jnp.exp(bm[N_META - 1:N_META] - bm)).astype(BF16)
        vm = zm[:, C_I:C_G].astype(BF16)
        for h in range(HG_HEADS):
            mst_ref[h] = _dot_tn(vm[:, _head(h)], k_end[:, _head(h)])
        mglu_ref[...] = zm[:, C_A:C_B] * jax.nn.sigmoid(zm[:, C_B:C_END])
        zs = _project(sx_ref[...], n1g, win_ref)
        fs = lb + (1.0 - lb) * jax.nn.sigmoid(zs[:, C_F:C_I])
        qs = _silu(zs[:, C_Q:C_F])
        for h in range(HG_HEADS):
            sft_ref[h] = fs[:, _head(h)].T
            sqt_ref[h] = qs[:, _head(h)].T
        pair = SAMPLE_HEADS * HG_DV
        for p in range(HG_HEADS // SAMPLE_HEADS):
            sv_ref[p] = zs[:, C_I + p * pair:C_I + (p + 1) * pair]
        sglu_ref[...] = zs[:, C_A:C_B] * jax.nn.sigmoid(zs[:, C_B:C_END])
        sgate_ref[...] = _silu(zs[:, C_G:C_A])

    @pl.when(mixing & (t == 0))
    def _start_of_batch_row():
        st_ref[...] = mst_ref[...]
        e_ref[0:HIST - N_META, :] = jnp.zeros((HIST - N_META, CV_WIDTH), F32)
        e_ref[HIST - N_META:HIST, :] = mglu_ref[...]

    cb = cb_ref[...]
    z_ref[...] = _project(x_ref[0], n1g, win_ref)
    _sample_state_update(s, sst_ref, ssc_ref, snst_ref, snsc_ref, cw_ref, cb, sft_ref, sqt_ref, sv_ref, sglu_ref,
                         so_ref, sdw_ref)
    hn_ref[...] = _rms(h_ref[...], n2g_ref[...]).astype(BF16)
    y_out_ref[0] = h_ref[...]
    for r0 in range(0, tile_t, CUMSUM_ROWS):
        rows = slice(r0, r0 + CUMSUM_ROWS)
        f = lb + (1.0 - lb) * jax.nn.sigmoid(z_ref[rows, C_F:C_I])
        z_ref[rows, C_F:C_I] = f
        b_ref[rows, :] = _chunk_cumsum(jnp.log(f))
    causal = _tril(HG_CHUNK)
    og = og_ref[...]
    ff = wup_ref.shape[1] // (tile_t // HG_CHUNK)
    dt = MXU_COLS
    assert (CV_WIDTH // 128) * dt == y_out_ref.shape[2]

    for c in range(tile_t // HG_CHUNK):
        rows = slice(c * HG_CHUNK, (c + 1) * HG_CHUNK)
        f = z_ref[rows, C_F:C_I]
        k = 1.0 - f
        b = b_ref[rows, :]
        b_mid = b[HG_CHUNK // 2 - 1:HG_CHUNK // 2]
        b_end = b[HG_CHUNK - 1:HG_CHUNK]
        q = _silu(z_ref[rows, C_Q:C_F])
        q_rel = (q * jnp.exp(b - b_mid)).astype(BF16)
        k_rel = (k * jnp.exp(b_mid - b)).astype(BF16)
        q_abs = (q * jnp.exp(b)).astype(BF16)
        k_end = (k * jnp.exp(b_end - b)).astype(BF16)
        decay = jnp.exp(b_end)
        v = z_ref[rows, C_I:C_G].astype(BF16)
        gate = _silu(z_ref[rows, C_G:C_A])
        for h in range(HG_HEADS):
            hs = _head(h)
            scores = jnp.where(causal, _dot_nt(q_rel[:, hs], k_rel[:, hs]), 0.0)
            s_t = st_ref[h]
            o = (jnp.dot(scores.astype(BF16), v[:, hs], preferred_element_type=F32)
                 + _dot_nt(q_abs[:, hs], s_t.astype(BF16)))
            st_ref[h] = s_t * decay[:, hs] + _dot_tn(v[:, hs], k_end[:, hs])
            y_ref[rows, hs] = (_rms(o, og) * gate[:, hs]).astype(BF16)
        for n in range(ff // MXU_COLS):
            ucols = slice(c * ff + n * MXU_COLS, c * ff + (n + 1) * MXU_COLS)
            u = jnp.dot(hn_ref[...], wup_ref[:, ucols], preferred_element_type=F32)
            a_ref[:, n * MXU_COLS:(n + 1) * MXU_COLS] = jnp.square(jnp.maximum(u, 0.0)).astype(BF16)
        e_ref[HIST + c * HG_CHUNK:HIST + (c + 1) * HG_CHUNK, :] = (z_ref[rows, C_A:C_B]
                                                                   * jax.nn.sigmoid(z_ref[rows, C_B:C_END]))
        dw = []
        for g in range(CV_WIDTH // 128):
            dw.append(_dwconv_lane_group(e_ref, cw_ref, cb, HIST + c * HG_CHUNK, HG_CHUNK, g))
            dcols = slice(g * dt, (g + 1) * dt)
            y_out_ref[0, :, dcols] += jnp.dot(a_ref[...], wdn_ref[c * ff:(c + 1) * ff, dcols],
                                              preferred_element_type=F32)
        dw = jnp.concatenate(dw, axis=-1)
        y_ref[rows, HG_WIDTH:] = _silu(_layernorm(dw, lng_ref[...], lnb_ref[...])).astype(BF16)

    y_out_ref[0] = _rms(y_out_ref[0], fg_ref[...])
    h_ref[...] = x_ref[0] + jnp.dot(y_ref[...], wout_ref[...], preferred_element_type=F32)
    e_ref[0:HIST, :] = e_ref[tile_t:tile_t + HIST, :]

    @pl.when(mixing & (t == nt - 1))
    def _final_state():
        for h in range(HG_HEADS):
            sfin_ref[0, h] = st_ref[h].T
        cfin_ref[0] = e_ref[tile_t + HIST - (CONV_K - 1):tile_t + HIST, :]

    @pl.when(s == ns)
    def _sample_outputs():
        y_hg = []
        for h in range(HG_HEADS):
            p, hh = divmod(h, SAMPLE_HEADS)
            o_h = so_ref[p, :, hh * HG_DV:(hh + 1) * HG_DV]
            y_hg.append((_rms(o_h, og) * sgate_ref[:, _head(h)]).astype(BF16))
        y_cv = _silu(_layernorm(sdw_ref[...], lng_ref[...], lnb_ref[...])).astype(BF16)
        ycat = jnp.concatenate(y_hg + [y_cv], axis=-1)
        hs = sx_ref[...] + jnp.dot(ycat, wout_ref[...], preferred_element_type=F32)
        ys_ref[...] = _mlp(hs, n2g_ref[...], wup_ref, wdn_ref, fg_ref[...])


def _weight_slots(stage_ref):
    per_row = stage_ref.shape[1] // W_COLS
    n_slot = (stage_ref.shape[0] // W_ROWS) * per_row

    def slot(k):
        r, c = divmod(k, per_row)
        return stage_ref.at[pl.ds(r * W_ROWS, W_ROWS), pl.ds(c * W_COLS, W_COLS)]

    return n_slot, slot


def _load_weights_as_bf16(pairs, stage_ref, sem):
    n_slot, slot = _weight_slots(stage_ref)
    blocks = [(src, dst, r, c) for src, dst in pairs
              for r in range(0, dst.shape[0], W_ROWS) for c in range(0, dst.shape[1], W_COLS)]

    def copy(i):
        src, _, r, c = blocks[i]
        k = i % n_slot
        return pltpu.make_async_copy(src.at[0, pl.ds(r, W_ROWS), pl.ds(c, W_COLS)], slot(k), sem.at[k])

    for i in range(min(n_slot, len(blocks))):
        copy(i).start()
    for i, (_, dst, r, c) in enumerate(blocks):
        copy(i).wait()
        dst[r:r + W_ROWS, c:c + W_COLS] = slot(i % n_slot)[...].astype(BF16)
        if i + n_slot < len(blocks):
            copy(i + n_slot).start()


def _sample_units(s, nseq):
    n_pairs = HG_HEADS // SAMPLE_HEADS
    unit = jnp.minimum(s, nseq // SAMPLE_SEQS * n_pairs - 1)
    return unit // n_pairs, unit % n_pairs, jnp.minimum(s, nseq // SAMPLE_SEQS - 1)


def _sample_state_update(s, sst_ref, ssc_ref, snst_ref, snsc_ref, cw_ref, cb, sft_ref, sqt_ref, sv_ref, sglu_ref,
                         so_ref, sdw_ref):
    nseq = sft_ref.shape[2]
    sb, hp, cblk = _sample_units(s, nseq)
    r0 = pl.multiple_of(sb * SAMPLE_SEQS, SAMPLE_SEQS)
    shift = lax.rem(nseq - r0, nseq)
    v_rows = sv_ref[hp, pl.ds(r0, SAMPLE_SEQS), :]
    seq_row = lax.broadcasted_iota(jnp.int32, (SAMPLE_SEQS, HG_DV), 0)
    for hh in range(SAMPLE_HEADS):
        h = hp * SAMPLE_HEADS + hh
        f_t = pltpu.roll(sft_ref[h], shift, 1)
        q_t = pltpu.roll(sqt_ref[h], shift, 1)
        o = jnp.zeros((SAMPLE_SEQS, HG_DV), F32)
        for j in range(SAMPLE_SEQS):
            f_b = jnp.broadcast_to(f_t[:, j:j + 1], (HG_DK, HG_DV))
            s_new = f_b * sst_ref[j, hh] + (1.0 - f_b) * v_rows[j:j + 1, hh * HG_DV:(hh + 1) * HG_DV]
            snst_ref[j, hh] = s_new
            q_b = jnp.broadcast_to(q_t[:, j:j + 1], (HG_DK, HG_DV))
            o = jnp.where(seq_row == j, jnp.sum(q_b * s_new, axis=0, keepdims=True), o)
        so_ref[hp, pl.ds(r0, SAMPLE_SEQS), hh * HG_DV:(hh + 1) * HG_DV] = o
    c0 = pl.multiple_of(cblk * SAMPLE_SEQS, SAMPLE_SEQS)
    glu = sglu_ref[pl.ds(c0, SAMPLE_SEQS), :]
    taps = CONV_K - 1
    dw = cb + cw_ref[taps:taps + 1, :] * glu
    for j in range(taps):
        dw = dw + cw_ref[j:j + 1, :] * ssc_ref[j]
    sdw_ref[pl.ds(c0, SAMPLE_SEQS), :] = dw
    snsc_ref[0:taps - 1] = ssc_ref[1:taps]
    snsc_ref[taps - 1] = glu


def _const_spec(shape, single=False):
    zeros = (0,) * len(shape)
    if single:
        return pl.BlockSpec(shape, lambda *_: zeros, pipeline_mode=pl.Buffered(1))
    return pl.BlockSpec(shape, lambda *_: zeros)


def _layer(x, meta, lbp, n1g, win, og, cw, cb, lng, lnb, wout, n2g, wup, wdn, fg, xs, sst, ssc):
    bsz, seq, d = x.shape
    nt = seq // TILE_T
    ns = bsz * nt
    nseq = xs.shape[0]
    n_pairs = HG_HEADS // SAMPLE_HEADS
    assert nseq // SAMPLE_SEQS * n_pairs <= ns, "one (sequence block, head pair) per grid step"

    def mix_tile(s):
        s1 = jnp.minimum(s, ns - 1)
        return s1 // nt, s1 % nt

    def mlp_tile(s):
        s2 = jnp.maximum(s - 1, 0)
        return s2 // nt, s2 % nt

    state_spec = pl.BlockSpec((SAMPLE_SEQS, SAMPLE_HEADS, HG_DK, HG_DV),
                              lambda s: (*_sample_units(s, nseq)[:2], 0, 0))
    conv_spec = pl.BlockSpec((CONV_K - 1, SAMPLE_SEQS, CV_WIDTH), lambda s: (0, _sample_units(s, nseq)[2], 0))
    pair = SAMPLE_HEADS * HG_DV
    hbm = pl.BlockSpec(memory_space=pl.ANY)
    n_slot = (TILE_T // W_ROWS) * (C_END // W_COLS)

    return pl.pallas_call(
        functools.partial(_layer_kernel, tile_t=TILE_T, nt=nt),
        grid=(ns + 1,),
        in_specs=[
            pl.BlockSpec((1, TILE_T, d), lambda s: (*mix_tile(s), 0)),
            _const_spec(meta.shape), _const_spec(lbp.shape), _const_spec(n1g.shape),
            hbm, _const_spec(og.shape), _const_spec(cw.shape),
            _const_spec(cb.shape), _const_spec(lng.shape), _const_spec(lnb.shape),
            hbm, _const_spec(n2g.shape), hbm, hbm, _const_spec(fg.shape),
            _const_spec(xs.shape, single=True), state_spec, conv_spec,
        ],
        out_specs=[
            pl.BlockSpec((1, TILE_T, d), lambda s: (*mlp_tile(s), 0)),
            pl.BlockSpec((1, HG_HEADS, HG_DK, HG_DV), lambda s: (mix_tile(s)[0], 0, 0, 0)),
            pl.BlockSpec((1, CONV_K - 1, CV_WIDTH), lambda s: (mix_tile(s)[0], 0, 0)),
            state_spec, conv_spec, _const_spec(xs.shape),
        ],
        out_shape=[
            jax.ShapeDtypeStruct((bsz, seq, d), F32),
            jax.ShapeDtypeStruct((bsz, HG_HEADS, HG_DK, HG_DV), F32),
            jax.ShapeDtypeStruct((bsz, CONV_K - 1, CV_WIDTH), F32),
            jax.ShapeDtypeStruct(sst.shape, F32),
            jax.ShapeDtypeStruct(ssc.shape, F32),
            jax.ShapeDtypeStruct(xs.shape, F32),
        ],
        scratch_shapes=[
            pltpu.VMEM(win.shape[1:], BF16), pltpu.VMEM(wout.shape[1:], BF16),
            pltpu.VMEM(wup.shape[1:], BF16), pltpu.VMEM(wdn.shape[1:], BF16),
            pltpu.SemaphoreType.DMA((n_slot,)),
            pltpu.VMEM((TILE_T, C_END), F32),
            pltpu.VMEM((TILE_T, HG_WIDTH), F32),
            pltpu.VMEM((TILE_T, HG_WIDTH + CV_WIDTH), BF16),
            pltpu.VMEM((HG_HEADS, HG_DV, HG_DK), F32),
            pltpu.VMEM((HG_HEADS, HG_DV, HG_DK), F32),
            pltpu.VMEM((N_META, CV_WIDTH), F32),
            pltpu.VMEM((HIST + TILE_T, CV_WIDTH), F32),
            pltpu.VMEM((TILE_T, d), F32),
            pltpu.VMEM((TILE_T, d), BF16),
            pltpu.VMEM((TILE_T, wup.shape[2] // (TILE_T // HG_CHUNK)), BF16),
            pltpu.VMEM((HG_HEADS, HG_DK, nseq), F32),
            pltpu.VMEM((HG_HEADS, HG_DK, nseq), F32),
            pltpu.VMEM((n_pairs, nseq, pair), F32),
            pltpu.VMEM((nseq, CV_WIDTH), F32),
            pltpu.VMEM((nseq, HG_WIDTH), F32),
            pltpu.VMEM((n_pairs, nseq, pair), F32),
            pltpu.VMEM((nseq, CV_WIDTH), F32),
        ],
        compiler_params=pltpu.CompilerParams(dimension_semantics=("arbitrary",), vmem_limit_bytes=VMEM_LIMIT),
        name="layer",
    )(x, meta, lbp, n1g, win, og, cw, cb, lng, lnb, wout, n2g, wup, wdn, fg, xs, sst, ssc)


def kernel(x_prompt, x_sample, state_hgrn, state_conv, meta_tokens, hg_lb, norm1_g, w_in, hg_onorm_g, conv_w, conv_b,
           conv_ln_g, conv_ln_b, w_out, norm2_g, w_up, w_down, final_g):
    assert state_hgrn.shape[0] == 1, "single-layer stack"
    bsz, seq, d = x_prompt.shape
    row = lambda a: a.reshape(1, -1)
    n1g, og, cb, lng, lnb, n2g, fg = (row(norm1_g[0]), row(hg_onorm_g[0]), row(conv_b[0]), row(conv_ln_g[0]),
                                      row(conv_ln_b[0]), row(norm2_g[0]), row(final_g))
    cw = conv_w[0]

    assert x_sample.shape[1] == 1, "one new token per running sequence"
    x_s = x_sample.reshape(x_sample.shape[0], d)
    y_p, s_p, c_p, s_s, c_s, y_s = _layer(x_prompt, meta_tokens, hg_lb, n1g, w_in, og, cw, cb, lng, lnb, w_out, n2g,
                                          w_up, w_down, fg, x_s, state_hgrn[0], jnp.swapaxes(state_conv[0], 0, 1))

    return (y_p, y_s.reshape(x_sample.shape), s_p[None], c_p[None], s_s[None], jnp.swapaxes(c_s, 0, 1)[None])
```

```python
import functools

import jax
import jax.numpy as jnp
from jax import lax
from jax.experimental import pallas as pl
from jax.experimental.pallas import tpu as pltpu

F32 = jnp.float32
BF16 = jnp.bfloat16

N_META = 16
HG_HEADS = 4
HG_DK = 128
HG_DV = 128
HG_WIDTH = HG_HEADS * HG_DK
CV_WIDTH = 512
CONV_K = 31
HG_CHUNK = 128
EPS = 1e-6
C_Q, C_F, C_I, C_G, C_A, C_B, C_END = (0, HG_WIDTH, 2 * HG_WIDTH, 3 * HG_WIDTH, 4 * HG_WIDTH,
                                       4 * HG_WIDTH + CV_WIDTH, 4 * HG_WIDTH + 2 * CV_WIDTH)
HIST = 32
MXU_COLS = 256
CUMSUM_ROWS = 256
W_ROWS, W_COLS = 256, 1024
TILE_T = 512
SAMPLE_SEQS = 8
SAMPLE_HEADS = 2
VMEM_LIMIT = 62 * 1024 * 1024


def _rms(x, g):
    return x * lax.rsqrt(jnp.mean(x * x, axis=-1, keepdims=True) + EPS) * g


def _layernorm(x, g, b):
    xc = x - jnp.mean(x, axis=-1, keepdims=True)
    return xc * lax.rsqrt(jnp.mean(xc * xc, axis=-1, keepdims=True) + EPS) * g + b


def _silu(x):
    return x * jax.nn.sigmoid(x)


def _lower_bound(lbp):
    e = jnp.exp(lbp - jnp.max(lbp, axis=0, keepdims=True))
    return e[0:1] / jnp.sum(e, axis=0, keepdims=True)


def _project(x, n1g, win_ref):
    hn = _rms(x, n1g).astype(BF16)
    return jnp.dot(hn, win_ref[...], preferred_element_type=F32)


def _tril(n):
    r = lax.broadcasted_iota(jnp.int32, (n, n), 0)
    c = lax.broadcasted_iota(jnp.int32, (n, n), 1)
    return r >= c


def _cumsum_rows(x):
    n = x.shape[0]
    return jnp.dot(_tril(n).astype(F32), x, precision=lax.Precision.HIGHEST, preferred_element_type=F32)


def _chunk_cumsum(x):
    n = x.shape[0]
    r = lax.broadcasted_iota(jnp.int32, (n, n), 0)
    c = lax.broadcasted_iota(jnp.int32, (n, n), 1)
    tri = ((r >= c) & (r // HG_CHUNK == c // HG_CHUNK)).astype(BF16)
    hi = x.astype(BF16)
    rest = x - hi.astype(F32)
    mid = rest.astype(BF16)
    lo = (rest - mid.astype(F32)).astype(BF16)
    return (jnp.dot(tri, hi, preferred_element_type=F32) + jnp.dot(tri, mid, preferred_element_type=F32)
            + jnp.dot(tri, lo, preferred_element_type=F32))


def _head(h):
    return slice(h * HG_DK, (h + 1) * HG_DK)


def _dot_nt(a, b):
    return lax.dot_general(a, b, (((1,), (1,)), ((), ())), preferred_element_type=F32)


def _dot_tn(a, b):
    return lax.dot_general(a, b, (((0,), (0,)), ((), ())), preferred_element_type=F32)


def _mlp(h, n2g, wup_ref, wdn_ref, fg):
    hn = _rms(h, n2g).astype(BF16)
    u = jnp.dot(hn, wup_ref[...], preferred_element_type=F32)
    a = jnp.square(jnp.maximum(u, 0.0)).astype(BF16)
    h2 = h + jnp.dot(a, wdn_ref[...], preferred_element_type=F32)
    return _rms(h2, fg)


def _dwconv_lane_group(e_ref, cw_ref, bias, base, rows, g):
    sub = 8
    lanes = slice(g * 128, (g + 1) * 128)
    ext = e_ref[base - HIST:base + rows, lanes]
    acc = None
    for r in range(sub):
        u = None
        for a in range(-(-CONV_K // sub)):
            d = sub * a + r
            if d < CONV_K:
                lo = HIST - sub - sub * a
                term = cw_ref[CONV_K - 1 - d:CONV_K - d, lanes] * ext[lo:lo + rows + sub]
                u = term if u is None else u + term
        shifted = u[sub:] if r == 0 else pltpu.roll(u, r, 0)[sub:]
        acc = shifted if acc is None else acc + shifted
    return acc + bias[:, lanes]


def _layer_kernel(x_ref, meta_ref, lbp_ref, n1g_ref, win_hbm, og_ref, cw_ref, cb_ref, lng_ref, lnb_ref,
                  wout_hbm, n2g_ref, wup_hbm, wdn_hbm, fg_ref, sx_ref, sst_ref, ssc_ref,
                  y_out_ref, sfin_ref, cfin_ref, snst_ref, snsc_ref, ys_ref,
                  win_ref, wout_ref, wup_ref, wdn_ref, wsem,
                  z_ref, b_ref, y_ref, st_ref, mst_ref, mglu_ref, e_ref, h_ref, hn_ref, a_ref,
                  sft_ref, sqt_ref, sv_ref, sglu_ref, sgate_ref, so_ref, sdw_ref, *, tile_t, nt):
    s = pl.program_id(0)
    ns = pl.num_programs(0) - 1
    mixing = s < ns
    t = lax.rem(jnp.minimum(s, ns - 1), nt)
    lb = _lower_bound(lbp_ref[...])
    n1g = n1g_ref[...]

    @pl.when(s == 0)
    def _first_step():
        _load_weights_as_bf16(((win_hbm, win_ref), (wout_hbm, wout_ref), (wup_hbm, wup_ref), (wdn_hbm, wdn_ref)),
                              z_ref, wsem)
        h_ref[...] = jnp.zeros(h_ref.shape, F32)
        zm = _project(meta_ref[...], n1g, win_ref)
        fm = lb + (1.0 - lb) * jax.nn.sigmoid(zm[:, C_F:C_I])
        bm = _cumsum_rows(jnp.log(fm))
        k_end = ((1.0 - fm) * jnp.exp(bm[N_META - 1:N_META] - bm)).astype(BF16)
        vm = zm[:, C_I:C_G].astype(BF16)
        for h in range(HG_HEADS):
            mst_ref[h] = _dot_tn(vm[:, _head(h)], k_end[:, _head(h)])
        mglu_ref[...] = zm[:, C_A:C_B] * jax.nn.sigmoid(zm[:, C_B:C_END])
        zs = _project(sx_ref[...], n1g, win_ref)
        fs = lb + (1.0 - lb) * jax.nn.sigmoid(zs[:, C_F:C_I])
        qs = _silu(zs[:, C_Q:C_F])
        for h in range(HG_HEADS):
            sft_ref[h] = fs[:, _head(h)].T
            sqt_ref[h] = qs[:, _head(h)].T
        pair = SAMPLE_HEADS * HG_DV
        for p in range(HG_HEADS // SAMPLE_HEADS):
            sv_ref[p] = zs[:, C_I + p * pair:C_I + (p + 1) * pair]
        sglu_ref[...] = zs[:, C_A:C_B] * jax.nn.sigmoid(zs[:, C_B:C_END])
        sgate_ref[...] = _silu(zs[:, C_G:C_A])

    @pl.when(mixing & (t == 0))
    def _start_of_batch_row():
        st_ref[...] = mst_ref[...]
        e_ref[0:HIST - N_META, :] = jnp.zeros((HIST - N_META, CV_WIDTH), F32)
        e_ref[HIST - N_META:HIST, :] = mglu_ref[...]

    cb = cb_ref[...]
    z_ref[...] = _project(x_ref[0], n1g, win_ref)
    hn_ref[...] = _rms(h_ref[...], n2g_ref[...]).astype(BF16)
    y_out_ref[0] = h_ref[...]
    for r0 in range(0, tile_t, CUMSUM_ROWS):
        rows = slice(r0, r0 + CUMSUM_ROWS)
        f = lb + (1.0 - lb) * jax.nn.sigmoid(z_ref[rows, C_F:C_I])
        z_ref[rows, C_F:C_I] = f
        b_ref[rows, :] = _chunk_cumsum(jnp.log(f))
    _sample_state_update(s, sst_ref, ssc_ref, snst_ref, snsc_ref, cw_ref, cb, sft_ref, sqt_ref, sv_ref, sglu_ref,
                         so_ref, sdw_ref)
    causal = _tril(HG_CHUNK)
    og = og_ref[...]
    ff = wup_ref.shape[1] // (tile_t // HG_CHUNK)
    dt = MXU_COLS
    assert (CV_WIDTH // 128) * dt == y_out_ref.shape[2]

    for c in range(tile_t // HG_CHUNK):
        rows = slice(c * HG_CHUNK, (c + 1) * HG_CHUNK)
        f = z_ref[rows, C_F:C_I]
        k = 1.0 - f
        b = b_ref[rows, :]
        b_mid = b[HG_CHUNK // 2 - 1:HG_CHUNK // 2]
        b_end = b[HG_CHUNK - 1:HG_CHUNK]
        q = _silu(z_ref[rows, C_Q:C_F])
        q_rel = (q * jnp.exp(b - b_mid)).astype(BF16)
        k_rel = (k * jnp.exp(b_mid - b)).astype(BF16)
        q_abs = (q * jnp.exp(b)).astype(BF16)
        k_end = (k * jnp.exp(b_end - b)).astype(BF16)
        decay = jnp.exp(b_end)
        v = z_ref[rows, C_I:C_G].astype(BF16)
        gate = _silu(z_ref[rows, C_G:C_A])
        for h in range(HG_HEADS):
            hs = _head(h)
            scores = jnp.where(causal, _dot_nt(q_rel[:, hs], k_rel[:, hs]), 0.0)
            s_t = st_ref[h]
            o = (jnp.dot(scores.astype(BF16), v[:, hs], preferred_element_type=F32)
                 + _dot_nt(q_abs[:, hs], s_t.astype(BF16)))
            st_ref[h] = s_t * decay[:, hs] + _dot_tn(v[:, hs], k_end[:, hs])
            y_ref[rows, hs] = (_rms(o, og) * gate[:, hs]).astype(BF16)
        for n in range(ff // MXU_COLS):
            ucols = slice(c * ff + n * MXU_COLS, c * ff + (n + 1) * MXU_COLS)
            u = jnp.dot(hn_ref[...], wup_ref[:, ucols], preferred_element_type=F32)
            a_ref[:, n * MXU_COLS:(n + 1) * MXU_COLS] = jnp.square(jnp.maximum(u, 0.0)).astype(BF16)
        e_ref[HIST + c * HG_CHUNK:HIST + (c + 1) * HG_CHUNK, :] = (z_ref[rows, C_A:C_B]
                                                                   * jax.nn.sigmoid(z_ref[rows, C_B:C_END]))
        dw = []
        for g in range(CV_WIDTH // 128):
            dw.append(_dwconv_lane_group(e_ref, cw_ref, cb, HIST + c * HG_CHUNK, HG_CHUNK, g))
            dcols = slice(g * dt, (g + 1) * dt)
            y_out_ref[0, :, dcols] += jnp.dot(a_ref[...], wdn_ref[c * ff:(c + 1) * ff, dcols],
                                              preferred_element_type=F32)
        dw = jnp.concatenate(dw, axis=-1)
        y_ref[rows, HG_WIDTH:] = _silu(_layernorm(dw, lng_ref[...], lnb_ref[...])).astype(BF16)

    y_out_ref[0] = _rms(y_out_ref[0], fg_ref[...])
    h_ref[...] = x_ref[0] + jnp.dot(y_ref[...], wout_ref[...], preferred_element_type=F32)
    e_ref[0:HIST, :] = e_ref[tile_t:tile_t + HIST, :]

    @pl.when(mixing & (t == nt - 1))
    def _final_state():
        for h in range(HG_HEADS):
            sfin_ref[0, h] = st_ref[h].T
        cfin_ref[0] = e_ref[tile_t + HIST - (CONV_K - 1):tile_t + HIST, :]

    @pl.when(s == ns)
    def _sample_outputs():
        y_hg = []
        for h in range(HG_HEADS):
            p, hh = divmod(h, SAMPLE_HEADS)
            o_h = so_ref[p, :, hh * HG_DV:(hh + 1) * HG_DV]
            y_hg.append((_rms(o_h, og) * sgate_ref[:, _head(h)]).astype(BF16))
        y_cv = _silu(_layernorm(sdw_ref[...], lng_ref[...], lnb_ref[...])).astype(BF16)
        ycat = jnp.concatenate(y_hg + [y_cv], axis=-1)
        hs = sx_ref[...] + jnp.dot(ycat, wout_ref[...], preferred_element_type=F32)
        ys_ref[...] = _mlp(hs, n2g_ref[...], wup_ref, wdn_ref, fg_ref[...])


def _weight_slots(stage_ref):
    per_row = stage_ref.shape[1] // W_COLS
    n_slot = (stage_ref.shape[0] // W_ROWS) * per_row

    def slot(k):
        r, c = divmod(k, per_row)
        return stage_ref.at[pl.ds(r * W_ROWS, W_ROWS), pl.ds(c * W_COLS, W_COLS)]

    return n_slot, slot


def _load_weights_as_bf16(pairs, stage_ref, sem):
    n_slot, slot = _weight_slots(stage_ref)
    blocks = [(src, dst, r, c) for src, dst in pairs
              for r in range(0, dst.shape[0], W_ROWS) for c in range(0, dst.shape[1], W_COLS)]

    def copy(i):
        src, _, r, c = blocks[i]
        k = i % n_slot
        return pltpu.make_async_copy(src.at[0, pl.ds(r, W_ROWS), pl.ds(c, W_COLS)], slot(k), sem.at[k])

    for i in range(min(n_slot, len(blocks))):
        copy(i).start()
    for i, (_, dst, r, c) in enumerate(blocks):
        copy(i).wait()
        dst[r:r + W_ROWS, c:c + W_COLS] = slot(i % n_slot)[...].astype(BF16)
        if i + n_slot < len(blocks):
            copy(i + n_slot).start()


def _sample_units(s, nseq):
    n_pairs = HG_HEADS // SAMPLE_HEADS
    unit = jnp.minimum(s, nseq // SAMPLE_SEQS * n_pairs - 1)
    return unit // n_pairs, unit % n_pairs, jnp.minimum(s, nseq // SAMPLE_SEQS - 1)


def _sample_state_update(s, sst_ref, ssc_ref, snst_ref, snsc_ref, cw_ref, cb, sft_ref, sqt_ref, sv_ref, sglu_ref,
                         so_ref, sdw_ref):
    nseq = sft_ref.shape[2]
    sb, hp, cblk = _sample_units(s, nseq)
    r0 = pl.multiple_of(sb * SAMPLE_SEQS, SAMPLE_SEQS)
    shift = lax.rem(nseq - r0, nseq)
    v_rows = sv_ref[hp, pl.ds(r0, SAMPLE_SEQS), :]
    seq_row = lax.broadcasted_iota(jnp.int32, (SAMPLE_SEQS, HG_DV), 0)
    for hh in range(SAMPLE_HEADS):
        h = hp * SAMPLE_HEADS + hh
        f_t = pltpu.roll(sft_ref[h], shift, 1)
        q_t = pltpu.roll(sqt_ref[h], shift, 1)
        o = jnp.zeros((SAMPLE_SEQS, HG_DV), F32)
        for j in range(SAMPLE_SEQS):
            f_b = jnp.broadcast_to(f_t[:, j:j + 1], (HG_DK, HG_DV))
            s_new = f_b * sst_ref[j, hh] + (1.0 - f_b) * v_rows[j:j + 1, hh * HG_DV:(hh + 1) * HG_DV]
            snst_ref[j, hh] = s_new
            q_b = jnp.broadcast_to(q_t[:, j:j + 1], (HG_DK, HG_DV))
            o = jnp.where(seq_row == j, jnp.sum(q_b * s_new, axis=0, keepdims=True), o)
        so_ref[hp, pl.ds(r0, SAMPLE_SEQS), hh * HG_DV:(hh + 1) * HG_DV] = o
    c0 = pl.multiple_of(cblk * SAMPLE_SEQS, SAMPLE_SEQS)
    glu = sglu_ref[pl.ds(c0, SAMPLE_SEQS), :]
    taps = CONV_K - 1
    dw = cb + cw_ref[taps:taps + 1, :] * glu
    for j in range(taps):
        dw = dw + cw_ref[j:j + 1, :] * ssc_ref[j]
    sdw_ref[pl.ds(c0, SAMPLE_SEQS), :] = dw
    snsc_ref[0:taps - 1] = ssc_ref[1:taps]
    snsc_ref[taps - 1] = glu


def _const_spec(shape, single=False):
    zeros = (0,) * len(shape)
    if single:
        return pl.BlockSpec(shape, lambda *_: zeros, pipeline_mode=pl.Buffered(1))
    return pl.BlockSpec(shape, lambda *_: zeros)


def _layer(x, meta, lbp, n1g, win, og, cw, cb, lng, lnb, wout, n2g, wup, wdn, fg, xs, sst, ssc):
    bsz, seq, d = x.shape
    nt = seq // TILE_T
    ns = bsz * nt
    nseq = xs.shape[0]
    n_pairs = HG_HEADS // SAMPLE_HEADS
    assert nseq // SAMPLE_SEQS * n_pairs <= ns, "one (sequence block, head pair) per grid step"

    def mix_tile(s):
        s1 = jnp.minimum(s, ns - 1)
        return s1 // nt, s1 % nt

    def mlp_tile(s):
        s2 = jnp.maximum(s - 1, 0)
        return s2 // nt, s2 % nt

    state_spec = pl.BlockSpec((SAMPLE_SEQS, SAMPLE_HEADS, HG_DK, HG_DV),
                              lambda s: (*_sample_units(s, nseq)[:2], 0, 0))
    conv_spec = pl.BlockSpec((CONV_K - 1, SAMPLE_SEQS, CV_WIDTH), lambda s: (0, _sample_units(s, nseq)[2], 0))
    pair = SAMPLE_HEADS * HG_DV
    hbm = pl.BlockSpec(memory_space=pl.ANY)
    n_slot = (TILE_T // W_ROWS) * (C_END // W_COLS)

    return pl.pallas_call(
        functools.partial(_layer_kernel, tile_t=TILE_T, nt=nt),
        grid=(ns + 1,),
        in_specs=[
            pl.BlockSpec((1, TILE_T, d), lambda s: (*mix_tile(s), 0)),
            _const_spec(meta.shape), _const_spec(lbp.shape), _const_spec(n1g.shape),
            hbm, _const_spec(og.shape), _const_spec(cw.shape),
            _const_spec(cb.shape), _const_spec(lng.shape), _const_spec(lnb.shape),
            hbm, _const_spec(n2g.shape), hbm, hbm, _const_spec(fg.shape),
            _const_spec(xs.shape, single=True), state_spec, conv_spec,
        ],
        out_specs=[
            pl.BlockSpec((1, TILE_T, d), lambda s: (*mlp_tile(s), 0)),
            pl.BlockSpec((1, HG_HEADS, HG_DK, HG_DV), lambda s: (mix_tile(s)[0], 0, 0, 0)),
            pl.BlockSpec((1, CONV_K - 1, CV_WIDTH), lambda s: (mix_tile(s)[0], 0, 0)),
            state_spec, conv_spec, _const_spec(xs.shape),
        ],
        out_shape=[
            jax.ShapeDtypeStruct((bsz, seq, d), F32),
            jax.ShapeDtypeStruct((bsz, HG_HEADS, HG_DK, HG_DV), F32),
            jax.ShapeDtypeStruct((bsz, CONV_K - 1, CV_WIDTH), F32),
            jax.ShapeDtypeStruct(sst.shape, F32),
            jax.ShapeDtypeStruct(ssc.shape, F32),
            jax.ShapeDtypeStruct(xs.shape, F32),
        ],
        scratch_shapes=[
            pltpu.VMEM(win.shape[1:], BF16), pltpu.VMEM(wout.shape[1:], BF16),
            pltpu.VMEM(wup.shape[1:], BF16), pltpu.VMEM(wdn.shape[1:], BF16),
            pltpu.SemaphoreType.DMA((n_slot,)),
            pltpu.VMEM((TILE_T, C_END), F32),
            pltpu.VMEM((TILE_T, HG_WIDTH), F32),
            pltpu.VMEM((TILE_T, HG_WIDTH + CV_WIDTH), BF16),
            pltpu.VMEM((HG_HEADS, HG_DV, HG_DK), F32),
            pltpu.VMEM((HG_HEADS, HG_DV, HG_DK), F32),
            pltpu.VMEM((N_META, CV_WIDTH), F32),
            pltpu.VMEM((HIST + TILE_T, CV_WIDTH), F32),
            pltpu.VMEM((TILE_T, d), F32),
            pltpu.VMEM((TILE_T, d), BF16),
            pltpu.VMEM((TILE_T, wup.shape[2] // (TILE_T // HG_CHUNK)), BF16),
            pltpu.VMEM((HG_HEADS, HG_DK, nseq), F32),
            pltpu.VMEM((HG_HEADS, HG_DK, nseq), F32),
            pltpu.VMEM((n_pairs, nseq, pair), F32),
            pltpu.VMEM((nseq, CV_WIDTH), F32),
            pltpu.VMEM((nseq, HG_WIDTH), F32),
            pltpu.VMEM((n_pairs, nseq, pair), F32),
            pltpu.VMEM((nseq, CV_WIDTH), F32),
        ],
        compiler_params=pltpu.CompilerParams(dimension_semantics=("arbitrary",), vmem_limit_bytes=VMEM_LIMIT),
        name="layer",
    )(x, meta, lbp, n1g, win, og, cw, cb, lng, lnb, wout, n2g, wup, wdn, fg, xs, sst, ssc)


def kernel(x_prompt, x_sample, state_hgrn, state_conv, meta_tokens, hg_lb, norm1_g, w_in, hg_onorm_g, conv_w, conv_b,
           conv_ln_g, conv_ln_b, w_out, norm2_g, w_up, w_down, final_g):
    assert state_hgrn.shape[0] == 1, "single-layer stack"
    bsz, seq, d = x_prompt.shape
    row = lambda a: a.reshape(1, -1)
    n1g, og, cb, lng, lnb, n2g, fg = (row(norm1_g[0]), row(hg_onorm_g[0]), row(conv_b[0]), row(conv_ln_g[0]),
                                      row(conv_ln_b[0]), row(norm2_g[0]), row(final_g))
    cw = conv_w[0]

    assert x_sample.shape[1] == 1, "one new token per running sequence"
    x_s = x_sample.reshape(x_sample.shape[0], d)
    y_p, s_p, c_p, s_s, c_s, y_s = _layer(x_prompt, meta_tokens, hg_lb, n1g, w_in, og, cw, cb, lng, lnb, w_out, n2g,
                                          w_up, w_down, fg, x_s, state_hgrn[0], jnp.swapaxes(state_conv[0], 0, 1))

    return (y_p, y_s.reshape(x_sample.shape), s_p[None], c_p[None], s_s[None], jnp.swapaxes(c_s, 0, 1)[None])
```

```python
import functools

import jax
import jax.numpy as jnp
from jax import lax
from jax.experimental import pallas as pl
from jax.experimental.pallas import tpu as pltpu

F32 = jnp.float32
BF16 = jnp.bfloat16

N_META = 16
HG_HEADS = 4
HG_DK = 128
HG_DV = 128
HG_WIDTH = HG_HEADS * HG_DK
CV_WIDTH = 512
CONV_K = 31
HG_CHUNK = 128
EPS = 1e-6
C_Q, C_F, C_I, C_G, C_A, C_B, C_END = (0, HG_WIDTH, 2 * HG_WIDTH, 3 * HG_WIDTH, 4 * HG_WIDTH,
                                       4 * HG_WIDTH + CV_WIDTH, 4 * HG_WIDTH + 2 * CV_WIDTH)
HIST = 32
MXU_COLS = 256
CUMSUM_ROWS = 256
W_ROWS, W_COLS = 256, 1024
TILE_T = 512
SAMPLE_SEQS = 8
SAMPLE_HEADS = 2
VMEM_LIMIT = 62 * 1024 * 1024


def _rms(x, g):
    return x * lax.rsqrt(jnp.mean(x * x, axis=-1, keepdims=True) + EPS) * g


def _layernorm(x, g, b):
    xc = x - jnp.mean(x, axis=-1, keepdims=True)
    return xc * lax.rsqrt(jnp.mean(xc * xc, axis=-1, keepdims=True) + EPS) * g + b


def _silu(x):
    return x * jax.nn.sigmoid(x)


def _lower_bound(lbp):
    e = jnp.exp(lbp - jnp.max(lbp, axis=0, keepdims=True))
    return e[0:1] / jnp.sum(e, axis=0, keepdims=True)


def _project(x, n1g, win_ref):
    hn = _rms(x, n1g).astype(BF16)
    return jnp.dot(hn, win_ref[...], preferred_element_type=F32)


def _tril(n):
    r = lax.broadcasted_iota(jnp.int32, (n, n), 0)
    c = lax.broadcasted_iota(jnp.int32, (n, n), 1)
    return r >= c


def _cumsum_rows(x):
    n = x.shape[0]
    return jnp.dot(_tril(n).astype(F32), x, precision=lax.Precision.HIGHEST, preferred_element_type=F32)


def _chunk_cumsum(x):
    n = x.shape[0]
    r = lax.broadcasted_iota(jnp.int32, (n, n), 0)
    c = lax.broadcasted_iota(jnp.int32, (n, n), 1)
    tri = ((r >= c) & (r // HG_CHUNK == c // HG_CHUNK)).astype(BF16)
    hi = x.astype(BF16)
    rest = x - hi.astype(F32)
    mid = rest.astype(BF16)
    lo = (rest - mid.astype(F32)).astype(BF16)
    return (jnp.dot(tri, hi, preferred_element_type=F32) + jnp.dot(tri, mid, preferred_element_type=F32)
            + jnp.dot(tri, lo, preferred_element_type=F32))


def _head(h):
    return slice(h * HG_DK, (h + 1) * HG_DK)


def _dot_nt(a, b):
    return lax.dot_general(a, b, (((1,), (1,)), ((), ())), preferred_element_type=F32)


def _dot_tn(a, b):
    return lax.dot_general(a, b, (((0,), (0,)), ((), ())), preferred_element_type=F32)


def _mlp(h, n2g, wup_ref, wdn_ref, fg):
    hn = _rms(h, n2g).astype(BF16)
    u = jnp.dot(hn, wup_ref[...], preferred_element_type=F32)
    a = jnp.square(jnp.maximum(u, 0.0)).astype(BF16)
    h2 = h + jnp.dot(a, wdn_ref[...], preferred_element_type=F32)
    return _rms(h2, fg)


def _dwconv_lane_group(e_ref, cw_ref, bias, base, rows, g):
    sub = 8
    lanes = slice(g * 128, (g + 1) * 128)
    ext = e_ref[base - HIST:base + rows, lanes]
    acc = None
    for r in range(sub):
        u = None
        for a in range(-(-CONV_K // sub)):
            d = sub * a + r
            if d < CONV_K:
                lo = HIST - sub - sub * a
                term = cw_ref[CONV_K - 1 - d:CONV_K - d, lanes] * ext[lo:lo + rows + sub]
                u = term if u is None else u + term
        shifted = u[sub:] if r == 0 else pltpu.roll(u, r, 0)[sub:]
        acc = shifted if acc is None else acc + shifted
    return acc + bias[:, lanes]


def _layer_kernel(x_ref, meta_ref, lbp_ref, n1g_ref, win_hbm, og_ref, cw_ref, cb_ref, lng_ref, lnb_ref,
                  wout_hbm, n2g_ref, wup_hbm, wdn_hbm, fg_ref, sx_ref, sst_ref, ssc_ref,
                  y_out_ref, sfin_ref, cfin_ref, snst_ref, snsc_ref, ys_ref,
                  win_ref, wout_ref, wup_ref, wdn_ref, wsem,
                  z_ref, b_ref, y_ref, st_ref, mst_ref, mglu_ref, e_ref, h_ref, hn_ref, a_ref,
                  sft_ref, sqt_ref, sv_ref, sglu_ref, sgate_ref, so_ref, sdw_ref, *, tile_t, nt):
    s = pl.program_id(0)
    ns = pl.num_programs(0) - 1
    mixing = s < ns
    t = lax.rem(jnp.minimum(s, ns - 1), nt)
    lb = _lower_bound(lbp_ref[...])
    n1g = n1g_ref[...]

    @pl.when(s == 0)
    def _first_step():
        _load_weights_as_bf16(((win_hbm, win_ref), (wout_hbm, wout_ref), (wup_hbm, wup_ref), (wdn_hbm, wdn_ref)),
                              z_ref, wsem)
        h_ref[...] = jnp.zeros(h_ref.shape, F32)
        zm = _project(meta_ref[...], n1g, win_ref)
        fm = lb + (1.0 - lb) * jax.nn.sigmoid(zm[:, C_F:C_I])
        bm = _cumsum_rows(jnp.log(fm))
        k_end = ((1.0 - fm) * jnp.exp(bm[N_META - 1:N_META] - bm)).astype(BF16)
        vm = zm[:, C_I:C_G].astype(BF16)
        for h in range(HG_HEADS):
            mst_ref[h] = _dot_tn(vm[:, _head(h)], k_end[:, _head(h)])
        mglu_ref[...] = zm[:, C_A:C_B] * jax.nn.sigmoid(zm[:, C_B:C_END])
        zs = _project(sx_ref[...], n1g, win_ref)
        fs = lb + (1.0 - lb) * jax.nn.sigmoid(zs[:, C_F:C_I])
        qs = _silu(zs[:, C_Q:C_F])
        for h in range(HG_HEADS):
            sft_ref[h] = fs[:, _head(h)].T
            sqt_ref[h] = qs[:, _head(h)].T
        pair = SAMPLE_HEADS * HG_DV
        for p in range(HG_HEADS // SAMPLE_HEADS):
            sv_ref[p] = zs[:, C_I + p * pair:C_I + (p + 1) * pair]
        sglu_ref[...] = zs[:, C_A:C_B] * jax.nn.sigmoid(zs[:, C_B:C_END])
        sgate_ref[...] = _silu(zs[:, C_G:C_A])

    @pl.when(mixing & (t == 0))
    def _start_of_batch_row():
        st_ref[...] = mst_ref[...]
        e_ref[0:HIST - N_META, :] = jnp.zeros((HIST - N_META, CV_WIDTH), F32)
        e_ref[HIST - N_META:HIST, :] = mglu_ref[...]

    cb = cb_ref[...]
    _sample_state_update(s, sst_ref, ssc_ref, snst_ref, snsc_ref, cw_ref, cb, sft_ref, sqt_ref, sv_ref, sglu_ref,
                         so_ref, sdw_ref)
    z_ref[...] = _project(x_ref[0], n1g, win_ref)
    hn_ref[...] = _rms(h_ref[...], n2g_ref[...]).astype(BF16)
    y_out_ref[0] = h_ref[...]
    for r0 in range(0, tile_t, CUMSUM_ROWS):
        rows = slice(r0, r0 + CUMSUM_ROWS)
        f = lb + (1.0 - lb) * jax.nn.sigmoid(z_ref[rows, C_F:C_I])
        z_ref[rows, C_F:C_I] = f
        b_ref[rows, :] = _chunk_cumsum(jnp.log(f))
    causal = _tril(HG_CHUNK)
    og = og_ref[...]
    ff = wup_ref.shape[1] // (tile_t // HG_CHUNK)
    dt = MXU_COLS
    assert (CV_WIDTH // 128) * dt == y_out_ref.shape[2]

    for c in range(tile_t // HG_CHUNK):
        rows = slice(c * HG_CHUNK, (c + 1) * HG_CHUNK)
        f = z_ref[rows, C_F:C_I]
        k = 1.0 - f
        b = b_ref[rows, :]
        b_mid = b[HG_CHUNK // 2 - 1:HG_CHUNK // 2]
        b_end = b[HG_CHUNK - 1:HG_CHUNK]
        q = _silu(z_ref[rows, C_Q:C_F])
        q_rel = (q * jnp.exp(b - b_mid)).astype(BF16)
        k_rel = (k * jnp.exp(b_mid - b)).astype(BF16)
        q_abs = (q * jnp.exp(b)).astype(BF16)
        k_end = (k * jnp.exp(b_end - b)).astype(BF16)
        decay = jnp.exp(b_end)
        v = z_ref[rows, C_I:C_G].astype(BF16)
        gate = _silu(z_ref[rows, C_G:C_A])
        for h in range(HG_HEADS):
            hs = _head(h)
            scores = jnp.where(causal, _dot_nt(q_rel[:, hs], k_rel[:, hs]), 0.0)
            s_t = st_ref[h]
            o = (jnp.dot(scores.astype(BF16), v[:, hs], preferred_element_type=F32)
                 + _dot_nt(q_abs[:, hs], s_t.astype(BF16)))
            st_ref[h] = s_t * decay[:, hs] + _dot_tn(v[:, hs], k_end[:, hs])
            y_ref[rows, hs] = (_rms(o, og) * gate[:, hs]).astype(BF16)
        for n in range(ff // MXU_COLS):
            ucols = slice(c * ff + n * MXU_COLS, c * ff + (n + 1) * MXU_COLS)
            u = jnp.dot(hn_ref[...], wup_ref[:, ucols], preferred_element_type=F32)
            a_ref[:, n * MXU_COLS:(n + 1) * MXU_COLS] = jnp.square(jnp.maximum(u, 0.0)).astype(BF16)
        e_ref[HIST + c * HG_CHUNK:HIST + (c + 1) * HG_CHUNK, :] = (z_ref[rows, C_A:C_B]
                                                                   * jax.nn.sigmoid(z_ref[rows, C_B:C_END]))
        dw = []
        for g in range(CV_WIDTH // 128):
            dw.append(_dwconv_lane_group(e_ref, cw_ref, cb, HIST + c * HG_CHUNK, HG_CHUNK, g))
            dcols = slice(g * dt, (g + 1) * dt)
            y_out_ref[0, :, dcols] += jnp.dot(a_ref[...], wdn_ref[c * ff:(c + 1) * ff, dcols],
                                              preferred_element_type=F32)
        dw = jnp.concatenate(dw, axis=-1)
        y_ref[rows, HG_WIDTH:] = _silu(_layernorm(dw, lng_ref[...], lnb_ref[...])).astype(BF16)

    y_out_ref[0] = _rms(y_out_ref[0], fg_ref[...])
    h_ref[...] = x_ref[0] + jnp.dot(y_ref[...], wout_ref[...], preferred_element_type=F32)
    e_ref[0:HIST, :] = e_ref[tile_t:tile_t + HIST, :]

    @pl.when(mixing & (t == nt - 1))
    def _final_state():
        for h in range(HG_HEADS):
            sfin_ref[0, h] = st_ref[h].T
        cfin_ref[0] = e_ref[tile_t + HIST - (CONV_K - 1):tile_t + HIST, :]

    @pl.when(s == ns)
    def _sample_outputs():
        y_hg = []
        for h in range(HG_HEADS):
            p, hh = divmod(h, SAMPLE_HEADS)
            o_h = so_ref[p, :, hh * HG_DV:(hh + 1) * HG_DV]
            y_hg.append((_rms(o_h, og) * sgate_ref[:, _head(h)]).astype(BF16))
        y_cv = _silu(_layernorm(sdw_ref[...], lng_ref[...], lnb_ref[...])).astype(BF16)
        ycat = jnp.concatenate(y_hg + [y_cv], axis=-1)
        hs = sx_ref[...] + jnp.dot(ycat, wout_ref[...], preferred_element_type=F32)
        ys_ref[...] = _mlp(hs, n2g_ref[...], wup_ref, wdn_ref, fg_ref[...])


def _weight_slots(stage_ref):
    per_row = stage_ref.shape[1] // W_COLS
    n_slot = (stage_ref.shape[0] // W_ROWS) * per_row

    def slot(k):
        r, c = divmod(k, per_row)
        return stage_ref.at[pl.ds(r * W_ROWS, W_ROWS), pl.ds(c * W_COLS, W_COLS)]

    return n_slot, slot


def _load_weights_as_bf16(pairs, stage_ref, sem):
    n_slot, slot = _weight_slots(stage_ref)
    blocks = [(src, dst, r, c) for src, dst in pairs
              for r in range(0, dst.shape[0], W_ROWS) for c in range(0, dst.shape[1], W_COLS)]

    def copy(i):
        src, _, r, c = blocks[i]
        k = i % n_slot
        return pltpu.make_async_copy(src.at[0, pl.ds(r, W_ROWS), pl.ds(c, W_COLS)], slot(k), sem.at[k])

    for i in range(min(n_slot, len(blocks))):
        copy(i).start()
    for i, (_, dst, r, c) in enumerate(blocks):
        copy(i).wait()
        dst[r:r + W_ROWS, c:c + W_COLS] = slot(i % n_slot)[...].astype(BF16)
        if i + n_slot < len(blocks):
            copy(i + n_slot).start()


def _sample_units(s, nseq):
    n_pairs = HG_HEADS // SAMPLE_HEADS
    unit = jnp.minimum(s, nseq // SAMPLE_SEQS * n_pairs - 1)
    return unit // n_pairs, unit % n_pairs, jnp.minimum(s, nseq // SAMPLE_SEQS - 1)


def _sample_state_update(s, sst_ref, ssc_ref, snst_ref, snsc_ref, cw_ref, cb, sft_ref, sqt_ref, sv_ref, sglu_ref,
                         so_ref, sdw_ref):
    nseq = sft_ref.shape[2]
    sb, hp, cblk = _sample_units(s, nseq)
    r0 = pl.multiple_of(sb * SAMPLE_SEQS, SAMPLE_SEQS)
    shift = lax.rem(nseq - r0, nseq)
    v_rows = sv_ref[hp, pl.ds(r0, SAMPLE_SEQS), :]
    seq_row = lax.broadcasted_iota(jnp.int32, (SAMPLE_SEQS, HG_DV), 0)
    for hh in range(SAMPLE_HEADS):
        h = hp * SAMPLE_HEADS + hh
        f_t = pltpu.roll(sft_ref[h], shift, 1)
        q_t = pltpu.roll(sqt_ref[h], shift, 1)
        o = jnp.zeros((SAMPLE_SEQS, HG_DV), F32)
        for j in range(SAMPLE_SEQS):
            f_b = jnp.broadcast_to(f_t[:, j:j + 1], (HG_DK, HG_DV))
            s_new = f_b * sst_ref[j, hh] + (1.0 - f_b) * v_rows[j:j + 1, hh * HG_DV:(hh + 1) * HG_DV]
            snst_ref[j, hh] = s_new
            q_b = jnp.broadcast_to(q_t[:, j:j + 1], (HG_DK, HG_DV))
            o = jnp.where(seq_row == j, jnp.sum(q_b * s_new, axis=0, keepdims=True), o)
        so_ref[hp, pl.ds(r0, SAMPLE_SEQS), hh * HG_DV:(hh + 1) * HG_DV] = o
    c0 = pl.multiple_of(cblk * SAMPLE_SEQS, SAMPLE_SEQS)
    glu = sglu_ref[pl.ds(c0, SAMPLE_SEQS), :]
    taps = CONV_K - 1
    dw = cb + cw_ref[taps:taps + 1, :] * glu
    for j in range(taps):
        dw = dw + cw_ref[j:j + 1, :] * ssc_ref[j]
    sdw_ref[pl.ds(c0, SAMPLE_SEQS), :] = dw
    snsc_ref[0:taps - 1] = ssc_ref[1:taps]
    snsc_ref[taps - 1] = glu


def _const_spec(shape, single=False):
    zeros = (0,) * len(shape)
    if single:
        return pl.BlockSpec(shape, lambda *_: zeros, pipeline_mode=pl.Buffered(1))
    return pl.BlockSpec(shape, lambda *_: zeros)


def _layer(x, meta, lbp, n1g, win, og, cw, cb, lng, lnb, wout, n2g, wup, wdn, fg, xs, sst, ssc):
    bsz, seq, d = x.shape
    nt = seq // TILE_T
    ns = bsz * nt
    nseq = xs.shape[0]
    n_pairs = HG_HEADS // SAMPLE_HEADS
    assert nseq // SAMPLE_SEQS * n_pairs <= ns, "one (sequence block, head pair) per grid step"

    def mix_tile(s):
        s1 = jnp.minimum(s, ns - 1)
        return s1 // nt, s1 % nt

    def mlp_tile(s):
        s2 = jnp.maximum(s - 1, 0)
        return s2 // nt, s2 % nt

    state_spec = pl.BlockSpec((SAMPLE_SEQS, SAMPLE_HEADS, HG_DK, HG_DV),
                              lambda s: (*_sample_units(s, nseq)[:2], 0, 0))
    conv_spec = pl.BlockSpec((CONV_K - 1, SAMPLE_SEQS, CV_WIDTH), lambda s: (0, _sample_units(s, nseq)[2], 0))
    pair = SAMPLE_HEADS * HG_DV
    hbm = pl.BlockSpec(memory_space=pl.ANY)
    n_slot = (TILE_T // W_ROWS) * (C_END // W_COLS)

    return pl.pallas_call(
        functools.partial(_layer_kernel, tile_t=TILE_T, nt=nt),
        grid=(ns + 1,),
        in_specs=[
            pl.BlockSpec((1, TILE_T, d), lambda s: (*mix_tile(s), 0)),
            _const_spec(meta.shape), _const_spec(lbp.shape), _const_spec(n1g.shape),
            hbm, _const_spec(og.shape), _const_spec(cw.shape),
            _const_spec(cb.shape), _const_spec(lng.shape), _const_spec(lnb.shape),
            hbm, _const_spec(n2g.shape), hbm, hbm, _const_spec(fg.shape),
            _const_spec(xs.shape, single=True), state_spec, conv_spec,
        ],
        out_specs=[
            pl.BlockSpec((1, TILE_T, d), lambda s: (*mlp_tile(s), 0)),
            pl.BlockSpec((1, HG_HEADS, HG_DK, HG_DV), lambda s: (mix_tile(s)[0], 0, 0, 0)),
            pl.BlockSpec((1, CONV_K - 1, CV_WIDTH), lambda s: (mix_tile(s)[0], 0, 0)),
            state_spec, conv_spec, _const_spec(xs.shape),
        ],
        out_shape=[
            jax.ShapeDtypeStruct((bsz, seq, d), F32),
            jax.ShapeDtypeStruct((bsz, HG_HEADS, HG_DK, HG_DV), F32),
            jax.ShapeDtypeStruct((bsz, CONV_K - 1, CV_WIDTH), F32),
            jax.ShapeDtypeStruct(sst.shape, F32),
            jax.ShapeDtypeStruct(ssc.shape, F32),
            jax.ShapeDtypeStruct(xs.shape, F32),
        ],
        scratch_shapes=[
            pltpu.VMEM(win.shape[1:], BF16), pltpu.VMEM(wout.shape[1:], BF16),
            pltpu.VMEM(wup.shape[1:], BF16), pltpu.VMEM(wdn.shape[1:], BF16),
            pltpu.SemaphoreType.DMA((n_slot,)),
            pltpu.VMEM((TILE_T, C_END), F32),
            pltpu.VMEM((TILE_T, HG_WIDTH), F32),
            pltpu.VMEM((TILE_T, HG_WIDTH + CV_WIDTH), BF16),
            pltpu.VMEM((HG_HEADS, HG_DV, HG_DK), F32),
            pltpu.VMEM((HG_HEADS, HG_DV, HG_DK), F32),
            pltpu.VMEM((N_META, CV_WIDTH), F32),
            pltpu.VMEM((HIST + TILE_T, CV_WIDTH), F32),
            pltpu.VMEM((TILE_T, d), F32),
            pltpu.VMEM((TILE_T, d), BF16),
            pltpu.VMEM((TILE_T, wup.shape[2] // (TILE_T // HG_CHUNK)), BF16),
            pltpu.VMEM((HG_HEADS, HG_DK, nseq), F32),
            pltpu.VMEM((HG_HEADS, HG_DK, nseq), F32),
            pltpu.VMEM((n_pairs, nseq, pair), F32),
            pltpu.VMEM((nseq, CV_WIDTH), F32),
            pltpu.VMEM((nseq, HG_WIDTH), F32),
            pltpu.VMEM((n_pairs, nseq, pair), F32),
            pltpu.VMEM((nseq, CV_WIDTH), F32),
        ],
        compiler_params=pltpu.CompilerParams(dimension_semantics=("arbitrary",), vmem_limit_bytes=VMEM_LIMIT),
        name="layer",
    )(x, meta, lbp, n1g, win, og, cw, cb, lng, lnb, wout, n2g, wup, wdn, fg, xs, sst, ssc)


def kernel(x_prompt, x_sample, state_hgrn, state_conv, meta_tokens, hg_lb, norm1_g, w_in, hg_onorm_g, conv_w, conv_b,
           conv_ln_g, conv_ln_b, w_out, norm2_g, w_up, w_down, final_g):
    assert state_hgrn.shape[0] == 1, "single-layer stack"
    bsz, seq, d = x_prompt.shape
    row = lambda a: a.reshape(1, -1)
    n1g, og, cb, lng, lnb, n2g, fg = (row(norm1_g[0]), row(hg_onorm_g[0]), row(conv_b[0]), row(conv_ln_g[0]),
                                      row(conv_ln_b[0]), row(norm2_g[0]), row(final_g))
    cw = conv_w[0]

    assert x_sample.shape[1] == 1, "one new token per running sequence"
    x_s = x_sample.reshape(x_sample.shape[0], d)
    y_p, s_p, c_p, s_s, c_s, y_s = _layer(x_prompt, meta_tokens, hg_lb, n1g, w_in, og, cw, cb, lng, lnb, w_out, n2g,
                                          w_up, w_down, fg, x_s, state_hgrn[0], jnp.swapaxes(state_conv[0], 0, 1))

    return (y_p, y_s.reshape(x_sample.shape), s_p[None], c_p[None], s_s[None], jnp.swapaxes(c_s, 0, 1)[None])
```

```python
import functools

import jax
import jax.numpy as jnp
from jax import lax
from jax.experimental import pallas as pl
from jax.experimental.pallas import tpu as pltpu

F32 = jnp.float32
BF16 = jnp.bfloat16

N_META = 16
HG_HEADS = 4
HG_DK = 128
HG_DV = 128
HG_WIDTH = HG_HEADS * HG_DK
CV_WIDTH = 512
CONV_K = 31
HG_CHUNK = 128
EPS = 1e-6
C_Q, C_F, C_I, C_G, C_A, C_B, C_END = (0, HG_WIDTH, 2 * HG_WIDTH, 3 * HG_WIDTH, 4 * HG_WIDTH,
                                       4 * HG_WIDTH + CV_WIDTH, 4 * HG_WIDTH + 2 * CV_WIDTH)
HIST = 32
MXU_COLS = 256
CUMSUM_ROWS = 256
W_ROWS, W_COLS = 256, 1024
TILE_T = 512
SAMPLE_SEQS = 8
SAMPLE_HEADS = 2
VMEM_LIMIT = 62 * 1024 * 1024


def _rms(x, g):
    return x * lax.rsqrt(jnp.mean(x * x, axis=-1, keepdims=True) + EPS) * g


def _layernorm(x, g, b):
    xc = x - jnp.mean(x, axis=-1, keepdims=True)
    return xc * lax.rsqrt(jnp.mean(xc * xc, axis=-1, keepdims=True) + EPS) * g + b


def _silu(x):
    return x * jax.nn.sigmoid(x)


def _lower_bound(lbp):
    e = jnp.exp(lbp - jnp.max(lbp, axis=0, keepdims=True))
    return e[0:1] / jnp.sum(e, axis=0, keepdims=True)


def _project(x, n1g, win_ref):
    hn = _rms(x, n1g).astype(BF16)
    return jnp.dot(hn, win_ref[...], preferred_element_type=F32)


def _tril(n):
    r = lax.broadcasted_iota(jnp.int32, (n, n), 0)
    c = lax.broadcasted_iota(jnp.int32, (n, n), 1)
    return r >= c


def _cumsum_rows(x):
    n = x.shape[0]
    return jnp.dot(_tril(n).astype(F32), x, precision=lax.Precision.HIGHEST, preferred_element_type=F32)


def _chunk_cumsum(x):
    n = x.shape[0]
    r = lax.broadcasted_iota(jnp.int32, (n, n), 0)
    c = lax.broadcasted_iota(jnp.int32, (n, n), 1)
    tri = ((r >= c) & (r // HG_CHUNK == c // HG_CHUNK)).astype(BF16)
    hi = x.astype(BF16)
    rest = x - hi.astype(F32)
    mid = rest.astype(BF16)
    lo = (rest - mid.astype(F32)).astype(BF16)
    return (jnp.dot(tri, hi, preferred_element_type=F32) + jnp.dot(tri, mid, preferred_element_type=F32)
            + jnp.dot(tri, lo, preferred_element_type=F32))


def _head(h):
    return slice(h * HG_DK, (h + 1) * HG_DK)


def _derived_zero(x):
    bits = lax.bitcast_convert_type(x, jnp.uint32)
    return ((bits >> 16) >> 16).astype(F32)


def _dot_nt(a, b):
    return lax.dot_general(a, b, (((1,), (1,)), ((), ())), preferred_element_type=F32)


def _dot_tn(a, b):
    return lax.dot_general(a, b, (((0,), (0,)), ((), ())), preferred_element_type=F32)


def _mlp(h, n2g, wup_ref, wdn_ref, fg):
    hn = _rms(h, n2g).astype(BF16)
    u = jnp.dot(hn, wup_ref[...], preferred_element_type=F32)
    a = jnp.square(jnp.maximum(u, 0.0)).astype(BF16)
    h2 = h + jnp.dot(a, wdn_ref[...], preferred_element_type=F32)
    return _rms(h2, fg)


def _dwconv_lane_group(e_ref, cw_ref, bias, base, rows, g):
    sub = 8
    lanes = slice(g * 128, (g + 1) * 128)
    ext = e_ref[base - HIST:base + rows, lanes]
    acc = None
    for r in range(sub):
        u = None
        for a in range(-(-CONV_K // sub)):
            d = sub * a + r
            if d < CONV_K:
                lo = HIST - sub - sub * a
                term = cw_ref[CONV_K - 1 - d:CONV_K - d, lanes] * ext[lo:lo + rows + sub]
                u = term if u is None else u + term
        shifted = u[sub:] if r == 0 else pltpu.roll(u, r, 0)[sub:]
        acc = shifted if acc is None else acc + shifted
    return acc + bias[:, lanes]


def _layer_kernel(x_ref, meta_ref, lbp_ref, n1g_ref, win_hbm, og_ref, cw_ref, cb_ref, lng_ref, lnb_ref,
                  wout_hbm, n2g_ref, wup_hbm, wdn_hbm, fg_ref, sx_ref, sst_ref, ssc_ref,
                  y_out_ref, sfin_ref, cfin_ref, snst_ref, snsc_ref, ys_ref,
                  win_ref, wout_ref, wup_ref, wdn_ref, wsem,
                  z_ref, b_ref, y_ref, st_ref, mst_ref, mglu_ref, e_ref, h_ref, hn_ref, a_ref,
                  sft_ref, sqt_ref, sv_ref, sglu_ref, sgate_ref, so_ref, sdw_ref, *, tile_t, nt):
    s = pl.program_id(0)
    ns = pl.num_programs(0) - 1
    mixing = s < ns
    t = lax.rem(jnp.minimum(s, ns - 1), nt)
    lb = _lower_bound(lbp_ref[...])
    n1g = n1g_ref[...]

    @pl.when(s == 0)
    def _first_step():
        _load_weights_as_bf16(((win_hbm, win_ref), (wout_hbm, wout_ref), (wup_hbm, wup_ref), (wdn_hbm, wdn_ref)),
                              z_ref, wsem)
        h_ref[...] = jnp.zeros(h_ref.shape, F32)
        zm = _project(meta_ref[...], n1g, win_ref)
        fm = lb + (1.0 - lb) * jax.nn.sigmoid(zm[:, C_F:C_I])
        bm = _cumsum_rows(jnp.log(fm))
        k_end = ((1.0 - fm) * jnp.exp(bm[N_META - 1:N_META] - bm)).astype(BF16)
        vm = zm[:, C_I:C_G].astype(BF16)
        for h in range(HG_HEADS):
            mst_ref[h] = _dot_tn(vm[:, _head(h)], k_end[:, _head(h)])
        mglu_ref[...] = zm[:, C_A:C_B] * jax.nn.sigmoid(zm[:, C_B:C_END])
        zs = _project(sx_ref[...], n1g, win_ref)
        fs = lb + (1.0 - lb) * jax.nn.sigmoid(zs[:, C_F:C_I])
        qs = _silu(zs[:, C_Q:C_F])
        for h in range(HG_HEADS):
            sft_ref[h] = fs[:, _head(h)].T
            sqt_ref[h] = qs[:, _head(h)].T
        pair = SAMPLE_HEADS * HG_DV
        for p in range(HG_HEADS // SAMPLE_HEADS):
            sv_ref[p] = zs[:, C_I + p * pair:C_I + (p + 1) * pair]
        sglu_ref[...] = zs[:, C_A:C_B] * jax.nn.sigmoid(zs[:, C_B:C_END])
        sgate_ref[...] = _silu(zs[:, C_G:C_A])

    @pl.when(mixing & (t == 0))
    def _start_of_batch_row():
        st_ref[...] = mst_ref[...]
        e_ref[0:HIST - N_META, :] = jnp.zeros((HIST - N_META, CV_WIDTH), F32)
        e_ref[HIST - N_META:HIST, :] = mglu_ref[...]

    cb = cb_ref[...]
    z_ref[...] = _project(x_ref[0], n1g, win_ref)
    _sample_state_update(s, sst_ref, ssc_ref, snst_ref, snsc_ref, cw_ref, cb, sft_ref, sqt_ref, sv_ref, sglu_ref,
                         so_ref, sdw_ref)
    hn_ref[...] = _rms(h_ref[...], n2g_ref[...]).astype(BF16)
    y_out_ref[0] = h_ref[...]
    for r0 in range(0, tile_t, CUMSUM_ROWS):
        rows = slice(r0, r0 + CUMSUM_ROWS)
        f = lb + (1.0 - lb) * jax.nn.sigmoid(z_ref[rows, C_F:C_I])
        z_ref[rows, C_F:C_I] = f
        b_ref[rows, :] = _chunk_cumsum(jnp.log(f))
    causal = _tril(HG_CHUNK)
    og = og_ref[...]
    ff = wup_ref.shape[1] // (tile_t // HG_CHUNK)
    dt = MXU_COLS
    assert (CV_WIDTH // 128) * dt == y_out_ref.shape[2]

    conv_done = None
    for c in range(tile_t // HG_CHUNK):
        rows = slice(c * HG_CHUNK, (c + 1) * HG_CHUNK)
        f = z_ref[rows, C_F:C_I]
        k = 1.0 - f
        b = b_ref[rows, :]
        b_mid = b[HG_CHUNK // 2 - 1:HG_CHUNK // 2]
        b_end = b[HG_CHUNK - 1:HG_CHUNK]
        if conv_done is not None:
            b_mid = b_mid + conv_done
        q = _silu(z_ref[rows, C_Q:C_F])
        q_rel = (q * jnp.exp(b - b_mid)).astype(BF16)
        k_rel = (k * jnp.exp(b_mid - b)).astype(BF16)
        q_abs = (q * jnp.exp(b)).astype(BF16)
        k_end = (k * jnp.exp(b_end - b)).astype(BF16)
        decay = jnp.exp(b_end)
        v = z_ref[rows, C_I:C_G].astype(BF16)
        gate = _silu(z_ref[rows, C_G:C_A])
        for h in range(HG_HEADS):
            hs = _head(h)
            scores = jnp.where(causal, _dot_nt(q_rel[:, hs], k_rel[:, hs]), 0.0)
            s_t = st_ref[h]
            o = (jnp.dot(scores.astype(BF16), v[:, hs], preferred_element_type=F32)
                 + _dot_nt(q_abs[:, hs], s_t.astype(BF16)))
            st_ref[h] = s_t * decay[:, hs] + _dot_tn(v[:, hs], k_end[:, hs])
            y_ref[rows, hs] = (_rms(o, og) * gate[:, hs]).astype(BF16)
        for n in range(ff // MXU_COLS):
            ucols = slice(c * ff + n * MXU_COLS, c * ff + (n + 1) * MXU_COLS)
            u = jnp.dot(hn_ref[...], wup_ref[:, ucols], preferred_element_type=F32)
            a_ref[:, n * MXU_COLS:(n + 1) * MXU_COLS] = jnp.square(jnp.maximum(u, 0.0)).astype(BF16)
        e_ref[HIST + c * HG_CHUNK:HIST + (c + 1) * HG_CHUNK, :] = (z_ref[rows, C_A:C_B]
                                                                   * jax.nn.sigmoid(z_ref[rows, C_B:C_END]))
        dw = []
        for g in range(CV_WIDTH // 128):
            dw.append(_dwconv_lane_group(e_ref, cw_ref, cb, HIST + c * HG_CHUNK, HG_CHUNK, g))
            dcols = slice(g * dt, (g + 1) * dt)
            y_out_ref[0, :, dcols] += jnp.dot(a_ref[...], wdn_ref[c * ff:(c + 1) * ff, dcols],
                                              preferred_element_type=F32)
        dw = jnp.concatenate(dw, axis=-1)
        y_ref[rows, HG_WIDTH:] = _silu(_layernorm(dw, lng_ref[...], lnb_ref[...])).astype(BF16)
        conv_done = _derived_zero(dw[0:8, :])[0:1]

    y_out_ref[0] = _rms(y_out_ref[0], fg_ref[...])
    h_ref[...] = x_ref[0] + jnp.dot(y_ref[...], wout_ref[...], preferred_element_type=F32)
    e_ref[0:HIST, :] = e_ref[tile_t:tile_t + HIST, :]

    @pl.when(mixing & (t == nt - 1))
    def _final_state():
        for h in range(HG_HEADS):
            sfin_ref[0, h] = st_ref[h].T
        cfin_ref[0] = e_ref[tile_t + HIST - (CONV_K - 1):tile_t + HIST, :]

    @pl.when(s == ns)
    def _sample_outputs():
        y_hg = []
        for h in range(HG_HEADS):
            p, hh = divmod(h, SAMPLE_HEADS)
            o_h = so_ref[p, :, hh * HG_DV:(hh + 1) * HG_DV]
            y_hg.append((_rms(o_h, og) * sgate_ref[:, _head(h)]).astype(BF16))
        y_cv = _silu(_layernorm(sdw_ref[...], lng_ref[...], lnb_ref[...])).astype(BF16)
        ycat = jnp.concatenate(y_hg + [y_cv], axis=-1)
        hs = sx_ref[...] + jnp.dot(ycat, wout_ref[...], preferred_element_type=F32)
        ys_ref[...] = _mlp(hs, n2g_ref[...], wup_ref, wdn_ref, fg_ref[...])


def _weight_slots(stage_ref):
    per_row = stage_ref.shape[1] // W_COLS
    n_slot = (stage_ref.shape[0] // W_ROWS) * per_row

    def slot(k):
        r, c = divmod(k, per_row)
        return stage_ref.at[pl.ds(r * W_ROWS, W_ROWS), pl.ds(c * W_COLS, W_COLS)]

    return n_slot, slot


def _load_weights_as_bf16(pairs, stage_ref, sem):
    n_slot, slot = _weight_slots(stage_ref)
    blocks = [(src, dst, r, c) for src, dst in pairs
              for r in range(0, dst.shape[0], W_ROWS) for c in range(0, dst.shape[1], W_COLS)]

    def copy(i):
        src, _, r, c = blocks[i]
        k = i % n_slot
        return pltpu.make_async_copy(src.at[0, pl.ds(r, W_ROWS), pl.ds(c, W_COLS)], slot(k), sem.at[k])

    for i in range(min(n_slot, len(blocks))):
        copy(i).start()
    for i, (_, dst, r, c) in enumerate(blocks):
        copy(i).wait()
        dst[r:r + W_ROWS, c:c + W_COLS] = slot(i % n_slot)[...].astype(BF16)
        if i + n_slot < len(blocks):
            copy(i + n_slot).start()


def _sample_units(s, nseq):
    n_pairs = HG_HEADS // SAMPLE_HEADS
    unit = jnp.minimum(s, nseq // SAMPLE_SEQS * n_pairs - 1)
    return unit // n_pairs, unit % n_pairs, jnp.minimum(s, nseq // SAMPLE_SEQS - 1)


def _sample_state_update(s, sst_ref, ssc_ref, snst_ref, snsc_ref, cw_ref, cb, sft_ref, sqt_ref, sv_ref, sglu_ref,
                         so_ref, sdw_ref):
    nseq = sft_ref.shape[2]
    sb, hp, cblk = _sample_units(s, nseq)
    r0 = pl.multiple_of(sb * SAMPLE_SEQS, SAMPLE_SEQS)
    shift = lax.rem(nseq - r0, nseq)
    v_rows = sv_ref[hp, pl.ds(r0, SAMPLE_SEQS), :]
    seq_row = lax.broadcasted_iota(jnp.int32, (SAMPLE_SEQS, HG_DV), 0)
    for hh in range(SAMPLE_HEADS):
        h = hp * SAMPLE_HEADS + hh
        f_t = pltpu.roll(sft_ref[h], shift, 1)
        q_t = pltpu.roll(sqt_ref[h], shift, 1)
        o = jnp.zeros((SAMPLE_SEQS, HG_DV), F32)
        for j in range(SAMPLE_SEQS):
            f_b = jnp.broadcast_to(f_t[:, j:j + 1], (HG_DK, HG_DV))
            s_new = f_b * sst_ref[j, hh] + (1.0 - f_b) * v_rows[j:j + 1, hh * HG_DV:(hh + 1) * HG_DV]
            snst_ref[j, hh] = s_new
            q_b = jnp.broadcast_to(q_t[:, j:j + 1], (HG_DK, HG_DV))
            o = jnp.where(seq_row == j, jnp.sum(q_b * s_new, axis=0, keepdims=True), o)
        so_ref[hp, pl.ds(r0, SAMPLE_SEQS), hh * HG_DV:(hh + 1) * HG_DV] = o
    c0 = pl.multiple_of(cblk * SAMPLE_SEQS, SAMPLE_SEQS)
    glu = sglu_ref[pl.ds(c0, SAMPLE_SEQS), :]
    taps = CONV_K - 1
    dw = cb + cw_ref[taps:taps + 1, :] * glu
    for j in range(taps):
        dw = dw + cw_ref[j:j + 1, :] * ssc_ref[j]
    sdw_ref[pl.ds(c0, SAMPLE_SEQS), :] = dw
    snsc_ref[0:taps - 1] = ssc_ref[1:taps]
    snsc_ref[taps - 1] = glu


def _const_spec(shape, single=False):
    zeros = (0,) * len(shape)
    if single:
        return pl.BlockSpec(shape, lambda *_: zeros, pipeline_mode=pl.Buffered(1))
    return pl.BlockSpec(shape, lambda *_: zeros)


def _layer(x, meta, lbp, n1g, win, og, cw, cb, lng, lnb, wout, n2g, wup, wdn, fg, xs, sst, ssc):
    bsz, seq, d = x.shape
    nt = seq // TILE_T
    ns = bsz * nt
    nseq = xs.shape[0]
    n_pairs = HG_HEADS // SAMPLE_HEADS
    assert nseq // SAMPLE_SEQS * n_pairs <= ns, "one (sequence block, head pair) per grid step"

    def mix_tile(s):
        s1 = jnp.minimum(s, ns - 1)
        return s1 // nt, s1 % nt

    def mlp_tile(s):
        s2 = jnp.maximum(s - 1, 0)
        return s2 // nt, s2 % nt

    state_spec = pl.BlockSpec((SAMPLE_SEQS, SAMPLE_HEADS, HG_DK, HG_DV),
                              lambda s: (*_sample_units(s, nseq)[:2], 0, 0))
    conv_spec = pl.BlockSpec((CONV_K - 1, SAMPLE_SEQS, CV_WIDTH), lambda s: (0, _sample_units(s, nseq)[2], 0))
    pair = SAMPLE_HEADS * HG_DV
    hbm = pl.BlockSpec(memory_space=pl.ANY)
    n_slot = (TILE_T // W_ROWS) * (C_END // W_COLS)

    return pl.pallas_call(
        functools.partial(_layer_kernel, tile_t=TILE_T, nt=nt),
        grid=(ns + 1,),
        in_specs=[
            pl.BlockSpec((1, TILE_T, d), lambda s: (*mix_tile(s), 0)),
            _const_spec(meta.shape), _const_spec(lbp.shape), _const_spec(n1g.shape),
            hbm, _const_spec(og.shape), _const_spec(cw.shape),
            _const_spec(cb.shape), _const_spec(lng.shape), _const_spec(lnb.shape),
            hbm, _const_spec(n2g.shape), hbm, hbm, _const_spec(fg.shape),
            _const_spec(xs.shape, single=True), state_spec, conv_spec,
        ],
        out_specs=[
            pl.BlockSpec((1, TILE_T, d), lambda s: (*mlp_tile(s), 0)),
            pl.BlockSpec((1, HG_HEADS, HG_DK, HG_DV), lambda s: (mix_tile(s)[0], 0, 0, 0)),
            pl.BlockSpec((1, CONV_K - 1, CV_WIDTH), lambda s: (mix_tile(s)[0], 0, 0)),
            state_spec, conv_spec, _const_spec(xs.shape),
        ],
        out_shape=[
            jax.ShapeDtypeStruct((bsz, seq, d), F32),
            jax.ShapeDtypeStruct((bsz, HG_HEADS, HG_DK, HG_DV), F32),
            jax.ShapeDtypeStruct((bsz, CONV_K - 1, CV_WIDTH), F32),
            jax.ShapeDtypeStruct(sst.shape, F32),
            jax.ShapeDtypeStruct(ssc.shape, F32),
            jax.ShapeDtypeStruct(xs.shape, F32),
        ],
        scratch_shapes=[
            pltpu.VMEM(win.shape[1:], BF16), pltpu.VMEM(wout.shape[1:], BF16),
            pltpu.VMEM(wup.shape[1:], BF16), pltpu.VMEM(wdn.shape[1:], BF16),
            pltpu.SemaphoreType.DMA((n_slot,)),
            pltpu.VMEM((TILE_T, C_END), F32),
            pltpu.VMEM((TILE_T, HG_WIDTH), F32),
            pltpu.VMEM((TILE_T, HG_WIDTH + CV_WIDTH), BF16),
            pltpu.VMEM((HG_HEADS, HG_DV, HG_DK), F32),
            pltpu.VMEM((HG_HEADS, HG_DV, HG_DK), F32),
            pltpu.VMEM((N_META, CV_WIDTH), F32),
            pltpu.VMEM((HIST + TILE_T, CV_WIDTH), F32),
            pltpu.VMEM((TILE_T, d), F32),
            pltpu.VMEM((TILE_T, d), BF16),
            pltpu.VMEM((TILE_T, wup.shape[2] // (TILE_T // HG_CHUNK)), BF16),
            pltpu.VMEM((HG_HEADS, HG_DK, nseq), F32),
            pltpu.VMEM((HG_HEADS, HG_DK, nseq), F32),
            pltpu.VMEM((n_pairs, nseq, pair), F32),
            pltpu.VMEM((nseq, CV_WIDTH), F32),
            pltpu.VMEM((nseq, HG_WIDTH), F32),
            pltpu.VMEM((n_pairs, nseq, pair), F32),
            pltpu.VMEM((nseq, CV_WIDTH), F32),
        ],
        compiler_params=pltpu.CompilerParams(dimension_semantics=("arbitrary",), vmem_limit_bytes=VMEM_LIMIT),
        name="layer",
    )(x, meta, lbp, n1g, win, og, cw, cb, lng, lnb, wout, n2g, wup, wdn, fg, xs, sst, ssc)


def kernel(x_prompt, x_sample, state_hgrn, state_conv, meta_tokens, hg_lb, norm1_g, w_in, hg_onorm_g, conv_w, conv_b,
           conv_ln_g, conv_ln_b, w_out, norm2_g, w_up, w_down, final_g):
    assert state_hgrn.shape[0] == 1, "single-layer stack"
    bsz, seq, d = x_prompt.shape
    row = lambda a: a.reshape(1, -1)
    n1g, og, cb, lng, lnb, n2g, fg = (row(norm1_g[0]), row(hg_onorm_g[0]), row(conv_b[0]), row(conv_ln_g[0]),
                                      row(conv_ln_b[0]), row(norm2_g[0]), row(final_g))
    cw = conv_w[0]

    assert x_sample.shape[1] == 1, "one new token per running sequence"
    x_s = x_sample.reshape(x_sample.shape[0], d)
    y_p, s_p, c_p, s_s, c_s, y_s = _layer(x_prompt, meta_tokens, hg_lb, n1g, w_in, og, cw, cb, lng, lnb, w_out, n2g,
                                          w_up, w_down, fg, x_s, state_hgrn[0], jnp.swapaxes(state_conv[0], 0, 1))

    return (y_p, y_s.reshape(x_sample.shape), s_p[None], c_p[None], s_s[None], jnp.swapaxes(c_s, 0, 1)[None])
```
